```python
import jax
import jax.numpy as jnp
from jax import lax
import numpy as np

D_MODEL = 1024
BATCH = 4
SEQ = 4096
DEPTH = 2
DEC_BATCH = 32
DEC_SEQ = 8
PAST_LEN = 16384
PAGE_SIZE = 128

HEAD_DIM = 64
N_HEADS_A = 8
D_A = N_HEADS_A * HEAD_DIM
N_HEADS_B = 4
D_B = N_HEADS_B * HEAD_DIM
N_GROUPS_C = 4
D_C = N_GROUPS_C * HEAD_DIM
GROUP_C = D_C // N_GROUPS_C
D_MIX = D_A + D_B + D_C
D_IN = 3 * D_A + 2 * D_B + 2 * D_C
SPLITS = (D_A, 2 * D_A, 3 * D_A, 3 * D_A + D_B, 3 * D_A + 2 * D_B, 3 * D_A + 2 * D_B + D_C)
DILATED_PATTERNS = ((128, 1), (512, 4), (2048, 16))
WIN_MAX = 2048
Q_BLOCK = 128
CHUNK = 128
CONV_W = 4
LRU_C = 8.0
N_EXPERTS = 32
TOP_K = 4
D_FF = 1024
SWIGLU_LIMIT = 7.0
SWIGLU_ALPHA = 1.702
MOE_BLOCK = 128
EPS = 1e-6
ATTN_SCALE = HEAD_DIM ** -0.5

kernel_name = "hymba_dilated_gmlp_rglru_moe_step"


def rms_norm(x, g):
    xf = x.astype(jnp.float32)
    y = xf * lax.rsqrt(jnp.mean(xf * xf, axis=-1, keepdims=True) + EPS)
    return (y * g.astype(jnp.float32)).astype(x.dtype)


def dilated_attention(q, q_pos, k_all, v_all, key_start):
    n_keys = k_all.shape[1]
    ms, nums, dens = [], [], []
    for window, dil in DILATED_PATTERNS:
        offs = dil * jnp.arange(window // dil + 1, dtype=jnp.int32)
        kpos = q_pos[:, None] - offs[None, :]
        valid = kpos >= key_start
        idx = jnp.clip(kpos - key_start, 0, n_keys - 1)
        kg = jnp.take(k_all, idx, axis=1)
        vg = jnp.take(v_all, idx, axis=1)
        s = jnp.einsum('bqhd,bqjhd->bhqj', q, kg, preferred_element_type=jnp.float32)
        s = jnp.where(valid[None, None], s, -jnp.inf)
        m = jnp.max(s, axis=-1)
        p = jnp.exp(s - m[..., None])
        ms.append(m)
        dens.append(jnp.sum(p, axis=-1))
        nums.append(jnp.einsum('bhqj,bqjhd->bhqd', p, vg.astype(jnp.float32)))
    m_all = jnp.max(jnp.stack(ms, axis=0), axis=0)
    wts = [jnp.exp(m - m_all) for m in ms]
    num = sum(w[..., None] * n for w, n in zip(wts, nums))
    den = sum(w * d for w, d in zip(wts, dens))
    o = num / den[..., None]
    return o.transpose(0, 2, 1, 3).astype(q.dtype)


def prompt_attention(q, k, v):
    B, S, H, Dh = q.shape
    nb = S // Q_BLOCK
    qb = q.reshape(B, nb, Q_BLOCK, H, Dh).transpose(1, 0, 2, 3, 4)
    pos = jnp.arange(S, dtype=jnp.int32).reshape(nb, Q_BLOCK)
    out = lax.map(lambda a: dilated_attention(a[0], a[1], k, v, 0), (qb, pos))
    return out.transpose(1, 0, 2, 3, 4).reshape(B, S, H, Dh)


def spatial_gate(ub, vb, w_s, b_s):
    B, T, _ = vb.shape
    n = min(T, CHUNK)
    vc = vb.reshape(B, T // n, n, N_HEADS_B, HEAD_DIM)
    w = jnp.tril(w_s[:, :n, :n])
    mixed = jnp.einsum('hij,bcjhd->bcihd', w, vc) + b_s[:, :n].T[None, None, :, :, None]
    return ub * mixed.reshape(B, T, D_B)


def rg_lru_mixer(xc, gc, conv_buf, h0, conv_w, conv_b, w_a, b_a, w_x, b_x, lru_lambda):
    B, T, _ = xc.shape
    xpad = jnp.concatenate([conv_buf.astype(xc.dtype), xc], axis=1)
    xconv = conv_b + sum(xpad[:, k:k + T] * conv_w[k] for k in range(CONV_W))
    xf = xconv.astype(jnp.float32)
    xg = xf.reshape(B, T, N_GROUPS_C, GROUP_C)
    r = jax.nn.sigmoid(jnp.einsum('btgi,gij->btgj', xg, w_a.astype(jnp.float32)).reshape(B, T, D_C)
                       + b_a.astype(jnp.float32))
    i = jax.nn.sigmoid(jnp.einsum('btgi,gij->btgj', xg, w_x.astype(jnp.float32)).reshape(B, T, D_C)
                       + b_x.astype(jnp.float32))
    log_a = -LRU_C * r * jax.nn.softplus(-lru_lambda.astype(jnp.float32))
    a = jnp.exp(log_a)
    b = jnp.sqrt(-jnp.expm1(2.0 * log_a)) * (i * xf)
    b = b.at[:, 0].add(a[:, 0] * h0.astype(jnp.float32))

    def combine(c1, c2):
        a1, b1 = c1
        a2, b2 = c2
        return a1 * a2, a2 * b1 + b2

    _, h = lax.associative_scan(combine, (a, b), axis=1)
    y = (h * jax.nn.gelu(gc.astype(jnp.float32))).astype(xc.dtype)
    return y, xpad[:, -(CONV_W - 1):], h[:, -1]


def moe_ffn(h, w_router, b_router, w_gu, b_gu, w_down, b_down):
    B, T, D = h.shape
    n_tok = B * T
    xt = h.reshape(n_tok, D)
    logits = jnp.einsum('nd,de->ne', xt, w_router, preferred_element_type=jnp.float32) + b_router.astype(jnp.float32)
    top_logit, top_idx = lax.top_k(logits, TOP_K)
    gates = jax.nn.softmax(top_logit, axis=-1)
    n_assign = n_tok * TOP_K
    flat_e = top_idx.reshape(-1).astype(jnp.int32)
    flat_tok = jnp.arange(n_assign, dtype=jnp.int32) // TOP_K
    flat_gate = gates.reshape(-1)
    order = jnp.argsort(flat_e)
    e_sorted = flat_e[order]
    counts = jnp.zeros((N_EXPERTS,), jnp.int32).at[flat_e].add(1)
    padded = (counts + MOE_BLOCK - 1) // MOE_BLOCK * MOE_BLOCK
    pad_end = jnp.cumsum(padded)
    pad_start = pad_end - padded
    start = jnp.cumsum(counts) - counts
    dest = pad_start[e_sorted] + (jnp.arange(n_assign, dtype=jnp.int32) - start[e_sorted])
    n_blocks = -(-n_assign // MOE_BLOCK) + N_EXPERTS
    n_slots = n_blocks * MOE_BLOCK
    slot_tok = jnp.full((n_slots,), n_tok, jnp.int32).at[dest].set(flat_tok[order])
    slot_gate = jnp.zeros((n_slots,), jnp.float32).at[dest].set(flat_gate[order])
    block_expert = jnp.minimum(
        jnp.searchsorted(pad_end, jnp.arange(n_blocks, dtype=jnp.int32) * MOE_BLOCK, side='right'),
        N_EXPERTS - 1)
    x_pad = jnp.concatenate([xt, jnp.zeros((1, D), xt.dtype)], axis=0)

    def run_block(args):
        tok, g, e = args
        z = x_pad[tok] @ w_gu[e] + b_gu[e]
        x_glu = jnp.minimum(z[:, ::2], SWIGLU_LIMIT)
        x_lin = jnp.clip(z[:, 1::2], -SWIGLU_LIMIT, SWIGLU_LIMIT)
        act = x_glu * jax.nn.sigmoid(SWIGLU_ALPHA * x_glu) * (x_lin + 1)
        return (act @ w_down[e] + b_down[e]).astype(jnp.float32) * g[:, None]

    out = lax.map(run_block, (slot_tok.reshape(n_blocks, MOE_BLOCK),
                              slot_gate.reshape(n_blocks, MOE_BLOCK), block_expert))
    y = jnp.zeros((n_tok + 1, D), jnp.float32).at[slot_tok].add(out.reshape(n_slots, D))
    return y[:n_tok].reshape(B, T, D).astype(h.dtype)


def trunk_layer(x, attend, conv_buf, h0, ln1_g, w_in, g_q, g_k, g_vb, w_s, b_s, conv_w, conv_b,
                w_a, b_a, w_x, b_x, lru_lambda, g_out, w_out, ln2_g, w_router, b_router,
                w_gu, b_gu, w_down, b_down):
    B, T, _ = x.shape
    z = rms_norm(x, ln1_g) @ w_in
    q, k, v, ub, vb, xc, gc = jnp.split(z, SPLITS, axis=-1)
    q = rms_norm(q.reshape(B, T, N_HEADS_A, HEAD_DIM), g_q) * ATTN_SCALE
    k = rms_norm(k.reshape(B, T, N_HEADS_A, HEAD_DIM), g_k)
    v = v.reshape(B, T, N_HEADS_A, HEAD_DIM)
    vb = rms_norm(vb, g_vb)
    o_a = attend(q, k, v).reshape(B, T, D_A)
    o_b = spatial_gate(ub, vb, w_s, b_s)
    o_c, conv_new, h_new = rg_lru_mixer(xc, gc, conv_buf, h0, conv_w, conv_b, w_a, b_a, w_x, b_x, lru_lambda)
    o = jnp.concatenate([rms_norm(o_a, g_out[:D_A]),
                         rms_norm(o_b, g_out[D_A:D_A + D_B]),
                         rms_norm(o_c, g_out[D_A + D_B:])], axis=-1)
    x = x + o @ w_out
    x = x + moe_ffn(rms_norm(x, ln2_g), w_router, b_router, w_gu, b_gu, w_down, b_down)
    return x, k, v, vb, conv_new, h_new


def setup_inputs(seed: int = 0) -> dict:
    key = jax.random.key(seed)
    ks = list(jax.random.split(key, 32))

    def nrm(shape, scale):
        return scale * jax.random.normal(ks.pop(), shape, jnp.float32)

    def gain(shape):
        return 1.0 + nrm(shape, 0.05)

    w_buf = min(WIN_MAX, PAST_LEN)
    a0 = jax.random.uniform(ks.pop(), (DEPTH, D_C), jnp.float32, minval=0.9, maxval=0.999)
    a_base = a0 ** (1.0 / LRU_C)
    lru_lambda = jnp.log(a_base) - jnp.log1p(-a_base)
    return {
        "x_prompt": nrm((BATCH, SEQ, D_MODEL), 1.0),
        "x_sample": nrm((DEC_BATCH, DEC_SEQ, D_MODEL), 1.0),
        "cache_win_k": nrm((DEPTH, DEC_BATCH, w_buf, N_HEADS_A, HEAD_DIM), 1.0),
        "cache_win_v": nrm((DEPTH, DEC_BATCH, w_buf, N_HEADS_A, HEAD_DIM), 1.0),
        "state_conv": nrm((DEPTH, DEC_BATCH, CONV_W - 1, D_C), 1.0),
        "state_lru": nrm((DEPTH, DEC_BATCH, D_C), 0.5),
        "ln1_g": gain((DEPTH, D_MODEL)),
        "w_in": nrm((DEPTH, D_MODEL, D_IN), D_MODEL ** -0.5),
        "g_q": gain((DEPTH, HEAD_DIM)),
        "g_k": gain((DEPTH, HEAD_DIM)),
        "g_vb": gain((DEPTH, D_B)),
        "w_s": nrm((DEPTH, N_HEADS_B, CHUNK, CHUNK), CHUNK ** -0.5),
        "b_s": gain((DEPTH, N_HEADS_B, CHUNK)),
        "conv_w": nrm((DEPTH, CONV_W, D_C), CONV_W ** -0.5),
        "conv_b": nrm((DEPTH, D_C), 0.05),
        "w_a": nrm((DEPTH, N_GROUPS_C, GROUP_C, GROUP_C), GROUP_C ** -0.5),
        "b_a": nrm((DEPTH, D_C), 0.1),
        "w_x": nrm((DEPTH, N_GROUPS_C, GROUP_C, GROUP_C), GROUP_C ** -0.5),
        "b_x": nrm((DEPTH, D_C), 0.1),
        "lru_lambda": lru_lambda,
        "g_out": gain((DEPTH, D_MIX)),
        "w_out": nrm((DEPTH, D_MIX, D_MODEL), D_MIX ** -0.5),
        "ln2_g": gain((DEPTH, D_MODEL)),
        "w_router": nrm((DEPTH, D_MODEL, N_EXPERTS), D_MODEL ** -0.5),
        "b_router": nrm((DEPTH, N_EXPERTS), 0.01),
        "w_gu": nrm((DEPTH, N_EXPERTS, D_MODEL, 2 * D_FF), D_MODEL ** -0.5),
        "b_gu": nrm((DEPTH, N_EXPERTS, 2 * D_FF), 0.02),
        "w_down": nrm((DEPTH, N_EXPERTS, D_FF, D_MODEL), D_FF ** -0.5),
        "b_down": nrm((DEPTH, N_EXPERTS, D_MODEL), 0.02),
    }


def reference(x_prompt, x_sample, cache_win_k, cache_win_v, state_conv, state_lru,
              ln1_g, w_in, g_q, g_k, g_vb, w_s, b_s, conv_w, conv_b, w_a, b_a, w_x, b_x,
              lru_lambda, g_out, w_out, ln2_g, w_router, b_router, w_gu, b_gu, w_down, b_down):
    xp, xs = x_prompt, x_sample
    bp, sp = xp.shape[0], xp.shape[1]
    bs, ss = xs.shape[0], xs.shape[1]
    w_buf = cache_win_k.shape[2]
    key_start = PAST_LEN - w_buf
    s_pos = PAST_LEN + jnp.arange(ss, dtype=jnp.int32)
    keep = min(WIN_MAX, sp)
    pk, pv, pconv, plru, sk, sv, sconv, slru, svb = [], [], [], [], [], [], [], [], []
    for l in range(DEPTH):
        lp = (ln1_g[l], w_in[l], g_q[l], g_k[l], g_vb[l], w_s[l], b_s[l], conv_w[l], conv_b[l],
              w_a[l], b_a[l], w_x[l], b_x[l], lru_lambda[l], g_out[l], w_out[l], ln2_g[l],
              w_router[l], b_router[l], w_gu[l], b_gu[l], w_down[l], b_down[l])
        xp, k_p, v_p, _, conv_p, h_p = trunk_layer(
            xp, prompt_attention,
            jnp.zeros((bp, CONV_W - 1, D_C), xp.dtype), jnp.zeros((bp, D_C), jnp.float32), *lp)
        pk.append(k_p[:, sp - keep:])
        pv.append(v_p[:, sp - keep:])
        pconv.append(conv_p)
        plru.append(h_p)
        ck, cv = cache_win_k[l], cache_win_v[l]

        def sample_attend(q, k, v, ck=ck, cv=cv):
            return dilated_attention(q, s_pos, jnp.concatenate([ck, k], axis=1),
                                     jnp.concatenate([cv, v], axis=1), key_start)

        xs, k_s, v_s, vb_s, conv_s, h_s = trunk_layer(xs, sample_attend, state_conv[l], state_lru[l], *lp)
        sk.append(k_s)
        sv.append(v_s)
        sconv.append(conv_s)
        slru.append(h_s)
        svb.append(vb_s)
    return (xp, xs, jnp.stack(pk), jnp.stack(pv), jnp.stack(pconv), jnp.stack(plru),
            jnp.stack(sk), jnp.stack(sv), jnp.stack(sconv), jnp.stack(slru), jnp.stack(svb))
```

```python
import functools

import jax
import jax.numpy as jnp
from jax import lax
from jax.experimental import pallas as pl
from jax.experimental.pallas import tpu as pltpu

F32 = jnp.float32
BF16 = jnp.bfloat16

D_MODEL = 1024
HEAD_DIM = 64
N_HEADS_A = 8
D_A = N_HEADS_A * HEAD_DIM
N_HEADS_B = 4
D_B = N_HEADS_B * HEAD_DIM
N_GROUPS_C = 4
D_C = N_GROUPS_C * HEAD_DIM
D_IN = 3 * D_A + 2 * D_B + 2 * D_C
DILATED_PATTERNS = ((128, 1), (512, 4), (2048, 16))
N_PATTERNS = len(DILATED_PATTERNS)
CHUNK = 128
CONV_W = 4
LRU_C = 8.0
N_EXPERTS = 32
TOP_K = 4
D_FF = 1024
SWIGLU_LIMIT = 7.0
SWIGLU_ALPHA = 1.702
EPS = 1e-6
ATTN_SCALE = HEAD_DIM ** -0.5
PAST_LEN = 16384

LANES = 128
SUBLANES = 8
VMEM_LIMIT_BYTES = 56 * 1024 * 1024

Q_BLOCK = 128
ATTN_SPAN = 2048
TOKEN_TILE = 256
MIX_TILE = 512
MOE_TILE = 256
ROUTER_PAD = LANES
NEG_BIG = -1e30


def _cparams(semantics):
    return pltpu.CompilerParams(dimension_semantics=semantics,
                                vmem_limit_bytes=VMEM_LIMIT_BYTES)


def _full(shape):
    return pl.BlockSpec(shape, lambda *_: (0,) * len(shape))


def _rms(t, g):
    ms = jnp.mean(t * t, axis=-1, keepdims=True)
    return t * lax.rsqrt(ms + EPS) * g


def _split_bf16(t):
    hi = t.astype(BF16)
    lo = (t - hi.astype(F32)).astype(BF16)
    return hi, lo


def _in_proj_kernel(x_ref, g1_ref, w_ref, gq_ref, gk_ref, gvb_ref, ones_ref,
                    q_ref, k_ref, v_ref, bc_ref):
    h = _rms(x_ref[...], g1_ref[...]).astype(BF16)
    z = jnp.dot(h, w_ref[...], preferred_element_type=F32)

    def head_norm(t, g):
        hi, lo = _split_bf16(t * t)
        ss = (jnp.dot(hi, ones_ref[...], preferred_element_type=F32)
              + jnp.dot(lo, ones_ref[...], preferred_element_type=F32))
        return t * lax.rsqrt(ss * (1.0 / HEAD_DIM) + EPS) * g

    q_ref[...] = head_norm(z[:, 0:D_A], gq_ref[...]) * ATTN_SCALE
    k_ref[...] = head_norm(z[:, D_A:2 * D_A], gk_ref[...])
    v_ref[...] = z[:, 2 * D_A:3 * D_A]
    o = 3 * D_A
    bc_ref[...] = z[:, o:]
    bc_ref[:, D_B:2 * D_B] = _rms(z[:, o + D_B:o + 2 * D_B], gvb_ref[...])


def _in_proj(x, g1, w_bf16, gq, gk, gvb, ones_bd):
    n = x.shape[0]
    tm = TOKEN_TILE
    row = lambda w: pl.BlockSpec((tm, w), lambda i: (i, 0))
    return pl.pallas_call(
        _in_proj_kernel,
        grid=(n // tm,),
        in_specs=[row(D_MODEL), _full((1, D_MODEL)), _full((D_MODEL, D_IN)), _full((1, D_A)),
                  _full((1, D_A)), _full((1, D_B)), _full((D_A, D_A))],
        out_specs=[row(D_A), row(D_A), row(D_A), row(2 * D_B + 2 * D_C)],
        out_shape=[jax.ShapeDtypeStruct((n, D_A), F32)] * 3
        + [jax.ShapeDtypeStruct((n, 2 * D_B + 2 * D_C), F32)],
        compiler_params=_cparams(("parallel",)),
        name="in_proj",
    )(x, g1, w_bf16, gq, gk, gvb, ones_bd)


def _attn_prompt_kernel(q_ref, kp_ref, kc_ref, vp_ref, vc_ref, o_ref, kk, vv, m_s, l_s, a_s):
    span = pl.program_id(2)
    kk[0:ATTN_SPAN, :] = kp_ref[...]
    kk[ATTN_SPAN:, :] = kc_ref[...]
    vv[0:ATTN_SPAN, :] = vp_ref[...]
    vv[ATTN_SPAN:, :] = vc_ref[...]

    qb2 = 2 * Q_BLOCK
    lane = lax.broadcasted_iota(jnp.int32, (Q_BLOCK, LANES), 1)
    head0 = lane < HEAD_DIM
    row = lax.broadcasted_iota(jnp.int32, (qb2, qb2), 0) & (Q_BLOCK - 1)
    col = lax.broadcasted_iota(jnp.int32, (qb2, qb2), 1)
    band = (col >= row) & (col <= row + Q_BLOCK)
    cur = col >= Q_BLOCK

    def ds(start, size, d):
        return pl.ds(start, size) if d == 1 else pl.ds(start, size, stride=d)

    for p, (_, d) in enumerate(DILATED_PATTERNS):
        nblk = ATTN_SPAN // (Q_BLOCK * d)

        def body(it, carry, p=p, d=d, nblk=nblk):
            r = it // nblk
            ib = it % nblk
            qstart = r + d * Q_BLOCK * ib
            kstart = ATTN_SPAN + qstart - d * Q_BLOCK
            qb = q_ref[ds(qstart, Q_BLOCK, d), :]
            q2 = jnp.concatenate([jnp.where(head0, qb, 0.0), jnp.where(head0, 0.0, qb)], axis=0)
            qh, ql = _split_bf16(q2)
            kh, kl = _split_bf16(kk[ds(kstart, qb2, d), :])
            s = lax.dot_general(jnp.concatenate([qh, ql, qh], axis=1),
                                jnp.concatenate([kh, kh, kl], axis=1),
                                (((1,), (1,)), ((), ())), preferred_element_type=F32)
            prev_ok = jnp.logical_or(ib > 0, span > 0)
            s = jnp.where(band & (cur | prev_ok), s, -jnp.inf)
            m = jnp.max(s, axis=-1, keepdims=True)
            e = jnp.exp(s - m)
            l = jnp.sum(e, axis=-1, keepdims=True)
            eh, el = _split_bf16(e)
            vh, vl = _split_bf16(vv[ds(kstart, qb2, d), :])
            acc = jnp.dot(jnp.concatenate([eh, el, eh], axis=1),
                          jnp.concatenate([vh, vh, vl], axis=0),
                          preferred_element_type=F32)
            dst = ds(qstart, Q_BLOCK, d)
            shape = (Q_BLOCK, LANES)
            m_s[p, dst, :] = jnp.where(head0, jnp.broadcast_to(m[:Q_BLOCK], shape),
                                       jnp.broadcast_to(m[Q_BLOCK:], shape))
            l_s[p, dst, :] = jnp.where(head0, jnp.broadcast_to(l[:Q_BLOCK], shape),
                                       jnp.broadcast_to(l[Q_BLOCK:], shape))
            a_s[p, dst, :] = jnp.where(head0, acc[:Q_BLOCK], acc[Q_BLOCK:])
            return carry

        lax.fori_loop(0, ATTN_SPAN // Q_BLOCK, body, 0)

    def merge(c, carry):
        rows = pl.ds(pl.multiple_of(c * Q_BLOCK, Q_BLOCK), Q_BLOCK)
        ms = [m_s[p, rows, :] for p in range(N_PATTERNS)]
        m_all = functools.reduce(jnp.maximum, ms)
        ws = [jnp.exp(m - m_all) for m in ms]
        num = sum(w * a_s[p, rows, :] for p, w in enumerate(ws))
        den = sum(w * l_s[p, rows, :] for p, w in enumerate(ws))
        o_ref[rows, :] = num / den
        return carry

    lax.fori_loop(0, ATTN_SPAN // Q_BLOCK, merge, 0)


def _attn_prompt(q, k, v, batch, seq):
    nspan = seq // ATTN_SPAN
    blk = (ATTN_SPAN, LANES)
    cur = pl.BlockSpec(blk, lambda b, hp, s: (b * nspan + s, hp))
    prev = pl.BlockSpec(blk, lambda b, hp, s: (b * nspan + jnp.maximum(s - 1, 0), hp))
    acc = pltpu.VMEM((N_PATTERNS, ATTN_SPAN, LANES), F32)
    return pl.pallas_call(
        _attn_prompt_kernel,
        grid=(batch, D_A // LANES, nspan),
        in_specs=[cur, prev, cur, prev, cur],
        out_specs=cur,
        out_shape=jax.ShapeDtypeStruct((batch * seq, D_A), F32),
        scratch_shapes=[pltpu.VMEM((2 * ATTN_SPAN, LANES), F32), pltpu.VMEM((2 * ATTN_SPAN, LANES), F32),
                        acc, acc, acc],
        compiler_params=_cparams(("parallel", "parallel", "arbitrary")),
        name="attn_prompt",
    )(q, k, k, v, v)


def _attn_sample_kernel(q_ref, kn_ref, vn_ref, ck_ref, cv_ref, o_ref, *, w_buf, t_new):
    pad = LANES - t_new
    zeros = jnp.zeros((pad, D_A), F32)
    kall = jnp.concatenate([ck_ref[...], kn_ref[...], zeros], axis=0).astype(BF16)
    vall = jnp.concatenate([cv_ref[...], vn_ref[...], zeros], axis=0).astype(BF16)
    n_keys = w_buf + LANES
    n_rows = N_HEADS_A * t_new
    hrow = lax.broadcasted_iota(jnp.int32, (n_rows, D_A), 0) // t_new
    hlane = lax.broadcasted_iota(jnp.int32, (n_rows, D_A), 1) // HEAD_DIM
    own = hrow == hlane
    q_rep = jnp.concatenate([q_ref[...]] * N_HEADS_A, axis=0)
    q64 = jnp.where(own, q_rep, 0.0).astype(BF16)
    s = lax.dot_general(q64, kall, (((1,), (1,)), ((), ())), preferred_element_type=F32)
    t = lax.broadcasted_iota(jnp.int32, (n_rows, n_keys), 0) % t_new
    c = lax.broadcasted_iota(jnp.int32, (n_rows, n_keys), 1)
    dist = w_buf + t - c
    ms, ls, accs = [], [], []
    for w, d in DILATED_PATTERNS:
        ok = (dist >= 0) & (dist <= w) & ((dist & (d - 1)) == 0)
        sp = jnp.where(ok, s, -jnp.inf)
        m = jnp.max(sp, axis=-1, keepdims=True)
        e = jnp.exp(sp - m)
        ms.append(m)
        ls.append(jnp.sum(e, axis=-1, keepdims=True))
        accs.append(jnp.dot(e.astype(BF16), vall, preferred_element_type=F32))
    m_all = functools.reduce(jnp.maximum, ms)
    ws = [jnp.exp(m - m_all) for m in ms]
    num = sum(w * a for w, a in zip(ws, accs))
    den = sum(w * l for w, l in zip(ws, ls))
    o = jnp.where(own, num / den, 0.0)
    out = o[0:t_new]
    for h in range(1, N_HEADS_A):
        out = out + o[h * t_new:(h + 1) * t_new]
    o_ref[...] = out


def _attn_sample(q, k, v, cache_k, cache_v, layer, row0, batch, t_new):
    w_buf = cache_k.shape[2]
    assert row0 % t_new == 0 and t_new == SUBLANES
    new = pl.BlockSpec((t_new, D_A), lambda b: (row0 // t_new + b, 0))
    cache = pl.BlockSpec((None, None, w_buf, D_A), lambda b: (layer, b, 0, 0))
    return pl.pallas_call(
        functools.partial(_attn_sample_kernel, w_buf=w_buf, t_new=t_new),
        grid=(batch,),
        in_specs=[new, new, new, cache, cache],
        out_specs=pl.BlockSpec((t_new, D_A), lambda b: (b, 0)),
        out_shape=jax.ShapeDtypeStruct((batch * t_new, D_A), F32),
        compiler_params=_cparams(("parallel",)),
        name="attn_sample",
    )(q, k, v, cache_k, cache_v)


def _gelu_tanh(x):
    return 0.5 * x * (1.0 + jnp.tanh(0.7978845608028654 * (x + 0.044715 * x * x * x)))


def _mixer_bc_kernel(bc_ref, cb_ref, h0_ref, ws_ref, bs_ref, cw_ref, cbias_ref, wa_ref, ba_ref,
                     wx_ref, bx_ref, lam_ref, ob_ref, oc_ref, hl_ref, xp_s, h_s, *, tt, last_row):
    j = pl.program_id(1)

    @pl.when(j == 0)
    def _():
        xp_s[0:SUBLANES, :] = cb_ref[...]
        h_s[...] = h0_ref[...]

    nch = tt // CHUNK
    vcat = jnp.concatenate([bc_ref[c * CHUNK:(c + 1) * CHUNK, D_B:2 * D_B] for c in range(nch)],
                           axis=1).astype(BF16)
    ri = lax.broadcasted_iota(jnp.int32, (CHUNK, CHUNK), 0)
    ci = lax.broadcasted_iota(jnp.int32, (CHUNK, CHUNK), 1)
    hl = (lax.broadcasted_iota(jnp.int32, (CHUNK, nch * D_B), 1) % D_B) // HEAD_DIM
    mixed = jnp.zeros((CHUNK, nch * D_B), F32)
    for h in range(N_HEADS_B):
        wh = jnp.where(ri >= ci, ws_ref[h], 0.0).astype(BF16)
        mh = jnp.dot(wh, vcat, preferred_element_type=F32)
        mixed = mixed + jnp.where(hl == h, mh, 0.0)
    for c in range(nch):
        rows = slice(c * CHUNK, (c + 1) * CHUNK)
        ob_ref[rows, :] = bc_ref[rows, 0:D_B] * (mixed[:, c * D_B:(c + 1) * D_B] + bs_ref[...])

    xc = bc_ref[:, 2 * D_B:2 * D_B + D_C]
    xp_s[SUBLANES:SUBLANES + tt, :] = xc
    xconv = cbias_ref[...] + cw_ref[CONV_W - 1:CONV_W, :] * xc
    for kk in range(CONV_W - 1):
        off = SUBLANES - (CONV_W - 1) + kk
        xconv = xconv + cw_ref[kk:kk + 1, :] * xp_s[off:off + tt, :]
    xp_s[0:SUBLANES, :] = xp_s[tt:tt + SUBLANES, :]
    xb = xconv.astype(BF16)
    r = jax.nn.sigmoid(jnp.dot(xb, wa_ref[...], preferred_element_type=F32) + ba_ref[...])
    i = jax.nn.sigmoid(jnp.dot(xb, wx_ref[...], preferred_element_type=F32) + bx_ref[...])
    nl = -lam_ref[...]
    softplus = jnp.maximum(nl, 0.0) + jnp.log1p(jnp.exp(-jnp.abs(nl)))
    a = jnp.exp(-LRU_C * r * softplus)
    b = jnp.sqrt(1.0 - a * a) * (i * xconv)
    rowi = lax.broadcasted_iota(jnp.int32, (tt, D_C), 0)
    step = 1
    while step < tt:
        a_sh = pltpu.roll(a, step, axis=0)
        b_sh = pltpu.roll(b, step, axis=0)
        live = rowi >= step
        b = jnp.where(live, a * b_sh + b, b)
        a = jnp.where(live, a * a_sh, a)
        step *= 2
    h = a * h_s[...] + b
    h_s[...] = h[tt - 1:tt, :]
    oc_ref[...] = h * _gelu_tanh(bc_ref[:, 2 * D_B + D_C:])

    @pl.when(j == pl.num_programs(1) - 1)
    def _():
        hl_ref[...] = h[last_row:last_row + 1, :]


def _mixer_bc(bc, conv_buf8, h0, lw, batch, t_len, tt, last_row):
    nt = t_len // tt
    rows = lambda w: pl.BlockSpec((tt, w), lambda b, j: (b * nt + j, 0))
    per_b = lambda s: pl.BlockSpec((None,) + s, lambda b, j: (b,) + (0,) * len(s))
    return pl.pallas_call(
        functools.partial(_mixer_bc_kernel, tt=tt, last_row=last_row),
        grid=(batch, nt),
        in_specs=[rows(2 * D_B + 2 * D_C), per_b((SUBLANES, D_C)), per_b((1, D_C)),
                  _full((N_HEADS_B, CHUNK, CHUNK)), _full((CHUNK, D_B)), _full((CONV_W, D_C)),
                  _full((1, D_C)), _full((D_C, D_C)), _full((1, D_C)), _full((D_C, D_C)),
                  _full((1, D_C)), _full((1, D_C))],
        out_specs=[rows(D_B), rows(D_C), per_b((1, D_C))],
        out_shape=[jax.ShapeDtypeStruct((batch * t_len, D_B), F32),
                   jax.ShapeDtypeStruct((batch * t_len, D_C), F32),
                   jax.ShapeDtypeStruct((batch, 1, D_C), F32)],
        scratch_shapes=[pltpu.VMEM((tt + SUBLANES, D_C), F32), pltpu.VMEM((1, D_C), F32)],
        compiler_params=_cparams(("parallel", "arbitrary")),
        name="mixer_bc",
    )(bc, conv_buf8, h0, lw["w_s"], lw["bias_s"], lw["conv_w"], lw["conv_b"], lw["w_a_bd"],
      lw["b_a"], lw["w_x_bd"], lw["b_x"], lw["lru_lambda"])


def _out_proj_kernel(oa_ref, ob_ref, oc_ref, x_ref, go_ref, wo_ref, g2_ref, wr_ref, br_ref,
                     xn_ref, h2_ref, idx_ref, gate_ref):
    oa = _rms(oa_ref[...], go_ref[:, 0:D_A]).astype(BF16)
    ob = _rms(ob_ref[...], go_ref[:, D_A:D_A + D_B]).astype(BF16)
    oc = _rms(oc_ref[...], go_ref[:, D_A + D_B:]).astype(BF16)
    y = (jnp.dot(oa, wo_ref[0:D_A, :], preferred_element_type=F32)
         + jnp.dot(ob, wo_ref[D_A:D_A + D_B, :], preferred_element_type=F32)
         + jnp.dot(oc, wo_ref[D_A + D_B:, :], preferred_element_type=F32))
    xn = x_ref[...] + y
    xn_ref[...] = xn
    h2 = _rms(xn, g2_ref[...])
    h2_ref[...] = h2
    logits = jnp.dot(h2.astype(BF16), wr_ref[...], preferred_element_type=F32) + br_ref[...]
    lane = lax.broadcasted_iota(jnp.int32, logits.shape, 1).astype(F32)
    cur = logits
    tops, idxs = [], []
    for _ in range(TOP_K):
        m = jnp.max(cur, axis=-1, keepdims=True)
        ix = jnp.min(jnp.where(cur == m, lane, float(ROUTER_PAD)), axis=-1, keepdims=True)
        tops.append(m)
        idxs.append(ix)
        cur = jnp.where(lane == ix, -jnp.inf, cur)
    es = [jnp.exp(t - tops[0]) for t in tops]
    den = sum(es)
    idx_out = jnp.zeros(logits.shape, F32)
    gate_out = jnp.zeros(logits.shape, F32)
    for kk in range(TOP_K):
        idx_out = jnp.where(lane == kk, idxs[kk], idx_out)
        gate_out = jnp.where(lane == kk, es[kk] / den, gate_out)
    idx_ref[...] = idx_out.astype(jnp.int32)
    gate_ref[...] = gate_out


def _out_proj(oa, ob, oc, x, lw):
    n = x.shape[0]
    tm = TOKEN_TILE
    row = lambda w: pl.BlockSpec((tm, w), lambda i: (i, 0))
    return pl.pallas_call(
        _out_proj_kernel,
        grid=(n // tm,),
        in_specs=[row(D_A), row(D_B), row(D_C), row(D_MODEL), _full((1, D_MODEL)),
                  _full((D_MODEL, D_MODEL)), _full((1, D_MODEL)), _full((D_MODEL, ROUTER_PAD)),
                  _full((1, ROUTER_PAD))],
        out_specs=[row(D_MODEL), row(D_MODEL), row(ROUTER_PAD), row(ROUTER_PAD)],
        out_shape=[jax.ShapeDtypeStruct((n, D_MODEL), F32), jax.ShapeDtypeStruct((n, D_MODEL), F32),
                   jax.ShapeDtypeStruct((n, ROUTER_PAD), jnp.int32),
                   jax.ShapeDtypeStruct((n, ROUTER_PAD), F32)],
        compiler_params=_cparams(("parallel",)),
        name="out_proj",
    )(oa, ob, oc, x, lw["g_out"], lw["w_out"], lw["ln2_g"], lw["w_router"], lw["b_router"])


def _moe_kernel(te_ref, nu_ref, x_ref, wg_ref, wl_ref, bg_ref, bl_ref, wd_ref, bd_ref, y_ref):
    i = pl.program_id(0)

    @pl.when(i < nu_ref[0])
    def _():
        x = x_ref[...].astype(BF16)
        zg = jnp.dot(x, wg_ref[...], preferred_element_type=F32) + bg_ref[...]
        zl = jnp.dot(x, wl_ref[...], preferred_element_type=F32) + bl_ref[...]
        glu = jnp.minimum(zg, SWIGLU_LIMIT)
        lin = jnp.clip(zl, -SWIGLU_LIMIT, SWIGLU_LIMIT)
        act = glu * jax.nn.sigmoid(SWIGLU_ALPHA * glu) * (lin + 1.0)
        y_ref[...] = jnp.dot(act.astype(BF16), wd_ref[...], preferred_element_type=F32) + bd_ref[...]

    @pl.when(i >= nu_ref[0])
    def _():
        y_ref[...] = jnp.zeros(y_ref.shape, F32)


def _moe_ffn(xs, tile_expert, n_used, lw):
    n_slots = xs.shape[0]
    tm = MOE_TILE
    wspec = lambda r, c: pl.BlockSpec((None, r, c), lambda i, te, nu: (te[i], 0, 0))
    return pl.pallas_call(
        _moe_kernel,
        grid_spec=pltpu.PrefetchScalarGridSpec(
            num_scalar_prefetch=2,
            grid=(n_slots // tm,),
            in_specs=[pl.BlockSpec((tm, D_MODEL), lambda i, te, nu: (i, 0)),
                      wspec(D_MODEL, D_FF), wspec(D_MODEL, D_FF), wspec(1, D_FF), wspec(1, D_FF),
                      wspec(D_FF, D_MODEL), wspec(1, D_MODEL)],
            out_specs=pl.BlockSpec((tm, D_MODEL), lambda i, te, nu: (i, 0)),
        ),
        out_shape=jax.ShapeDtypeStruct((n_slots, D_MODEL), F32),
        compiler_params=_cparams(("arbitrary",)),
        name="moe_ffn",
    )(tile_expert, n_used, xs, lw["w_glu"], lw["w_lin"], lw["b_glu"], lw["b_lin"], lw["w_down"],
      lw["b_down"])


def _route(idx, n_tok):
    flat_e = idx.reshape(-1)
    n_assign = flat_e.shape[0]
    onehot = (flat_e[:, None] == jnp.arange(N_EXPERTS, dtype=jnp.int32)[None, :]).astype(jnp.int32)
    csum = jnp.cumsum(onehot, axis=0)
    rank = jnp.take_along_axis(csum, flat_e[:, None], axis=1)[:, 0] - 1
    counts = csum[-1]
    tiles = (counts + MOE_TILE - 1) // MOE_TILE
    tile_end = jnp.cumsum(tiles)
    pad_start = (tile_end - tiles) * MOE_TILE
    dest = pad_start[flat_e] + rank
    n_tiles = -(-n_assign // MOE_TILE) + N_EXPERTS
    tile_expert = jnp.minimum(
        jnp.searchsorted(tile_end, jnp.arange(n_tiles, dtype=jnp.int32), side="right"),
        N_EXPERTS - 1).astype(jnp.int32)
    n_used = tile_end[-1:].astype(jnp.int32)
    return dest.astype(jnp.int32), tile_expert, n_used, n_tiles


def _moe(xn, h2, idx, gates, lw):
    n_tok = xn.shape[0]
    dest, tile_expert, n_used, n_tiles = _route(idx, n_tok)
    n_slots = n_tiles * MOE_TILE
    flat_tok = jnp.arange(n_tok * TOP_K, dtype=jnp.int32) // TOP_K
    slot_tok = jnp.zeros((n_slots,), jnp.int32).at[dest].set(flat_tok)
    xs = jnp.take(h2, slot_tok, axis=0)
    y = _moe_ffn(xs, tile_expert, n_used, lw)
    yk = jnp.take(y, dest, axis=0).reshape(n_tok, TOP_K, D_MODEL)
    return xn + jnp.sum(yk * gates[:, :, None], axis=1)


def _block_diag(w):
    g, a, b = w.shape
    out = jnp.zeros((g * a, g * b), w.dtype)
    for i in range(g):
        out = out.at[i * a:(i + 1) * a, i * b:(i + 1) * b].set(w[i])
    return out


def _layer_weights(l, p):
    wr = jnp.pad(p["w_router"][l], ((0, 0), (0, ROUTER_PAD - N_EXPERTS)))
    hd = jnp.arange(D_A) // HEAD_DIM
    return {
        "ln1_g": p["ln1_g"][l][None], "w_in": p["w_in"][l].astype(BF16),
        "g_q": jnp.tile(p["g_q"][l], N_HEADS_A)[None], "g_k": jnp.tile(p["g_k"][l], N_HEADS_A)[None],
        "g_vb": p["g_vb"][l][None],
        "ones_bd": (hd[:, None] == hd[None, :]).astype(BF16),
        "w_s": p["w_s"][l],
        "bias_s": jnp.repeat(p["b_s"][l].T, HEAD_DIM, axis=1),
        "conv_w": p["conv_w"][l], "conv_b": p["conv_b"][l][None],
        "w_a_bd": _block_diag(p["w_a"][l]).astype(BF16), "b_a": p["b_a"][l][None],
        "w_x_bd": _block_diag(p["w_x"][l]).astype(BF16), "b_x": p["b_x"][l][None],
        "lru_lambda": p["lru_lambda"][l][None],
        "g_out": p["g_out"][l][None], "w_out": p["w_out"][l].astype(BF16),
        "ln2_g": p["ln2_g"][l][None],
        "w_router": wr.astype(BF16),
        "b_router": jnp.pad(p["b_router"][l], (0, ROUTER_PAD - N_EXPERTS),
                            constant_values=NEG_BIG)[None],
        "w_glu": p["w_gu"][l][:, :, 0::2].astype(BF16), "w_lin": p["w_gu"][l][:, :, 1::2].astype(BF16),
        "b_glu": p["b_gu"][l][:, None, 0::2], "b_lin": p["b_gu"][l][:, None, 1::2],
        "w_down": p["w_down"][l].astype(BF16), "b_down": p["b_down"][l][:, None, :],
    }


def kernel(x_prompt, x_sample, cache_win_k, cache_win_v, state_conv, state_lru, ln1_g, w_in, g_q, g_k,
           g_vb, w_s, b_s, conv_w, conv_b, w_a, b_a, w_x, b_x, lru_lambda, g_out, w_out, ln2_g, w_router,
           b_router, w_gu, b_gu, w_down, b_down):
    params = dict(ln1_g=ln1_g, w_in=w_in, g_q=g_q, g_k=g_k, g_vb=g_vb, w_s=w_s, b_s=b_s, conv_w=conv_w,
                  conv_b=conv_b, w_a=w_a, b_a=b_a, w_x=w_x, b_x=b_x, lru_lambda=lru_lambda, g_out=g_out,
                  w_out=w_out, ln2_g=ln2_g, w_router=w_router, b_router=b_router, w_gu=w_gu, b_gu=b_gu,
                  w_down=w_down, b_down=b_down)
    bp, sp, _ = x_prompt.shape
    bs, ss, _ = x_sample.shape
    depth = w_in.shape[0]
    n_p, n_s = bp * sp, bs * ss
    w_buf = cache_win_k.shape[2]
    keep = min(DILATED_PATTERNS[-1][0], sp)
    ck = cache_win_k.reshape(depth, bs, w_buf, D_A)
    cv = cache_win_v.reshape(depth, bs, w_buf, D_A)
    x = jnp.concatenate([x_prompt.reshape(n_p, D_MODEL), x_sample.reshape(n_s, D_MODEL)], axis=0)
    zero_conv = jnp.zeros((bp, SUBLANES, D_C), F32)
    zero_h = jnp.zeros((bp, 1, D_C), F32)
    outs = {name: [] for name in ("pk", "pv", "pconv", "plru", "sk", "sv", "sconv", "slru", "svb")}
    for l in range(depth):
        lw = _layer_weights(l, params)
        q, k, v, bc = _in_proj(x, lw["ln1_g"], lw["w_in"], lw["g_q"], lw["g_k"], lw["g_vb"], lw["ones_bd"])
        oa_p = _attn_prompt(q, k, v, bp, sp)
        ob_p, oc_p, h_p = _mixer_bc(bc, zero_conv, zero_h, lw, bp, sp, MIX_TILE, MIX_TILE - 1)
        oa_s = _attn_sample(q, k, v, ck, cv, l, n_p, bs, ss)
        bc_s = bc[n_p:].reshape(bs, ss, -1)
        bc_s_pad = jnp.pad(bc_s, ((0, 0), (0, CHUNK - ss), (0, 0))).reshape(bs * CHUNK, -1)
        conv8 = jnp.pad(state_conv[l], ((0, 0), (SUBLANES - (CONV_W - 1), 0), (0, 0)))
        ob_s, oc_s, h_s = _mixer_bc(bc_s_pad, conv8, state_lru[l][:, None, :], lw, bs, CHUNK, CHUNK, ss - 1)
        ob_s = ob_s.reshape(bs, CHUNK, D_B)[:, :ss].reshape(n_s, D_B)
        oc_s = oc_s.reshape(bs, CHUNK, D_C)[:, :ss].reshape(n_s, D_C)
        oa = jnp.concatenate([oa_p, oa_s], axis=0)
        ob = jnp.concatenate([ob_p, ob_s], axis=0)
        oc = jnp.concatenate([oc_p, oc_s], axis=0)
        xn, h2, idx, gates = _out_proj(oa, ob, oc, x, lw)
        x = _moe(xn, h2, idx[:, :TOP_K], gates[:, :TOP_K], lw)

        k_p = k[:n_p].reshape(bp, sp, N_HEADS_A, HEAD_DIM)
        v_p = v[:n_p].reshape(bp, sp, N_HEADS_A, HEAD_DIM)
        outs["pk"].append(k_p[:, sp - keep:])
        outs["pv"].append(v_p[:, sp - keep:])
        xc_p = bc[:n_p, 2 * D_B:2 * D_B + D_C].reshape(bp, sp, D_C)
        outs["pconv"].append(xc_p[:, sp - (CONV_W - 1):])
        outs["plru"].append(h_p[:, 0])
        outs["sk"].append(k[n_p:].reshape(bs, ss, N_HEADS_A, HEAD_DIM))
        outs["sv"].append(v[n_p:].reshape(bs, ss, N_HEADS_A, HEAD_DIM))
        xpad_s = jnp.concatenate([state_conv[l], bc_s[:, :, 2 * D_B:2 * D_B + D_C]], axis=1)
        outs["sconv"].append(xpad_s[:, -(CONV_W - 1):])
        outs["slru"].append(h_s[:, 0])
        outs["svb"].append(bc_s[:, :, D_B:2 * D_B])
    y_p = x[:n_p].reshape(bp, sp, D_MODEL)
    y_s = x[n_p:].reshape(bs, ss, D_MODEL)
    st = lambda name: jnp.stack(outs[name])
    return (y_p, y_s, st("pk"), st("pv"), st("pconv"), st("plru"), st("sk"), st("sv"), st("sconv"),
            st("slru"), st("svb"))
```

```python
import functools

import jax
import jax.numpy as jnp
from jax import lax
from jax.experimental import pallas as pl
from jax.experimental.pallas import tpu as pltpu

F32 = jnp.float32
BF16 = jnp.bfloat16

D_MODEL = 1024
HEAD_DIM = 64
N_HEADS_A = 8
D_A = N_HEADS_A * HEAD_DIM
N_HEADS_B = 4
D_B = N_HEADS_B * HEAD_DIM
N_GROUPS_C = 4
D_C = N_GROUPS_C * HEAD_DIM
D_IN = 3 * D_A + 2 * D_B + 2 * D_C
DILATED_PATTERNS = ((128, 1), (512, 4), (2048, 16))
N_PATTERNS = len(DILATED_PATTERNS)
CHUNK = 128
CONV_W = 4
LRU_C = 8.0
N_EXPERTS = 32
TOP_K = 4
D_FF = 1024
SWIGLU_LIMIT = 7.0
SWIGLU_ALPHA = 1.702
EPS = 1e-6
ATTN_SCALE = HEAD_DIM ** -0.5
PAST_LEN = 16384

LANES = 128
SUBLANES = 8
VMEM_LIMIT_BYTES = 56 * 1024 * 1024

Q_BLOCK = 128
ATTN_SPAN = 2048
TOKEN_TILE = 256
MIX_TILE = 512
MOE_TILE = 256
ROUTER_PAD = LANES
NEG_BIG = -1e30


def _cparams(semantics):
    return pltpu.CompilerParams(dimension_semantics=semantics,
                                vmem_limit_bytes=VMEM_LIMIT_BYTES)


def _full(shape):
    return pl.BlockSpec(shape, lambda *_: (0,) * len(shape))


def _rms(t, g):
    ms = jnp.mean(t * t, axis=-1, keepdims=True)
    return t * lax.rsqrt(ms + EPS) * g


def _split_bf16(t):
    hi = t.astype(BF16)
    lo = (t - hi.astype(F32)).astype(BF16)
    return hi, lo


def _in_proj_kernel(x_ref, g1_ref, w_ref, gq_ref, gk_ref, gvb_ref, ones_ref,
                    q_ref, k_ref, v_ref, bc_ref):
    h = _rms(x_ref[...], g1_ref[...]).astype(BF16)
    z = jnp.dot(h, w_ref[...], preferred_element_type=F32)

    def head_norm(t, g):
        hi, lo = _split_bf16(t * t)
        ss = (jnp.dot(hi, ones_ref[...], preferred_element_type=F32)
              + jnp.dot(lo, ones_ref[...], preferred_element_type=F32))
        return t * lax.rsqrt(ss * (1.0 / HEAD_DIM) + EPS) * g

    q_ref[...] = head_norm(z[:, 0:D_A], gq_ref[...]) * ATTN_SCALE
    k_ref[...] = head_norm(z[:, D_A:2 * D_A], gk_ref[...])
    v_ref[...] = z[:, 2 * D_A:3 * D_A]
    o = 3 * D_A
    bc_ref[...] = z[:, o:]
    bc_ref[:, D_B:2 * D_B] = _rms(z[:, o + D_B:o + 2 * D_B], gvb_ref[...])


def _in_proj(x, g1, w_bf16, gq, gk, gvb, ones_bd):
    n = x.shape[0]
    tm = TOKEN_TILE
    row = lambda w: pl.BlockSpec((tm, w), lambda i: (i, 0))
    return pl.pallas_call(
        _in_proj_kernel,
        grid=(n // tm,),
        in_specs=[row(D_MODEL), _full((1, D_MODEL)), _full((D_MODEL, D_IN)), _full((1, D_A)),
                  _full((1, D_A)), _full((1, D_B)), _full((D_A, D_A))],
        out_specs=[row(D_A), row(D_A), row(D_A), row(2 * D_B + 2 * D_C)],
        out_shape=[jax.ShapeDtypeStruct((n, D_A), F32)] * 3
        + [jax.ShapeDtypeStruct((n, 2 * D_B + 2 * D_C), F32)],
        compiler_params=_cparams(("parallel",)),
        name="in_proj",
    )(x, g1, w_bf16, gq, gk, gvb, ones_bd)


def _attn_prompt_kernel(q_ref, kp_ref, kc_ref, vp_ref, vc_ref, o_ref, kk, vv, m_s, l_s, a_s):
    span = pl.program_id(2)
    kk[0:ATTN_SPAN, :] = kp_ref[...]
    kk[ATTN_SPAN:, :] = kc_ref[...]
    vv[0:ATTN_SPAN, :] = vp_ref[...]
    vv[ATTN_SPAN:, :] = vc_ref[...]

    qb2 = 2 * Q_BLOCK
    lane = lax.broadcasted_iota(jnp.int32, (Q_BLOCK, LANES), 1)
    head0 = lane < HEAD_DIM
    row = lax.broadcasted_iota(jnp.int32, (qb2, qb2), 0) & (Q_BLOCK - 1)
    col = lax.broadcasted_iota(jnp.int32, (qb2, qb2), 1)
    band = (col >= row) & (col <= row + Q_BLOCK)
    cur = col >= Q_BLOCK

    def ds(start, size, d):
        return pl.ds(start, size) if d == 1 else pl.ds(start, size, stride=d)

    for p, (_, d) in enumerate(DILATED_PATTERNS):
        nblk = ATTN_SPAN // (Q_BLOCK * d)

        def body(it, carry, p=p, d=d, nblk=nblk):
            r = it // nblk
            ib = it % nblk
            qstart = r + d * Q_BLOCK * ib
            kstart = ATTN_SPAN + qstart - d * Q_BLOCK
            qb = q_ref[ds(qstart, Q_BLOCK, d), :]
            q2 = jnp.concatenate([jnp.where(head0, qb, 0.0), jnp.where(head0, 0.0, qb)], axis=0)
            qh, ql = _split_bf16(q2)
            kh, kl = _split_bf16(kk[ds(kstart, qb2, d), :])
            s = lax.dot_general(jnp.concatenate([qh, ql, qh], axis=1),
                                jnp.concatenate([kh, kh, kl], axis=1),
                                (((1,), (1,)), ((), ())), preferred_element_type=F32)
            prev_ok = jnp.logical_or(ib > 0, span > 0)
            s = jnp.where(band & (cur | prev_ok), s, -jnp.inf)
            m = jnp.max(s, axis=-1, keepdims=True)
            e = jnp.exp(s - m)
            l = jnp.sum(e, axis=-1, keepdims=True)
            eh, el = _split_bf16(e)
            vh, vl = _split_bf16(vv[ds(kstart, qb2, d), :])
            acc = jnp.dot(jnp.concatenate([eh, el, eh], axis=1),
                          jnp.concatenate([vh, vh, vl], axis=0),
                          preferred_element_type=F32)
            dst = ds(qstart, Q_BLOCK, d)
            shape = (Q_BLOCK, LANES)
            m_s[p, dst, :] = jnp.where(head0, jnp.broadcast_to(m[:Q_BLOCK], shape),
                                       jnp.broadcast_to(m[Q_BLOCK:], shape))
            l_s[p, dst, :] = jnp.where(head0, jnp.broadcast_to(l[:Q_BLOCK], shape),
                                       jnp.broadcast_to(l[Q_BLOCK:], shape))
            a_s[p, dst, :] = jnp.where(head0, acc[:Q_BLOCK], acc[Q_BLOCK:])
            return carry

        lax.fori_loop(0, ATTN_SPAN // Q_BLOCK, body, 0)

    def merge(c, carry):
        rows = pl.ds(pl.multiple_of(c * Q_BLOCK, Q_BLOCK), Q_BLOCK)
        ms = [m_s[p, rows, :] for p in range(N_PATTERNS)]
        m_all = functools.reduce(jnp.maximum, ms)
        ws = [jnp.exp(m - m_all) for m in ms]
        num = sum(w * a_s[p, rows, :] for p, w in enumerate(ws))
        den = sum(w * l_s[p, rows, :] for p, w in enumerate(ws))
        o_ref[rows, :] = num / den
        return carry

    lax.fori_loop(0, ATTN_SPAN // Q_BLOCK, merge, 0)


def _attn_prompt(q, k, v, batch, seq):
    nspan = seq // ATTN_SPAN
    blk = (ATTN_SPAN, LANES)
    cur = pl.BlockSpec(blk, lambda b, hp, s: (b * nspan + s, hp))
    prev = pl.BlockSpec(blk, lambda b, hp, s: (b * nspan + jnp.maximum(s - 1, 0), hp))
    acc = pltpu.VMEM((N_PATTERNS, ATTN_SPAN, LANES), F32)
    return pl.pallas_call(
        _attn_prompt_kernel,
        grid=(batch, D_A // LANES, nspan),
        in_specs=[cur, prev, cur, prev, cur],
        out_specs=cur,
        out_shape=jax.ShapeDtypeStruct((batch * seq, D_A), F32),
        scratch_shapes=[pltpu.VMEM((2 * ATTN_SPAN, LANES), F32), pltpu.VMEM((2 * ATTN_SPAN, LANES), F32),
                        acc, acc, acc],
        compiler_params=_cparams(("parallel", "parallel", "arbitrary")),
        name="attn_prompt",
    )(q, k, k, v, v)


def _attn_sample_kernel(q_ref, kn_ref, vn_ref, ck_ref, cv_ref, o_ref, *, w_buf, t_new):
    pad = LANES - t_new
    zeros = jnp.zeros((pad, D_A), F32)
    kall = jnp.concatenate([ck_ref[...], kn_ref[...], zeros], axis=0).astype(BF16)
    vall = jnp.concatenate([cv_ref[...], vn_ref[...], zeros], axis=0).astype(BF16)
    n_keys = w_buf + LANES
    n_rows = N_HEADS_A * t_new
    hrow = lax.broadcasted_iota(jnp.int32, (n_rows, D_A), 0) // t_new
    hlane = lax.broadcasted_iota(jnp.int32, (n_rows, D_A), 1) // HEAD_DIM
    own = hrow == hlane
    q_rep = jnp.concatenate([q_ref[...]] * N_HEADS_A, axis=0)
    q64 = jnp.where(own, q_rep, 0.0).astype(BF16)
    s = lax.dot_general(q64, kall, (((1,), (1,)), ((), ())), preferred_element_type=F32)
    t = lax.broadcasted_iota(jnp.int32, (n_rows, n_keys), 0) % t_new
    c = lax.broadcasted_iota(jnp.int32, (n_rows, n_keys), 1)
    dist = w_buf + t - c
    ms, ls, accs = [], [], []
    for w, d in DILATED_PATTERNS:
        ok = (dist >= 0) & (dist <= w) & ((dist & (d - 1)) == 0)
        sp = jnp.where(ok, s, -jnp.inf)
        m = jnp.max(sp, axis=-1, keepdims=True)
        e = jnp.exp(sp - m)
        ms.append(m)
        ls.append(jnp.sum(e, axis=-1, keepdims=True))
        accs.append(jnp.dot(e.astype(BF16), vall, preferred_element_type=F32))
    m_all = functools.reduce(jnp.maximum, ms)
    ws = [jnp.exp(m - m_all) for m in ms]
    num = sum(w * a for w, a in zip(ws, accs))
    den = sum(w * l for w, l in zip(ws, ls))
    o = jnp.where(own, num / den, 0.0)
    out = o[0:t_new]
    for h in range(1, N_HEADS_A):
        out = out + o[h * t_new:(h + 1) * t_new]
    o_ref[...] = out


def _attn_sample(q, k, v, cache_k, cache_v, layer, row0, batch, t_new):
    w_buf = cache_k.shape[2]
    assert row0 % t_new == 0 and t_new == SUBLANES
    new = pl.BlockSpec((t_new, D_A), lambda b: (row0 // t_new + b, 0))
    cache = pl.BlockSpec((None, None, w_buf, D_A), lambda b: (layer, b, 0, 0))
    return pl.pallas_call(
        functools.partial(_attn_sample_kernel, w_buf=w_buf, t_new=t_new),
        grid=(batch,),
        in_specs=[new, new, new, cache, cache],
        out_specs=pl.BlockSpec((t_new, D_A), lambda b: (b, 0)),
        out_shape=jax.ShapeDtypeStruct((batch * t_new, D_A), F32),
        compiler_params=_cparams(("parallel",)),
        name="attn_sample",
    )(q, k, v, cache_k, cache_v)


def _gelu_tanh(x):
    return 0.5 * x * (1.0 + jnp.tanh(0.7978845608028654 * (x + 0.044715 * x * x * x)))


def _mixer_bc_kernel(bc_ref, cb_ref, h0_ref, ws_ref, bs_ref, cw_ref, cbias_ref, wa_ref, ba_ref,
                     wx_ref, bx_ref, lam_ref, ob_ref, oc_ref, hl_ref, xp_s, h_s, *, tt, last_row):
    j = pl.program_id(1)

    @pl.when(j == 0)
    def _():
        xp_s[0:SUBLANES, :] = cb_ref[...]
        h_s[...] = h0_ref[...]

    nch = tt // CHUNK
    vcat = jnp.concatenate([bc_ref[c * CHUNK:(c + 1) * CHUNK, D_B:2 * D_B] for c in range(nch)],
                           axis=1).astype(BF16)
    ri = lax.broadcasted_iota(jnp.int32, (CHUNK, CHUNK), 0)
    ci = lax.broadcasted_iota(jnp.int32, (CHUNK, CHUNK), 1)
    hl = (lax.broadcasted_iota(jnp.int32, (CHUNK, nch * D_B), 1) % D_B) // HEAD_DIM
    mixed = jnp.zeros((CHUNK, nch * D_B), F32)
    for h in range(N_HEADS_B):
        wh = jnp.where(ri >= ci, ws_ref[h], 0.0).astype(BF16)
        mh = jnp.dot(wh, vcat, preferred_element_type=F32)
        mixed = mixed + jnp.where(hl == h, mh, 0.0)
    for c in range(nch):
        rows = slice(c * CHUNK, (c + 1) * CHUNK)
        ob_ref[rows, :] = bc_ref[rows, 0:D_B] * (mixed[:, c * D_B:(c + 1) * D_B] + bs_ref[...])

    xc = bc_ref[:, 2 * D_B:2 * D_B + D_C]
    xp_s[SUBLANES:SUBLANES + tt, :] = xc
    xconv = cbias_ref[...] + cw_ref[CONV_W - 1:CONV_W, :] * xc
    for kk in range(CONV_W - 1):
        off = SUBLANES - (CONV_W - 1) + kk
        xconv = xconv + cw_ref[kk:kk + 1, :] * xp_s[off:off + tt, :]
    xp_s[0:SUBLANES, :] = xp_s[tt:tt + SUBLANES, :]
    xb = xconv.astype(BF16)
    r = jax.nn.sigmoid(jnp.dot(xb, wa_ref[...], preferred_element_type=F32) + ba_ref[...])
    i = jax.nn.sigmoid(jnp.dot(xb, wx_ref[...], preferred_element_type=F32) + bx_ref[...])
    nl = -lam_ref[...]
    softplus = jnp.maximum(nl, 0.0) + jnp.log1p(jnp.exp(-jnp.abs(nl)))
    a = jnp.exp(-LRU_C * r * softplus)
    b = jnp.sqrt(1.0 - a * a) * (i * xconv)
    rowi = lax.broadcasted_iota(jnp.int32, (tt, D_C), 0)
    step = 1
    while step < tt:
        a_sh = pltpu.roll(a, step, axis=0)
        b_sh = pltpu.roll(b, step, axis=0)
        live = rowi >= step
        b = jnp.where(live, a * b_sh + b, b)
        a = jnp.where(live, a * a_sh, a)
        step *= 2
    h = a * h_s[...] + b
    h_s[...] = h[tt - 1:tt, :]
    oc_ref[...] = h * _gelu_tanh(bc_ref[:, 2 * D_B + D_C:])

    @pl.when(j == pl.num_programs(1) - 1)
    def _():
        hl_ref[...] = h[last_row:last_row + 1, :]


def _mixer_bc(bc, conv_buf8, h0, lw, batch, t_len, tt, last_row):
    nt = t_len // tt
    rows = lambda w: pl.BlockSpec((tt, w), lambda b, j: (b * nt + j, 0))
    per_b = lambda s: pl.BlockSpec((None,) + s, lambda b, j: (b,) + (0,) * len(s))
    return pl.pallas_call(
        functools.partial(_mixer_bc_kernel, tt=tt, last_row=last_row),
        grid=(batch, nt),
        in_specs=[rows(2 * D_B + 2 * D_C), per_b((SUBLANES, D_C)), per_b((1, D_C)),
                  _full((N_HEADS_B, CHUNK, CHUNK)), _full((CHUNK, D_B)), _full((CONV_W, D_C)),
                  _full((1, D_C)), _full((D_C, D_C)), _full((1, D_C)), _full((D_C, D_C)),
                  _full((1, D_C)), _full((1, D_C))],
        out_specs=[rows(D_B), rows(D_C), per_b((1, D_C))],
        out_shape=[jax.ShapeDtypeStruct((batch * t_len, D_B), F32),
                   jax.ShapeDtypeStruct((batch * t_len, D_C), F32),
                   jax.ShapeDtypeStruct((batch, 1, D_C), F32)],
        scratch_shapes=[pltpu.VMEM((tt + SUBLANES, D_C), F32), pltpu.VMEM((1, D_C), F32)],
        compiler_params=_cparams(("parallel", "arbitrary")),
        name="mixer_bc",
    )(bc, conv_buf8, h0, lw["w_s"], lw["bias_s"], lw["conv_w"], lw["conv_b"], lw["w_a_bd"],
      lw["b_a"], lw["w_x_bd"], lw["b_x"], lw["lru_lambda"])


def _out_proj_kernel(oa_ref, ob_ref, oc_ref, x_ref, go_ref, wo_ref, g2_ref, wr_ref, br_ref,
                     xn_ref, h2_ref, idx_ref, gate_ref):
    oa = _rms(oa_ref[...], go_ref[:, 0:D_A]).astype(BF16)
    ob = _rms(ob_ref[...], go_ref[:, D_A:D_A + D_B]).astype(BF16)
    oc = _rms(oc_ref[...], go_ref[:, D_A + D_B:]).astype(BF16)
    y = (jnp.dot(oa, wo_ref[0:D_A, :], preferred_element_type=F32)
         + jnp.dot(ob, wo_ref[D_A:D_A + D_B, :], preferred_element_type=F32)
         + jnp.dot(oc, wo_ref[D_A + D_B:, :], preferred_element_type=F32))
    xn = x_ref[...] + y
    xn_ref[...] = xn
    h2 = _rms(xn, g2_ref[...])
    h2_ref[...] = h2
    logits = jnp.dot(h2.astype(BF16), wr_ref[...], preferred_element_type=F32) + br_ref[...]
    lane = lax.broadcasted_iota(jnp.int32, logits.shape, 1).astype(F32)
    cur = logits
    tops, idxs = [], []
    for _ in range(TOP_K):
        m = jnp.max(cur, axis=-1, keepdims=True)
        ix = jnp.min(jnp.where(cur == m, lane, float(ROUTER_PAD)), axis=-1, keepdims=True)
        tops.append(m)
        idxs.append(ix)
        cur = jnp.where(lane == ix, -jnp.inf, cur)
    es = [jnp.exp(t - tops[0]) for t in tops]
    den = sum(es)
    idx_out = jnp.zeros(logits.shape, F32)
    gate_out = jnp.zeros(logits.shape, F32)
    for kk in range(TOP_K):
        idx_out = jnp.where(lane == kk, idxs[kk], idx_out)
        gate_out = jnp.where(lane == kk, es[kk] / den, gate_out)
    idx_ref[...] = idx_out.astype(jnp.int32)
    gate_ref[...] = gate_out


def _out_proj(oa, ob, oc, x, lw):
    n = x.shape[0]
    tm = TOKEN_TILE
    row = lambda w: pl.BlockSpec((tm, w), lambda i: (i, 0))
    return pl.pallas_call(
        _out_proj_kernel,
        grid=(n // tm,),
        in_specs=[row(D_A), row(D_B), row(D_C), row(D_MODEL), _full((1, D_MODEL)),
                  _full((D_MODEL, D_MODEL)), _full((1, D_MODEL)), _full((D_MODEL, ROUTER_PAD)),
                  _full((1, ROUTER_PAD))],
        out_specs=[row(D_MODEL), row(D_MODEL), row(ROUTER_PAD), row(ROUTER_PAD)],
        out_shape=[jax.ShapeDtypeStruct((n, D_MODEL), F32), jax.ShapeDtypeStruct((n, D_MODEL), F32),
                   jax.ShapeDtypeStruct((n, ROUTER_PAD), jnp.int32),
                   jax.ShapeDtypeStruct((n, ROUTER_PAD), F32)],
        compiler_params=_cparams(("parallel",)),
        name="out_proj",
    )(oa, ob, oc, x, lw["g_out"], lw["w_out"], lw["ln2_g"], lw["w_router"], lw["b_router"])


def _moe_kernel(te_ref, nu_ref, x_ref, wgu_ref, bg_ref, bl_ref, wd_ref, bd_ref, y_ref, wg_s, wl_s, wd_s):
    i = pl.program_id(0)
    used = i < nu_ref[0]
    new_expert = jnp.logical_or(i == 0, te_ref[i] != te_ref[jnp.maximum(i - 1, 0)])

    @pl.when(jnp.logical_and(used, new_expert))
    def _():
        rc = CHUNK
        lane = lax.broadcasted_iota(jnp.int32, (rc, LANES), 1)
        low = lane < LANES // 2
        even_first = jnp.where(low, 2 * lane, 2 * lane - LANES + 1)
        odd_first = jnp.where(low, 2 * lane + 1, 2 * lane - LANES)

        def body(r, carry):
            rows = pl.ds(pl.multiple_of(r * rc, rc), rc)
            for c in range(D_FF // LANES):
                a = wgu_ref[rows, 2 * c * LANES:(2 * c + 1) * LANES]
                b = wgu_ref[rows, (2 * c + 1) * LANES:(2 * c + 2) * LANES]
                take = lambda t, ix: jnp.take_along_axis(t, ix, axis=1)
                wg_s[rows, c * LANES:(c + 1) * LANES] = jnp.where(
                    low, take(a, even_first), take(b, odd_first)).astype(BF16)
                wl_s[rows, c * LANES:(c + 1) * LANES] = jnp.where(
                    low, take(a, odd_first), take(b, even_first)).astype(BF16)
            wd_s[rows, :] = wd_ref[rows, :].astype(BF16)
            return carry

        lax.fori_loop(0, D_MODEL // rc, body, 0)

    @pl.when(used)
    def _():
        x = x_ref[...].astype(BF16)
        zg = jnp.dot(x, wg_s[...], preferred_element_type=F32) + bg_ref[...]
        zl = jnp.dot(x, wl_s[...], preferred_element_type=F32) + bl_ref[...]
        glu = jnp.minimum(zg, SWIGLU_LIMIT)
        lin = jnp.clip(zl, -SWIGLU_LIMIT, SWIGLU_LIMIT)
        act = glu * jax.nn.sigmoid(SWIGLU_ALPHA * glu) * (lin + 1.0)
        y_ref[...] = jnp.dot(act.astype(BF16), wd_s[...], preferred_element_type=F32) + bd_ref[...]

    @pl.when(i >= nu_ref[0])
    def _():
        y_ref[...] = jnp.zeros(y_ref.shape, F32)


def _moe_ffn(xs, tile_expert, n_used, lw):
    n_slots = xs.shape[0]
    tm = MOE_TILE
    layer = lw["layer"]
    wspec = lambda r, c: pl.BlockSpec((None, r, c), lambda i, te, nu: (te[i], 0, 0))
    wfull = lambda r, c: pl.BlockSpec((None, None, r, c), lambda i, te, nu: (layer, te[i], 0, 0))
    return pl.pallas_call(
        _moe_kernel,
        grid_spec=pltpu.PrefetchScalarGridSpec(
            num_scalar_prefetch=2,
            grid=(n_slots // tm,),
            in_specs=[pl.BlockSpec((tm, D_MODEL), lambda i, te, nu: (i, 0)),
                      wfull(D_MODEL, 2 * D_FF), wspec(1, D_FF), wspec(1, D_FF),
                      wfull(D_FF, D_MODEL), wspec(1, D_MODEL)],
            out_specs=pl.BlockSpec((tm, D_MODEL), lambda i, te, nu: (i, 0)),
            scratch_shapes=[pltpu.VMEM((D_MODEL, D_FF), BF16), pltpu.VMEM((D_MODEL, D_FF), BF16),
                            pltpu.VMEM((D_FF, D_MODEL), BF16)],
        ),
        out_shape=jax.ShapeDtypeStruct((n_slots, D_MODEL), F32),
        compiler_params=_cparams(("arbitrary",)),
        name="moe_ffn",
    )(tile_expert, n_used, xs, lw["w_gu"], lw["b_glu"], lw["b_lin"], lw["w_down"], lw["b_down"])


def _route(idx, n_tok):
    flat_e = idx.reshape(-1)
    n_assign = flat_e.shape[0]
    onehot = (flat_e[:, None] == jnp.arange(N_EXPERTS, dtype=jnp.int32)[None, :]).astype(jnp.int32)
    csum = jnp.cumsum(onehot, axis=0)
    rank = jnp.take_along_axis(csum, flat_e[:, None], axis=1)[:, 0] - 1
    counts = csum[-1]
    tiles = (counts + MOE_TILE - 1) // MOE_TILE
    tile_end = jnp.cumsum(tiles)
    pad_start = (tile_end - tiles) * MOE_TILE
    dest = pad_start[flat_e] + rank
    n_tiles = -(-n_assign // MOE_TILE) + N_EXPERTS
    tile_expert = jnp.minimum(
        jnp.searchsorted(tile_end, jnp.arange(n_tiles, dtype=jnp.int32), side="right"),
        N_EXPERTS - 1).astype(jnp.int32)
    n_used = tile_end[-1:].astype(jnp.int32)
    return dest.astype(jnp.int32), tile_expert, n_used, n_tiles


def _moe(xn, h2, idx, gates, lw):
    n_tok = xn.shape[0]
    dest, tile_expert, n_used, n_tiles = _route(idx, n_tok)
    n_slots = n_tiles * MOE_TILE
    flat_tok = jnp.arange(n_tok * TOP_K, dtype=jnp.int32) // TOP_K
    slot_tok = jnp.zeros((n_slots,), jnp.int32).at[dest].set(flat_tok)
    xs = jnp.take(h2, slot_tok, axis=0)
    y = _moe_ffn(xs, tile_expert, n_used, lw)
    yk = jnp.take(y, dest, axis=0).reshape(n_tok, TOP_K, D_MODEL)
    return xn + jnp.sum(yk * gates[:, :, None], axis=1)


def _block_diag(w):
    g, a, b = w.shape
    out = jnp.zeros((g * a, g * b), w.dtype)
    for i in range(g):
        out = out.at[i * a:(i + 1) * a, i * b:(i + 1) * b].set(w[i])
    return out


def _layer_weights(l, p):
    wr = jnp.pad(p["w_router"][l], ((0, 0), (0, ROUTER_PAD - N_EXPERTS)))
    hd = jnp.arange(D_A) // HEAD_DIM
    return {
        "ln1_g": p["ln1_g"][l][None], "w_in": p["w_in"][l].astype(BF16),
        "g_q": jnp.tile(p["g_q"][l], N_HEADS_A)[None], "g_k": jnp.tile(p["g_k"][l], N_HEADS_A)[None],
        "g_vb": p["g_vb"][l][None],
        "ones_bd": (hd[:, None] == hd[None, :]).astype(BF16),
        "w_s": p["w_s"][l],
        "bias_s": jnp.repeat(p["b_s"][l].T, HEAD_DIM, axis=1),
        "conv_w": p["conv_w"][l], "conv_b": p["conv_b"][l][None],
        "w_a_bd": _block_diag(p["w_a"][l]).astype(BF16), "b_a": p["b_a"][l][None],
        "w_x_bd": _block_diag(p["w_x"][l]).astype(BF16), "b_x": p["b_x"][l][None],
        "lru_lambda": p["lru_lambda"][l][None],
        "g_out": p["g_out"][l][None], "w_out": p["w_out"][l].astype(BF16),
        "ln2_g": p["ln2_g"][l][None],
        "w_router": wr.astype(BF16),
        "b_router": jnp.pad(p["b_router"][l], (0, ROUTER_PAD - N_EXPERTS),
                            constant_values=NEG_BIG)[None],
        "layer": l, "w_gu": p["w_gu"], "w_down": p["w_down"],
        "b_glu": p["b_gu"][l][:, None, 0::2], "b_lin": p["b_gu"][l][:, None, 1::2],
        "b_down": p["b_down"][l][:, None, :],
    }


def kernel(x_prompt, x_sample, cache_win_k, cache_win_v, state_conv, state_lru, ln1_g, w_in, g_q, g_k,
           g_vb, w_s, b_s, conv_w, conv_b, w_a, b_a, w_x, b_x, lru_lambda, g_out, w_out, ln2_g, w_router,
           b_router, w_gu, b_gu, w_down, b_down):
    params = dict(ln1_g=ln1_g, w_in=w_in, g_q=g_q, g_k=g_k, g_vb=g_vb, w_s=w_s, b_s=b_s, conv_w=conv_w,
                  conv_b=conv_b, w_a=w_a, b_a=b_a, w_x=w_x, b_x=b_x, lru_lambda=lru_lambda, g_out=g_out,
                  w_out=w_out, ln2_g=ln2_g, w_router=w_router, b_router=b_router, w_gu=w_gu, b_gu=b_gu,
                  w_down=w_down, b_down=b_down)
    bp, sp, _ = x_prompt.shape
    bs, ss, _ = x_sample.shape
    depth = w_in.shape[0]
    n_p, n_s = bp * sp, bs * ss
    w_buf = cache_win_k.shape[2]
    keep = min(DILATED_PATTERNS[-1][0], sp)
    ck = cache_win_k.reshape(depth, bs, w_buf, D_A)
    cv = cache_win_v.reshape(depth, bs, w_buf, D_A)
    x = jnp.concatenate([x_prompt.reshape(n_p, D_MODEL), x_sample.reshape(n_s, D_MODEL)], axis=0)
    zero_conv = jnp.zeros((bp, SUBLANES, D_C), F32)
    zero_h = jnp.zeros((bp, 1, D_C), F32)
    outs = {name: [] for name in ("pk", "pv", "pconv", "plru", "sk", "sv", "sconv", "slru", "svb")}
    for l in range(depth):
        lw = _layer_weights(l, params)
        q, k, v, bc = _in_proj(x, lw["ln1_g"], lw["w_in"], lw["g_q"], lw["g_k"], lw["g_vb"], lw["ones_bd"])
        oa_p = _attn_prompt(q, k, v, bp, sp)
        ob_p, oc_p, h_p = _mixer_bc(bc, zero_conv, zero_h, lw, bp, sp, MIX_TILE, MIX_TILE - 1)
        oa_s = _attn_sample(q, k, v, ck, cv, l, n_p, bs, ss)
        bc_s = bc[n_p:].reshape(bs, ss, -1)
        bc_s_pad = jnp.pad(bc_s, ((0, 0), (0, CHUNK - ss), (0, 0))).reshape(bs * CHUNK, -1)
        conv8 = jnp.pad(state_conv[l], ((0, 0), (SUBLANES - (CONV_W - 1), 0), (0, 0)))
        ob_s, oc_s, h_s = _mixer_bc(bc_s_pad, conv8, state_lru[l][:, None, :], lw, bs, CHUNK, CHUNK, ss - 1)
        ob_s = ob_s.reshape(bs, CHUNK, D_B)[:, :ss].reshape(n_s, D_B)
        oc_s = oc_s.reshape(bs, CHUNK, D_C)[:, :ss].reshape(n_s, D_C)
        oa = jnp.concatenate([oa_p, oa_s], axis=0)
        ob = jnp.concatenate([ob_p, ob_s], axis=0)
        oc = jnp.concatenate([oc_p, oc_s], axis=0)
        xn, h2, idx, gates = _out_proj(oa, ob, oc, x, lw)
        x = _moe(xn, h2, idx[:, :TOP_K], gates[:, :TOP_K], lw)

        k_p = k[:n_p].reshape(bp, sp, N_HEADS_A, HEAD_DIM)
        v_p = v[:n_p].reshape(bp, sp, N_HEADS_A, HEAD_DIM)
        outs["pk"].append(k_p[:, sp - keep:])
        outs["pv"].append(v_p[:, sp - keep:])
        xc_p = bc[:n_p, 2 * D_B:2 * D_B + D_C].reshape(bp, sp, D_C)
        outs["pconv"].append(xc_p[:, sp - (CONV_W - 1):])
        outs["plru"].append(h_p[:, 0])
        outs["sk"].append(k[n_p:].reshape(bs, ss, N_HEADS_A, HEAD_DIM))
        outs["sv"].append(v[n_p:].reshape(bs, ss, N_HEADS_A, HEAD_DIM))
        xpad_s = jnp.concatenate([state_conv[l], bc_s[:, :, 2 * D_B:2 * D_B + D_C]], axis=1)
        outs["sconv"].append(xpad_s[:, -(CONV_W - 1):])
        outs["slru"].append(h_s[:, 0])
        outs["svb"].append(bc_s[:, :, D_B:2 * D_B])
    y_p = x[:n_p].reshape(bp, sp, D_MODEL)
    y_s = x[n_p:].reshape(bs, ss, D_MODEL)
    st = lambda name: jnp.stack(outs[name])
    return (y_p, y_s, st("pk"), st("pv"), st("pconv"), st("plru"), st("sk"), st("sv"), st("sconv"),
            st("slru"), st("svb"))
```

```python
import functools

import jax
import jax.numpy as jnp
from jax import lax
from jax.experimental import pallas as pl
from jax.experimental.pallas import tpu as pltpu
from jax.experimental.pallas import tpu_sc as plsc

F32 = jnp.float32
BF16 = jnp.bfloat16

D_MODEL = 1024
HEAD_DIM = 64
N_HEADS_A = 8
D_A = N_HEADS_A * HEAD_DIM
N_HEADS_B = 4
D_B = N_HEADS_B * HEAD_DIM
N_GROUPS_C = 4
D_C = N_GROUPS_C * HEAD_DIM
D_IN = 3 * D_A + 2 * D_B + 2 * D_C
DILATED_PATTERNS = ((128, 1), (512, 4), (2048, 16))
N_PATTERNS = len(DILATED_PATTERNS)
CHUNK = 128
CONV_W = 4
LRU_C = 8.0
N_EXPERTS = 32
TOP_K = 4
D_FF = 1024
SWIGLU_LIMIT = 7.0
SWIGLU_ALPHA = 1.702
EPS = 1e-6
ATTN_SCALE = HEAD_DIM ** -0.5
PAST_LEN = 16384

LANES = 128
SUBLANES = 8
VMEM_LIMIT_BYTES = 56 * 1024 * 1024

Q_BLOCK = 128
ATTN_SPAN = 2048
TOKEN_TILE = 256
MIX_TILE = 512
MOE_TILE = 256
ROUTER_PAD = LANES
SC_SUBCORES = 16
SC_WORKERS = 2 * SC_SUBCORES
SC_GROUP = LANES // TOP_K
NEG_BIG = -1e30


def _cparams(semantics):
    return pltpu.CompilerParams(dimension_semantics=semantics,
                                vmem_limit_bytes=VMEM_LIMIT_BYTES)


def _full(shape):
    return pl.BlockSpec(shape, lambda *_: (0,) * len(shape))


def _rms(t, g):
    ms = jnp.mean(t * t, axis=-1, keepdims=True)
    return t * lax.rsqrt(ms + EPS) * g


def _split_bf16(t):
    hi = t.astype(BF16)
    lo = (t - hi.astype(F32)).astype(BF16)
    return hi, lo


def _in_proj_kernel(x_ref, g1_ref, w_ref, gq_ref, gk_ref, gvb_ref, ones_ref,
                    q_ref, k_ref, v_ref, bc_ref):
    h = _rms(x_ref[...], g1_ref[...]).astype(BF16)
    z = jnp.dot(h, w_ref[...], preferred_element_type=F32)

    def head_norm(t, g):
        hi, lo = _split_bf16(t * t)
        ss = (jnp.dot(hi, ones_ref[...], preferred_element_type=F32)
              + jnp.dot(lo, ones_ref[...], preferred_element_type=F32))
        return t * lax.rsqrt(ss * (1.0 / HEAD_DIM) + EPS) * g

    q_ref[...] = head_norm(z[:, 0:D_A], gq_ref[...]) * ATTN_SCALE
    k_ref[...] = head_norm(z[:, D_A:2 * D_A], gk_ref[...])
    v_ref[...] = z[:, 2 * D_A:3 * D_A]
    o = 3 * D_A
    bc_ref[...] = z[:, o:]
    bc_ref[:, D_B:2 * D_B] = _rms(z[:, o + D_B:o + 2 * D_B], gvb_ref[...])


def _in_proj(x, g1, w_bf16, gq, gk, gvb, ones_bd):
    n = x.shape[0]
    tm = TOKEN_TILE
    row = lambda w: pl.BlockSpec((tm, w), lambda i: (i, 0))
    return pl.pallas_call(
        _in_proj_kernel,
        grid=(n // tm,),
        in_specs=[row(D_MODEL), _full((1, D_MODEL)), _full((D_MODEL, D_IN)), _full((1, D_A)),
                  _full((1, D_A)), _full((1, D_B)), _full((D_A, D_A))],
        out_specs=[row(D_A), row(D_A), row(D_A), row(2 * D_B + 2 * D_C)],
        out_shape=[jax.ShapeDtypeStruct((n, D_A), F32)] * 3
        + [jax.ShapeDtypeStruct((n, 2 * D_B + 2 * D_C), F32)],
        compiler_params=_cparams(("parallel",)),
        name="in_proj",
    )(x, g1, w_bf16, gq, gk, gvb, ones_bd)


def _attn_prompt_kernel(q_ref, kp_ref, kc_ref, vp_ref, vc_ref, o_ref, kk, vv, m_s, l_s, a_s):
    span = pl.program_id(2)
    kk[0:ATTN_SPAN, :] = kp_ref[...]
    kk[ATTN_SPAN:, :] = kc_ref[...]
    vv[0:ATTN_SPAN, :] = vp_ref[...]
    vv[ATTN_SPAN:, :] = vc_ref[...]

    qb2 = 2 * Q_BLOCK
    lane = lax.broadcasted_iota(jnp.int32, (Q_BLOCK, LANES), 1)
    head0 = lane < HEAD_DIM
    row = lax.broadcasted_iota(jnp.int32, (qb2, qb2), 0) & (Q_BLOCK - 1)
    col = lax.broadcasted_iota(jnp.int32, (qb2, qb2), 1)
    band = (col >= row) & (col <= row + Q_BLOCK)
    cur = col >= Q_BLOCK

    def ds(start, size, d):
        return pl.ds(start, size) if d == 1 else pl.ds(start, size, stride=d)

    for p, (_, d) in enumerate(DILATED_PATTERNS):
        nblk = ATTN_SPAN // (Q_BLOCK * d)

        def body(it, carry, p=p, d=d, nblk=nblk):
            r = it // nblk
            ib = it % nblk
            qstart = r + d * Q_BLOCK * ib
            kstart = ATTN_SPAN + qstart - d * Q_BLOCK
            qb = q_ref[ds(qstart, Q_BLOCK, d), :]
            q2 = jnp.concatenate([jnp.where(head0, qb, 0.0), jnp.where(head0, 0.0, qb)], axis=0)
            qh, ql = _split_bf16(q2)
            kh, kl = _split_bf16(kk[ds(kstart, qb2, d), :])
            s = lax.dot_general(jnp.concatenate([qh, ql, qh], axis=1),
                                jnp.concatenate([kh, kh, kl], axis=1),
                                (((1,), (1,)), ((), ())), preferred_element_type=F32)
            prev_ok = jnp.logical_or(ib > 0, span > 0)
            s = jnp.where(band & (cur | prev_ok), s, -jnp.inf)
            m = jnp.max(s, axis=-1, keepdims=True)
            e = jnp.exp(s - m)
            l = jnp.sum(e, axis=-1, keepdims=True)
            eh, el = _split_bf16(e)
            vh, vl = _split_bf16(vv[ds(kstart, qb2, d), :])
            acc = jnp.dot(jnp.concatenate([eh, el, eh], axis=1),
                          jnp.concatenate([vh, vh, vl], axis=0),
                          preferred_element_type=F32)
            dst = ds(qstart, Q_BLOCK, d)
            shape = (Q_BLOCK, LANES)
            m_s[p, dst, :] = jnp.where(head0, jnp.broadcast_to(m[:Q_BLOCK], shape),
                                       jnp.broadcast_to(m[Q_BLOCK:], shape))
            l_s[p, dst, :] = jnp.where(head0, jnp.broadcast_to(l[:Q_BLOCK], shape),
                                       jnp.broadcast_to(l[Q_BLOCK:], shape))
            a_s[p, dst, :] = jnp.where(head0, acc[:Q_BLOCK], acc[Q_BLOCK:])
            return carry

        lax.fori_loop(0, ATTN_SPAN // Q_BLOCK, body, 0)

    def merge(c, carry):
        rows = pl.ds(pl.multiple_of(c * Q_BLOCK, Q_BLOCK), Q_BLOCK)
        ms = [m_s[p, rows, :] for p in range(N_PATTERNS)]
        m_all = functools.reduce(jnp.maximum, ms)
        ws = [jnp.exp(m - m_all) for m in ms]
        num = sum(w * a_s[p, rows, :] for p, w in enumerate(ws))
        den = sum(w * l_s[p, rows, :] for p, w in enumerate(ws))
        o_ref[rows, :] = num / den
        return carry

    lax.fori_loop(0, ATTN_SPAN // Q_BLOCK, merge, 0)


def _attn_prompt(q, k, v, batch, seq):
    nspan = seq // ATTN_SPAN
    blk = (ATTN_SPAN, LANES)
    cur = pl.BlockSpec(blk, lambda b, hp, s: (b * nspan + s, hp))
    prev = pl.BlockSpec(blk, lambda b, hp, s: (b * nspan + jnp.maximum(s - 1, 0), hp))
    acc = pltpu.VMEM((N_PATTERNS, ATTN_SPAN, LANES), F32)
    return pl.pallas_call(
        _attn_prompt_kernel,
        grid=(batch, D_A // LANES, nspan),
        in_specs=[cur, prev, cur, prev, cur],
        out_specs=cur,
        out_shape=jax.ShapeDtypeStruct((batch * seq, D_A), F32),
        scratch_shapes=[pltpu.VMEM((2 * ATTN_SPAN, LANES), F32), pltpu.VMEM((2 * ATTN_SPAN, LANES), F32),
                        acc, acc, acc],
        compiler_params=_cparams(("parallel", "parallel", "arbitrary")),
        name="attn_prompt",
    )(q, k, k, v, v)


def _attn_sample_kernel(q_ref, kn_ref, vn_ref, ck_ref, cv_ref, o_ref, *, w_buf, t_new):
    pad = LANES - t_new
    zeros = jnp.zeros((pad, D_A), F32)
    kall = jnp.concatenate([ck_ref[...], kn_ref[...], zeros], axis=0).astype(BF16)
    vall = jnp.concatenate([cv_ref[...], vn_ref[...], zeros], axis=0).astype(BF16)
    n_keys = w_buf + LANES
    n_rows = N_HEADS_A * t_new
    hrow = lax.broadcasted_iota(jnp.int32, (n_rows, D_A), 0) // t_new
    hlane = lax.broadcasted_iota(jnp.int32, (n_rows, D_A), 1) // HEAD_DIM
    own = hrow == hlane
    q_rep = jnp.concatenate([q_ref[...]] * N_HEADS_A, axis=0)
    q64 = jnp.where(own, q_rep, 0.0).astype(BF16)
    s = lax.dot_general(q64, kall, (((1,), (1,)), ((), ())), preferred_element_type=F32)
    t = lax.broadcasted_iota(jnp.int32, (n_rows, n_keys), 0) % t_new
    c = lax.broadcasted_iota(jnp.int32, (n_rows, n_keys), 1)
    dist = w_buf + t - c
    ms, ls, accs = [], [], []
    for w, d in DILATED_PATTERNS:
        ok = (dist >= 0) & (dist <= w) & ((dist & (d - 1)) == 0)
        sp = jnp.where(ok, s, -jnp.inf)
        m = jnp.max(sp, axis=-1, keepdims=True)
        e = jnp.exp(sp - m)
        ms.append(m)
        ls.append(jnp.sum(e, axis=-1, keepdims=True))
        accs.append(jnp.dot(e.astype(BF16), vall, preferred_element_type=F32))
    m_all = functools.reduce(jnp.maximum, ms)
    ws = [jnp.exp(m - m_all) for m in ms]
    num = sum(w * a for w, a in zip(ws, accs))
    den = sum(w * l for w, l in zip(ws, ls))
    o = jnp.where(own, num / den, 0.0)
    out = o[0:t_new]
    for h in range(1, N_HEADS_A):
        out = out + o[h * t_new:(h + 1) * t_new]
    o_ref[...] = out


def _attn_sample(q, k, v, cache_k, cache_v, layer, row0, batch, t_new):
    w_buf = cache_k.shape[2]
    assert row0 % t_new == 0 and t_new == SUBLANES
    new = pl.BlockSpec((t_new, D_A), lambda b: (row0 // t_new + b, 0))
    cache = pl.BlockSpec((None, None, w_buf, D_A), lambda b: (layer, b, 0, 0))
    return pl.pallas_call(
        functools.partial(_attn_sample_kernel, w_buf=w_buf, t_new=t_new),
        grid=(batch,),
        in_specs=[new, new, new, cache, cache],
        out_specs=pl.BlockSpec((t_new, D_A), lambda b: (b, 0)),
        out_shape=jax.ShapeDtypeStruct((batch * t_new, D_A), F32),
        compiler_params=_cparams(("parallel",)),
        name="attn_sample",
    )(q, k, v, cache_k, cache_v)


def _gelu_tanh(x):
    return 0.5 * x * (1.0 + jnp.tanh(0.7978845608028654 * (x + 0.044715 * x * x * x)))


def _mixer_bc_kernel(bc_ref, cb_ref, h0_ref, ws_ref, bs_ref, cw_ref, cbias_ref, wa_ref, ba_ref,
                     wx_ref, bx_ref, lam_ref, ob_ref, oc_ref, hl_ref, xp_s, h_s, *, tt, last_row):
    j = pl.program_id(1)

    @pl.when(j == 0)
    def _():
        xp_s[0:SUBLANES, :] = cb_ref[...]
        h_s[...] = h0_ref[...]

    nch = tt // CHUNK
    vcat = jnp.concatenate([bc_ref[c * CHUNK:(c + 1) * CHUNK, D_B:2 * D_B] for c in range(nch)],
                           axis=1).astype(BF16)
    ri = lax.broadcasted_iota(jnp.int32, (CHUNK, CHUNK), 0)
    ci = lax.broadcasted_iota(jnp.int32, (CHUNK, CHUNK), 1)
    hl = (lax.broadcasted_iota(jnp.int32, (CHUNK, nch * D_B), 1) % D_B) // HEAD_DIM
    mixed = jnp.zeros((CHUNK, nch * D_B), F32)
    for h in range(N_HEADS_B):
        wh = jnp.where(ri >= ci, ws_ref[h], 0.0).astype(BF16)
        mh = jnp.dot(wh, vcat, preferred_element_type=F32)
        mixed = mixed + jnp.where(hl == h, mh, 0.0)
    for c in range(nch):
        rows = slice(c * CHUNK, (c + 1) * CHUNK)
        ob_ref[rows, :] = bc_ref[rows, 0:D_B] * (mixed[:, c * D_B:(c + 1) * D_B] + bs_ref[...])

    xc = bc_ref[:, 2 * D_B:2 * D_B + D_C]
    xp_s[SUBLANES:SUBLANES + tt, :] = xc
    xconv = cbias_ref[...] + cw_ref[CONV_W - 1:CONV_W, :] * xc
    for kk in range(CONV_W - 1):
        off = SUBLANES - (CONV_W - 1) + kk
        xconv = xconv + cw_ref[kk:kk + 1, :] * xp_s[off:off + tt, :]
    xp_s[0:SUBLANES, :] = xp_s[tt:tt + SUBLANES, :]
    xb = xconv.astype(BF16)
    r = jax.nn.sigmoid(jnp.dot(xb, wa_ref[...], preferred_element_type=F32) + ba_ref[...])
    i = jax.nn.sigmoid(jnp.dot(xb, wx_ref[...], preferred_element_type=F32) + bx_ref[...])
    nl = -lam_ref[...]
    softplus = jnp.maximum(nl, 0.0) + jnp.log1p(jnp.exp(-jnp.abs(nl)))
    a = jnp.exp(-LRU_C * r * softplus)
    b = jnp.sqrt(1.0 - a * a) * (i * xconv)
    rowi = lax.broadcasted_iota(jnp.int32, (tt, D_C), 0)
    step = 1
    while step < tt:
        a_sh = pltpu.roll(a, step, axis=0)
        b_sh = pltpu.roll(b, step, axis=0)
        live = rowi >= step
        b = jnp.where(live, a * b_sh + b, b)
        a = jnp.where(live, a * a_sh, a)
        step *= 2
    h = a * h_s[...] + b
    h_s[...] = h[tt - 1:tt, :]
    oc_ref[...] = h * _gelu_tanh(bc_ref[:, 2 * D_B + D_C:])

    @pl.when(j == pl.num_programs(1) - 1)
    def _():
        hl_ref[...] = h[last_row:last_row + 1, :]


def _mixer_bc(bc, conv_buf8, h0, lw, batch, t_len, tt, last_row):
    nt = t_len // tt
    rows = lambda w: pl.BlockSpec((tt, w), lambda b, j: (b * nt + j, 0))
    per_b = lambda s: pl.BlockSpec((None,) + s, lambda b, j: (b,) + (0,) * len(s))
    return pl.pallas_call(
        functools.partial(_mixer_bc_kernel, tt=tt, last_row=last_row),
        grid=(batch, nt),
        in_specs=[rows(2 * D_B + 2 * D_C), per_b((SUBLANES, D_C)), per_b((1, D_C)),
                  _full((N_HEADS_B, CHUNK, CHUNK)), _full((CHUNK, D_B)), _full((CONV_W, D_C)),
                  _full((1, D_C)), _full((D_C, D_C)), _full((1, D_C)), _full((D_C, D_C)),
                  _full((1, D_C)), _full((1, D_C))],
        out_specs=[rows(D_B), rows(D_C), per_b((1, D_C))],
        out_shape=[jax.ShapeDtypeStruct((batch * t_len, D_B), F32),
                   jax.ShapeDtypeStruct((batch * t_len, D_C), F32),
                   jax.ShapeDtypeStruct((batch, 1, D_C), F32)],
        scratch_shapes=[pltpu.VMEM((tt + SUBLANES, D_C), F32), pltpu.VMEM((1, D_C), F32)],
        compiler_params=_cparams(("parallel", "arbitrary")),
        name="mixer_bc",
    )(bc, conv_buf8, h0, lw["w_s"], lw["bias_s"], lw["conv_w"], lw["conv_b"], lw["w_a_bd"],
      lw["b_a"], lw["w_x_bd"], lw["b_x"], lw["lru_lambda"])


def _out_proj_kernel(oa_ref, ob_ref, oc_ref, x_ref, go_ref, wo_ref, g2_ref, wr_ref, br_ref,
                     xn_ref, h2_ref, idx_ref, gate_ref, rank_ref, cnt_ref):
    oa = _rms(oa_ref[...], go_ref[:, 0:D_A]).astype(BF16)
    ob = _rms(ob_ref[...], go_ref[:, D_A:D_A + D_B]).astype(BF16)
    oc = _rms(oc_ref[...], go_ref[:, D_A + D_B:]).astype(BF16)
    y = (jnp.dot(oa, wo_ref[0:D_A, :], preferred_element_type=F32)
         + jnp.dot(ob, wo_ref[D_A:D_A + D_B, :], preferred_element_type=F32)
         + jnp.dot(oc, wo_ref[D_A + D_B:, :], preferred_element_type=F32))
    xn = x_ref[...] + y
    xn_ref[...] = xn
    h2 = _rms(xn, g2_ref[...])
    h2_ref[...] = h2
    logits = jnp.dot(h2.astype(BF16), wr_ref[...], preferred_element_type=F32) + br_ref[...]
    lane = lax.broadcasted_iota(jnp.int32, logits.shape, 1).astype(F32)
    cur = logits
    tops, idxs = [], []
    for _ in range(TOP_K):
        m = jnp.max(cur, axis=-1, keepdims=True)
        ix = jnp.min(jnp.where(cur == m, lane, float(ROUTER_PAD)), axis=-1, keepdims=True)
        tops.append(m)
        idxs.append(ix)
        cur = jnp.where(lane == ix, -jnp.inf, cur)
    es = [jnp.exp(t - tops[0]) for t in tops]
    den = sum(es)
    idx_out = jnp.zeros(logits.shape, F32)
    gate_out = jnp.zeros(logits.shape, F32)
    for kk in range(TOP_K):
        idx_out = jnp.where(lane == kk, idxs[kk], idx_out)
        gate_out = jnp.where(lane == kk, es[kk] / den, gate_out)
    idx_ref[...] = idx_out.astype(jnp.int32)
    gate_ref[...] = gate_out
    tm = logits.shape[0]
    chosen = jnp.zeros(logits.shape, F32)
    for kk in range(TOP_K):
        chosen = jnp.where(lane == idxs[kk], 1.0, chosen)
    below = (lax.broadcasted_iota(jnp.int32, (tm, tm), 0)
             > lax.broadcasted_iota(jnp.int32, (tm, tm), 1)).astype(BF16)
    earlier = jnp.dot(below, chosen.astype(BF16), preferred_element_type=F32)
    rank_out = jnp.zeros(logits.shape, F32)
    for kk in range(TOP_K):
        rk = jnp.sum(jnp.where(lane == idxs[kk], earlier, 0.0), axis=-1, keepdims=True)
        rank_out = jnp.where(lane == kk, rk, rank_out)
    rank_ref[...] = rank_out.astype(jnp.int32)
    cnt_ref[...] = jnp.broadcast_to(jnp.sum(chosen, axis=0, keepdims=True),
                                    cnt_ref.shape).astype(jnp.int32)


def _out_proj(oa, ob, oc, x, lw):
    n = x.shape[0]
    tm = TOKEN_TILE
    row = lambda w: pl.BlockSpec((tm, w), lambda i: (i, 0))
    return pl.pallas_call(
        _out_proj_kernel,
        grid=(n // tm,),
        in_specs=[row(D_A), row(D_B), row(D_C), row(D_MODEL), _full((1, D_MODEL)),
                  _full((D_MODEL, D_MODEL)), _full((1, D_MODEL)), _full((D_MODEL, ROUTER_PAD)),
                  _full((1, ROUTER_PAD))],
        out_specs=[row(D_MODEL), row(D_MODEL), row(ROUTER_PAD), row(ROUTER_PAD), row(ROUTER_PAD),
                   pl.BlockSpec((None, SUBLANES, ROUTER_PAD), lambda i: (i, 0, 0))],
        out_shape=[jax.ShapeDtypeStruct((n, D_MODEL), F32), jax.ShapeDtypeStruct((n, D_MODEL), F32),
                   jax.ShapeDtypeStruct((n, ROUTER_PAD), jnp.int32),
                   jax.ShapeDtypeStruct((n, ROUTER_PAD), F32),
                   jax.ShapeDtypeStruct((n, ROUTER_PAD), jnp.int32),
                   jax.ShapeDtypeStruct((n // tm, SUBLANES, ROUTER_PAD), jnp.int32)],
        compiler_params=_cparams(("parallel",)),
        name="out_proj",
    )(oa, ob, oc, x, lw["g_out"], lw["w_out"], lw["ln2_g"], lw["w_router"], lw["b_router"])


def _moe_kernel(te_ref, nu_ref, x_ref, wgu_ref, bg_ref, bl_ref, wd_ref, bd_ref, y_ref, wg_s, wl_s, wd_s):
    i = pl.program_id(0)
    used = i < nu_ref[0]
    new_expert = jnp.logical_or(i == 0, te_ref[i] != te_ref[jnp.maximum(i - 1, 0)])

    @pl.when(jnp.logical_and(used, new_expert))
    def _():
        rc = CHUNK
        lane = lax.broadcasted_iota(jnp.int32, (rc, LANES), 1)
        low = lane < LANES // 2
        even_first = jnp.where(low, 2 * lane, 2 * lane - LANES + 1)
        odd_first = jnp.where(low, 2 * lane + 1, 2 * lane - LANES)

        def body(r, carry):
            rows = pl.ds(pl.multiple_of(r * rc, rc), rc)
            for c in range(D_FF // LANES):
                a = wgu_ref[rows, 2 * c * LANES:(2 * c + 1) * LANES]
                b = wgu_ref[rows, (2 * c + 1) * LANES:(2 * c + 2) * LANES]
                take = lambda t, ix: jnp.take_along_axis(t, ix, axis=1)
                wg_s[rows, c * LANES:(c + 1) * LANES] = jnp.where(
                    low, take(a, even_first), take(b, odd_first)).astype(BF16)
                wl_s[rows, c * LANES:(c + 1) * LANES] = jnp.where(
                    low, take(a, odd_first), take(b, even_first)).astype(BF16)
            wd_s[rows, :] = wd_ref[rows, :].astype(BF16)
            return carry

        lax.fori_loop(0, D_MODEL // rc, body, 0)

    @pl.when(used)
    def _():
        x = x_ref[...].astype(BF16)
        zg = jnp.dot(x, wg_s[...], preferred_element_type=F32) + bg_ref[...]
        zl = jnp.dot(x, wl_s[...], preferred_element_type=F32) + bl_ref[...]
        glu = jnp.minimum(zg, SWIGLU_LIMIT)
        lin = jnp.clip(zl, -SWIGLU_LIMIT, SWIGLU_LIMIT)
        act = glu * jax.nn.sigmoid(SWIGLU_ALPHA * glu) * (lin + 1.0)
        y_ref[...] = jnp.dot(act.astype(BF16), wd_s[...], preferred_element_type=F32) + bd_ref[...]

    @pl.when(i >= nu_ref[0])
    def _():
        y_ref[...] = jnp.zeros(y_ref.shape, F32)


def _moe_ffn(xs, tile_expert, n_used, lw):
    n_slots = xs.shape[0]
    tm = MOE_TILE
    layer = lw["layer"]
    wspec = lambda r, c: pl.BlockSpec((None, r, c), lambda i, te, nu: (te[i], 0, 0))
    wfull = lambda r, c: pl.BlockSpec((None, None, r, c), lambda i, te, nu: (layer, te[i], 0, 0))
    return pl.pallas_call(
        _moe_kernel,
        grid_spec=pltpu.PrefetchScalarGridSpec(
            num_scalar_prefetch=2,
            grid=(n_slots // tm,),
            in_specs=[pl.BlockSpec((tm, D_MODEL), lambda i, te, nu: (i, 0)),
                      wfull(D_MODEL, 2 * D_FF), wspec(1, D_FF), wspec(1, D_FF),
                      wfull(D_FF, D_MODEL), wspec(1, D_MODEL)],
            out_specs=pl.BlockSpec((tm, D_MODEL), lambda i, te, nu: (i, 0)),
            scratch_shapes=[pltpu.VMEM((D_MODEL, D_FF), BF16), pltpu.VMEM((D_MODEL, D_FF), BF16),
                            pltpu.VMEM((D_FF, D_MODEL), BF16)],
        ),
        out_shape=jax.ShapeDtypeStruct((n_slots, D_MODEL), F32),
        compiler_params=_cparams(("arbitrary",)),
        name="moe_ffn",
    )(tile_expert, n_used, xs, lw["w_gu"], lw["b_glu"], lw["b_lin"], lw["w_down"], lw["b_down"])


def _route(idx, rank, counts):
    n_tok = idx.shape[0]
    totals = jnp.sum(counts, axis=0)
    tiles = (totals + MOE_TILE - 1) // MOE_TILE
    tile_end = jnp.cumsum(tiles)
    pad_start = (tile_end - tiles) * MOE_TILE
    base = pad_start[None, :] + jnp.cumsum(counts, axis=0) - counts
    base_tok = jnp.repeat(base, TOKEN_TILE, axis=0)
    picked = idx[:, :, None] == jnp.arange(N_EXPERTS, dtype=jnp.int32)[None, None, :]
    dest = jnp.sum(jnp.where(picked, base_tok[:, None, :], 0), axis=-1) + rank
    n_tiles = -(-(n_tok * TOP_K) // MOE_TILE) + N_EXPERTS
    tile_expert = jnp.minimum(
        jnp.searchsorted(tile_end, jnp.arange(n_tiles, dtype=jnp.int32), side="right"),
        N_EXPERTS - 1).astype(jnp.int32)
    n_used = tile_end[-1:].astype(jnp.int32)
    return dest.astype(jnp.int32), tile_expert, n_used, n_tiles


def _sc_mesh():
    return plsc.VectorSubcoreMesh(core_axis_name="core", subcore_axis_name="subcore")


def _sc_dispatch(h2, dest_g, n_slots):
    ng = dest_g.shape[0]

    @pl.kernel(out_type=jax.ShapeDtypeStruct((n_slots, D_MODEL), F32), mesh=_sc_mesh(),
               scratch_types=[pltpu.VMEM((1, LANES), jnp.int32), pltpu.VMEM((SC_GROUP, D_MODEL), F32)])
    def kernel(h_hbm, d_hbm, o_hbm, i_vmem, buf):
        wid = lax.axis_index("core") * SC_SUBCORES + lax.axis_index("subcore")

        @pl.loop(0, pl.cdiv(ng, SC_WORKERS))
        def _(r):
            g = wid + SC_WORKERS * r

            @pl.when(g < ng)
            def _():
                pltpu.sync_copy(d_hbm.at[pl.ds(g, 1)], i_vmem)
                pltpu.sync_copy(h_hbm.at[pl.ds(g * SC_GROUP, SC_GROUP)], buf)
                for k in range(TOP_K):
                    pltpu.sync_copy(buf, o_hbm.at[i_vmem.at[0, pl.ds(k * SC_GROUP, SC_GROUP)]])

    return kernel(h2, dest_g)


def _sc_collect(y, dest_g):
    ng = dest_g.shape[0]

    @pl.kernel(out_type=jax.ShapeDtypeStruct((ng * LANES, D_MODEL), F32), mesh=_sc_mesh(),
               scratch_types=[pltpu.VMEM((1, LANES), jnp.int32), pltpu.VMEM((SC_GROUP, D_MODEL), F32)])
    def kernel(y_hbm, d_hbm, o_hbm, i_vmem, buf):
        wid = lax.axis_index("core") * SC_SUBCORES + lax.axis_index("subcore")

        @pl.loop(0, pl.cdiv(ng, SC_WORKERS))
        def _(r):
            g = wid + SC_WORKERS * r

            @pl.when(g < ng)
            def _():
                pltpu.sync_copy(d_hbm.at[pl.ds(g, 1)], i_vmem)
                for k in range(TOP_K):
                    pltpu.sync_copy(y_hbm.at[i_vmem.at[0, pl.ds(k * SC_GROUP, SC_GROUP)]], buf)
                    pltpu.sync_copy(buf, o_hbm.at[pl.ds(g * LANES + k * SC_GROUP, SC_GROUP)])

    return kernel(y, dest_g)


def _combine_kernel(xn_ref, g_ref, y_ref, o_ref):
    for gi in range(TOKEN_TILE // SC_GROUP):
        rows = slice(gi * SC_GROUP, (gi + 1) * SC_GROUP)
        acc = xn_ref[rows, :]
        for k in range(TOP_K):
            r0 = gi * LANES + k * SC_GROUP
            acc = acc + g_ref[rows, k:k + 1] * y_ref[r0:r0 + SC_GROUP, :]
        o_ref[rows, :] = acc


def _combine(xn, gates, y4):
    n = xn.shape[0]
    tm = TOKEN_TILE
    row = lambda r, w: pl.BlockSpec((r, w), lambda i: (i, 0))
    return pl.pallas_call(
        _combine_kernel,
        grid=(n // tm,),
        in_specs=[row(tm, D_MODEL), row(tm, ROUTER_PAD), row(tm * TOP_K, D_MODEL)],
        out_specs=row(tm, D_MODEL),
        out_shape=jax.ShapeDtypeStruct((n, D_MODEL), F32),
        compiler_params=_cparams(("parallel",)),
        name="moe_combine",
    )(xn, gates, y4)


def _moe(xn, h2, idx, gates, rank, counts, lw):
    n_tok = xn.shape[0]
    dest, tile_expert, n_used, n_tiles = _route(idx[:, :TOP_K], rank[:, :TOP_K], counts)
    dest_g = dest.reshape(n_tok // SC_GROUP, SC_GROUP, TOP_K).transpose(0, 2, 1).reshape(-1, LANES)
    xs = _sc_dispatch(h2, dest_g, n_tiles * MOE_TILE)
    y = _moe_ffn(xs, tile_expert, n_used, lw)
    return _combine(xn, gates, _sc_collect(y, dest_g))


def _block_diag(w):
    g, a, b = w.shape
    out = jnp.zeros((g * a, g * b), w.dtype)
    for i in range(g):
        out = out.at[i * a:(i + 1) * a, i * b:(i + 1) * b].set(w[i])
    return out


def _layer_weights(l, p):
    wr = jnp.pad(p["w_router"][l], ((0, 0), (0, ROUTER_PAD - N_EXPERTS)))
    hd = jnp.arange(D_A) // HEAD_DIM
    return {
        "ln1_g": p["ln1_g"][l][None], "w_in": p["w_in"][l].astype(BF16),
        "g_q": jnp.tile(p["g_q"][l], N_HEADS_A)[None], "g_k": jnp.tile(p["g_k"][l], N_HEADS_A)[None],
        "g_vb": p["g_vb"][l][None],
        "ones_bd": (hd[:, None] == hd[None, :]).astype(BF16),
        "w_s": p["w_s"][l],
        "bias_s": jnp.repeat(p["b_s"][l].T, HEAD_DIM, axis=1),
        "conv_w": p["conv_w"][l], "conv_b": p["conv_b"][l][None],
        "w_a_bd": _block_diag(p["w_a"][l]).astype(BF16), "b_a": p["b_a"][l][None],
        "w_x_bd": _block_diag(p["w_x"][l]).astype(BF16), "b_x": p["b_x"][l][None],
        "lru_lambda": p["lru_lambda"][l][None],
        "g_out": p["g_out"][l][None], "w_out": p["w_out"][l].astype(BF16),
        "ln2_g": p["ln2_g"][l][None],
        "w_router": wr.astype(BF16),
        "b_router": jnp.pad(p["b_router"][l], (0, ROUTER_PAD - N_EXPERTS),
                            constant_values=NEG_BIG)[None],
        "layer": l, "w_gu": p["w_gu"], "w_down": p["w_down"],
        "b_glu": p["b_gu"][l][:, None, 0::2], "b_lin": p["b_gu"][l][:, None, 1::2],
        "b_down": p["b_down"][l][:, None, :],
    }


def kernel(x_prompt, x_sample, cache_win_k, cache_win_v, state_conv, state_lru, ln1_g, w_in, g_q, g_k,
           g_vb, w_s, b_s, conv_w, conv_b, w_a, b_a, w_x, b_x, lru_lambda, g_out, w_out, ln2_g, w_router,
           b_router, w_gu, b_gu, w_down, b_down):
    params = dict(ln1_g=ln1_g, w_in=w_in, g_q=g_q, g_k=g_k, g_vb=g_vb, w_s=w_s, b_s=b_s, conv_w=conv_w,
                  conv_b=conv_b, w_a=w_a, b_a=b_a, w_x=w_x, b_x=b_x, lru_lambda=lru_lambda, g_out=g_out,
                  w_out=w_out, ln2_g=ln2_g, w_router=w_router, b_router=b_router, w_gu=w_gu, b_gu=b_gu,
                  w_down=w_down, b_down=b_down)
    bp, sp, _ = x_prompt.shape
    bs, ss, _ = x_sample.shape
    depth = w_in.shape[0]
    n_p, n_s = bp * sp, bs * ss
    w_buf = cache_win_k.shape[2]
    keep = min(DILATED_PATTERNS[-1][0], sp)
    ck = cache_win_k.reshape(depth, bs, w_buf, D_A)
    cv = cache_win_v.reshape(depth, bs, w_buf, D_A)
    x = jnp.concatenate([x_prompt.reshape(n_p, D_MODEL), x_sample.reshape(n_s, D_MODEL)], axis=0)
    zero_conv = jnp.zeros((bp, SUBLANES, D_C), F32)
    zero_h = jnp.zeros((bp, 1, D_C), F32)
    outs = {name: [] for name in ("pk", "pv", "pconv", "plru", "sk", "sv", "sconv", "slru", "svb")}
    for l in range(depth):
        lw = _layer_weights(l, params)
        q, k, v, bc = _in_proj(x, lw["ln1_g"], lw["w_in"], lw["g_q"], lw["g_k"], lw["g_vb"], lw["ones_bd"])
        oa_p = _attn_prompt(q, k, v, bp, sp)
        ob_p, oc_p, h_p = _mixer_bc(bc, zero_conv, zero_h, lw, bp, sp, MIX_TILE, MIX_TILE - 1)
        oa_s = _attn_sample(q, k, v, ck, cv, l, n_p, bs, ss)
        bc_s = bc[n_p:].reshape(bs, ss, -1)
        bc_s_pad = jnp.pad(bc_s, ((0, 0), (0, CHUNK - ss), (0, 0))).reshape(bs * CHUNK, -1)
        conv8 = jnp.pad(state_conv[l], ((0, 0), (SUBLANES - (CONV_W - 1), 0), (0, 0)))
        ob_s, oc_s, h_s = _mixer_bc(bc_s_pad, conv8, state_lru[l][:, None, :], lw, bs, CHUNK, CHUNK, ss - 1)
        ob_s = ob_s.reshape(bs, CHUNK, D_B)[:, :ss].reshape(n_s, D_B)
        oc_s = oc_s.reshape(bs, CHUNK, D_C)[:, :ss].reshape(n_s, D_C)
        oa = jnp.concatenate([oa_p, oa_s], axis=0)
        ob = jnp.concatenate([ob_p, ob_s], axis=0)
        oc = jnp.concatenate([oc_p, oc_s], axis=0)
        xn, h2, idx, gates, rank, counts = _out_proj(oa, ob, oc, x, lw)
        x = _moe(xn, h2, idx, gates, rank, counts[:, 0, :N_EXPERTS], lw)

        k_p = k[:n_p].reshape(bp, sp, N_HEADS_A, HEAD_DIM)
        v_p = v[:n_p].reshape(bp, sp, N_HEADS_A, HEAD_DIM)
        outs["pk"].append(k_p[:, sp - keep:])
        outs["pv"].append(v_p[:, sp - keep:])
        xc_p = bc[:n_p, 2 * D_B:2 * D_B + D_C].reshape(bp, sp, D_C)
        outs["pconv"].append(xc_p[:, sp - (CONV_W - 1):])
        outs["plru"].append(h_p[:, 0])
        outs["sk"].append(k[n_p:].reshape(bs, ss, N_HEADS_A, HEAD_DIM))
        outs["sv"].append(v[n_p:].reshape(bs, ss, N_HEADS_A, HEAD_DIM))
        xpad_s = jnp.concatenate([state_conv[l], bc_s[:, :, 2 * D_B:2 * D_B + D_C]], axis=1)
        outs["sconv"].append(xpad_s[:, -(CONV_W - 1):])
        outs["slru"].append(h_s[:, 0])
        outs["svb"].append(bc_s[:, :, D_B:2 * D_B])
    y_p = x[:n_p].reshape(bp, sp, D_MODEL)
    y_s = x[n_p:].reshape(bs, ss, D_MODEL)
    st = lambda name: jnp.stack(outs[name])
    return (y_p, y_s, st("pk"), st("pv"), st("pconv"), st("plru"), st("sk"), st("sv"), st("sconv"),
            st("slru"), st("svb"))
```

```python
import functools

import jax
import jax.numpy as jnp
from jax import lax
from jax.experimental import pallas as pl
from jax.experimental.pallas import tpu as pltpu
from jax.experimental.pallas import tpu_sc as plsc

F32 = jnp.float32
BF16 = jnp.bfloat16

D_MODEL = 1024
HEAD_DIM = 64
N_HEADS_A = 8
D_A = N_HEADS_A * HEAD_DIM
N_HEADS_B = 4
D_B = N_HEADS_B * HEAD_DIM
N_GROUPS_C = 4
D_C = N_GROUPS_C * HEAD_DIM
D_IN = 3 * D_A + 2 * D_B + 2 * D_C
DILATED_PATTERNS = ((128, 1), (512, 4), (2048, 16))
N_PATTERNS = len(DILATED_PATTERNS)
CHUNK = 128
CONV_W = 4
LRU_C = 8.0
N_EXPERTS = 32
TOP_K = 4
D_FF = 1024
SWIGLU_LIMIT = 7.0
SWIGLU_ALPHA = 1.702
EPS = 1e-6
ATTN_SCALE = HEAD_DIM ** -0.5
PAST_LEN = 16384

LANES = 128
SUBLANES = 8
VMEM_LIMIT_BYTES = 56 * 1024 * 1024

Q_BLOCK = 128
ATTN_SPAN = 2048
ATTN_GROUP = 4
TOKEN_TILE = 256
MIX_TILE = 512
MOE_TILE = 256
ROUTER_PAD = LANES
SC_SUBCORES = 16
SC_WORKERS = 2 * SC_SUBCORES
SC_GROUP = LANES // TOP_K
NEG_BIG = -1e30


def _cparams(semantics):
    return pltpu.CompilerParams(dimension_semantics=semantics,
                                vmem_limit_bytes=VMEM_LIMIT_BYTES)


def _full(shape):
    return pl.BlockSpec(shape, lambda *_: (0,) * len(shape))


def _rms(t, g):
    ms = jnp.mean(t * t, axis=-1, keepdims=True)
    return t * lax.rsqrt(ms + EPS) * g


def _split_bf16(t):
    hi = t.astype(BF16)
    lo = (t - hi.astype(F32)).astype(BF16)
    return hi, lo


def _in_proj_kernel(x_ref, g1_ref, w_ref, gq_ref, gk_ref, gvb_ref, ones_ref,
                    q_ref, k_ref, v_ref, bc_ref):
    h = _rms(x_ref[...], g1_ref[...]).astype(BF16)
    z = jnp.dot(h, w_ref[...], preferred_element_type=F32)

    def head_norm(t, g):
        hi, lo = _split_bf16(t * t)
        ss = (jnp.dot(hi, ones_ref[...], preferred_element_type=F32)
              + jnp.dot(lo, ones_ref[...], preferred_element_type=F32))
        return t * lax.rsqrt(ss * (1.0 / HEAD_DIM) + EPS) * g

    q_ref[...] = head_norm(z[:, 0:D_A], gq_ref[...]) * ATTN_SCALE
    k_ref[...] = head_norm(z[:, D_A:2 * D_A], gk_ref[...])
    v_ref[...] = z[:, 2 * D_A:3 * D_A]
    o = 3 * D_A
    bc_ref[...] = z[:, o:]
    bc_ref[:, D_B:2 * D_B] = _rms(z[:, o + D_B:o + 2 * D_B], gvb_ref[...])


def _in_proj(x, g1, w_bf16, gq, gk, gvb, ones_bd):
    n = x.shape[0]
    tm = TOKEN_TILE
    row = lambda w: pl.BlockSpec((tm, w), lambda i: (i, 0))
    return pl.pallas_call(
        _in_proj_kernel,
        grid=(n // tm,),
        in_specs=[row(D_MODEL), _full((1, D_MODEL)), _full((D_MODEL, D_IN)), _full((1, D_A)),
                  _full((1, D_A)), _full((1, D_B)), _full((D_A, D_A))],
        out_specs=[row(D_A), row(D_A), row(D_A), row(2 * D_B + 2 * D_C)],
        out_shape=[jax.ShapeDtypeStruct((n, D_A), F32)] * 3
        + [jax.ShapeDtypeStruct((n, 2 * D_B + 2 * D_C), F32)],
        compiler_params=_cparams(("parallel",)),
        name="in_proj",
    )(x, g1, w_bf16, gq, gk, gvb, ones_bd)


def _attn_prompt_kernel(q_ref, kp_ref, kc_ref, vp_ref, vc_ref, o_ref, kq0a, kq0b, kq1a, kq1b, vqh, vql,
                        m_s, l_s, a_s):
    span = pl.program_id(2)
    qb2 = 2 * Q_BLOCK
    lane = lax.broadcasted_iota(jnp.int32, (Q_BLOCK, LANES), 1)
    head0 = lane < HEAD_DIM
    lane2 = lax.broadcasted_iota(jnp.int32, (qb2, LANES), 1)
    head0_2 = lane2 < HEAD_DIM
    swap = lambda t: pltpu.roll(t, HEAD_DIM, axis=1)

    def pack(k_ref, v_ref, base):
        def body(c, carry):
            src = pl.ds(pl.multiple_of(c * qb2, qb2), qb2)
            dst = pl.ds(pl.multiple_of(base + c * qb2, qb2), qb2)
            k = k_ref[src, :]
            kh = k.astype(BF16).astype(F32)
            kl = k - kh
            kq0a[dst, :] = jnp.where(head0_2, kh, swap(kh))
            kq0b[dst, :] = jnp.where(head0_2, kl, 0.0)
            kq1a[dst, :] = jnp.where(head0_2, swap(kh), kh)
            kq1b[dst, :] = jnp.where(head0_2, swap(kl), 0.0)
            v = v_ref[src, :]
            vh = v.astype(BF16).astype(F32)
            vqh[dst, :] = vh
            vql[dst, :] = v - vh
            return carry
        lax.fori_loop(0, ATTN_SPAN // qb2, body, 0)

    pack(kp_ref, vp_ref, 0)
    pack(kc_ref, vc_ref, ATTN_SPAN)

    row = lax.broadcasted_iota(jnp.int32, (qb2, qb2), 0) & (Q_BLOCK - 1)
    col = lax.broadcasted_iota(jnp.int32, (qb2, qb2), 1)
    band = (col >= row) & (col <= row + Q_BLOCK)
    cur = col >= Q_BLOCK
    nt = (((1,), (1,)), ((), ()))

    def ds(start, size, d):
        return pl.ds(start, size) if d == 1 else pl.ds(start, size, stride=d)

    for p, (_, d) in enumerate(DILATED_PATTERNS):
        nblk = ATTN_SPAN // (Q_BLOCK * d)

        def scores(blk, d=d, nblk=nblk):
            r = blk // nblk
            ib = blk % nblk
            qstart = r + d * Q_BLOCK * ib
            if d == 1:
                qstart = pl.multiple_of(Q_BLOCK * blk, Q_BLOCK)
            kstart = ATTN_SPAN + qstart - d * Q_BLOCK
            q = q_ref[ds(qstart, Q_BLOCK, d), :]
            qh = q.astype(BF16).astype(F32)
            ql = q - qh
            lhs0 = jnp.concatenate([jnp.where(head0, qh, swap(ql)), jnp.where(head0, qh, 0.0)], axis=1)
            lhs1 = jnp.concatenate([jnp.where(head0, swap(qh), ql), jnp.where(head0, swap(qh), 0.0)], axis=1)
            keys = ds(kstart, qb2, d)
            k0 = jnp.concatenate([kq0a[keys, :], kq0b[keys, :]], axis=1).astype(BF16)
            k1 = jnp.concatenate([kq1a[keys, :], kq1b[keys, :]], axis=1).astype(BF16)
            s0 = lax.dot_general(lhs0.astype(BF16), k0, nt, preferred_element_type=F32)
            s1 = lax.dot_general(lhs1.astype(BF16), k1, nt, preferred_element_type=F32)
            prev_ok = jnp.logical_or(ib > 0, span > 0)
            s = jnp.concatenate([s0, s1], axis=0)
            return jnp.where(band & (cur | prev_ok), s, -jnp.inf), qstart, keys

        def softmax(s):
            m = jnp.max(s, axis=-1, keepdims=True)
            e = jnp.exp(s - m)
            return m, e, jnp.sum(e, axis=-1, keepdims=True)

        def weighted(e, keys):
            eh, el = _split_bf16(e)
            vh = vqh[keys, :]
            rhs = jnp.concatenate([jnp.concatenate([vh, vql[keys, :]], axis=1),
                                   jnp.concatenate([vh, jnp.zeros_like(vh)], axis=1)], axis=0).astype(BF16)
            out = jnp.dot(jnp.concatenate([eh, el], axis=1), rhs, preferred_element_type=F32)
            return out[:, :LANES] + out[:, LANES:]

        def store(qstart, m, l, acc, p=p, d=d):
            dst = ds(qstart, Q_BLOCK, d)
            shape = (Q_BLOCK, LANES)
            m_s[p, dst, :] = jnp.where(head0, jnp.broadcast_to(m[:Q_BLOCK], shape),
                                       jnp.broadcast_to(m[Q_BLOCK:], shape))
            l_s[p, dst, :] = jnp.where(head0, jnp.broadcast_to(l[:Q_BLOCK], shape),
                                       jnp.broadcast_to(l[Q_BLOCK:], shape))
            a_s[p, dst, :] = jnp.where(head0, acc[:Q_BLOCK], acc[Q_BLOCK:])

        def body(it, carry):
            sc = [scores(it * ATTN_GROUP + g) for g in range(ATTN_GROUP)]
            sm = [softmax(s) for s, _, _ in sc]
            ac = [weighted(e, keys) for (_, e, _), (_, _, keys) in zip(sm, sc)]
            for (_, qstart, _), (m, _, l), acc in zip(sc, sm, ac):
                store(qstart, m, l, acc)
            return carry

        lax.fori_loop(0, ATTN_SPAN // (Q_BLOCK * ATTN_GROUP), body, 0)

    def merge(c, carry):
        rows = pl.ds(pl.multiple_of(c * Q_BLOCK, Q_BLOCK), Q_BLOCK)
        ms = [m_s[p, rows, :] for p in range(N_PATTERNS)]
        m_all = functools.reduce(jnp.maximum, ms)
        ws = [jnp.exp(m - m_all) for m in ms]
        num = sum(w * a_s[p, rows, :] for p, w in enumerate(ws))
        den = sum(w * l_s[p, rows, :] for p, w in enumerate(ws))
        o_ref[rows, :] = num / den
        return carry

    lax.fori_loop(0, ATTN_SPAN // Q_BLOCK, merge, 0)


def _attn_prompt(q, k, v, batch, seq):
    nspan = seq // ATTN_SPAN
    blk = (ATTN_SPAN, LANES)
    cur = pl.BlockSpec(blk, lambda b, hp, s: (b * nspan + s, hp))
    prev = pl.BlockSpec(blk, lambda b, hp, s: (b * nspan + jnp.maximum(s - 1, 0), hp))
    acc = pltpu.VMEM((N_PATTERNS, ATTN_SPAN, LANES), F32)
    packed = pltpu.VMEM((2 * ATTN_SPAN, LANES), F32)
    return pl.pallas_call(
        _attn_prompt_kernel,
        grid=(batch, D_A // LANES, nspan),
        in_specs=[cur, prev, cur, prev, cur],
        out_specs=cur,
        out_shape=jax.ShapeDtypeStruct((batch * seq, D_A), F32),
        scratch_shapes=[packed] * 6 + [acc, acc, acc],
        compiler_params=_cparams(("parallel", "parallel", "arbitrary")),
        name="attn_prompt",
    )(q, k, k, v, v)


def _attn_sample_kernel(q_ref, kn_ref, vn_ref, ck_ref, cv_ref, o_ref, *, w_buf, t_new):
    pad = LANES - t_new
    zeros = jnp.zeros((pad, D_A), F32)
    heads = lambda c_ref: jnp.concatenate([c_ref[:, h, :] for h in range(N_HEADS_A)], axis=1)
    kall = jnp.concatenate([heads(ck_ref), kn_ref[...], zeros], axis=0).astype(BF16)
    vall = jnp.concatenate([heads(cv_ref), vn_ref[...], zeros], axis=0).astype(BF16)
    n_keys = w_buf + LANES
    n_rows = N_HEADS_A * t_new
    hrow = lax.broadcasted_iota(jnp.int32, (n_rows, D_A), 0) // t_new
    hlane = lax.broadcasted_iota(jnp.int32, (n_rows, D_A), 1) // HEAD_DIM
    own = hrow == hlane
    q_rep = jnp.concatenate([q_ref[...]] * N_HEADS_A, axis=0)
    q64 = jnp.where(own, q_rep, 0.0).astype(BF16)
    s = lax.dot_general(q64, kall, (((1,), (1,)), ((), ())), preferred_element_type=F32)
    t = lax.broadcasted_iota(jnp.int32, (n_rows, n_keys), 0) % t_new
    c = lax.broadcasted_iota(jnp.int32, (n_rows, n_keys), 1)
    dist = w_buf + t - c
    ms, ls, accs = [], [], []
    for w, d in DILATED_PATTERNS:
        ok = (dist >= 0) & (dist <= w) & ((dist & (d - 1)) == 0)
        sp = jnp.where(ok, s, -jnp.inf)
        m = jnp.max(sp, axis=-1, keepdims=True)
        e = jnp.exp(sp - m)
        ms.append(m)
        ls.append(jnp.sum(e, axis=-1, keepdims=True))
        accs.append(jnp.dot(e.astype(BF16), vall, preferred_element_type=F32))
    m_all = functools.reduce(jnp.maximum, ms)
    ws = [jnp.exp(m - m_all) for m in ms]
    num = sum(w * a for w, a in zip(ws, accs))
    den = sum(w * l for w, l in zip(ws, ls))
    o = jnp.where(own, num / den, 0.0)
    out = o[0:t_new]
    for h in range(1, N_HEADS_A):
        out = out + o[h * t_new:(h + 1) * t_new]
    o_ref[...] = out


def _attn_sample(q, k, v, cache_k, cache_v, layer, row0, batch, t_new):
    w_buf = cache_k.shape[2]
    assert row0 % t_new == 0 and t_new == SUBLANES
    new = pl.BlockSpec((t_new, D_A), lambda b: (row0 // t_new + b, 0))
    cache = pl.BlockSpec((None, None, w_buf, N_HEADS_A, HEAD_DIM), lambda b: (layer, b, 0, 0, 0))
    return pl.pallas_call(
        functools.partial(_attn_sample_kernel, w_buf=w_buf, t_new=t_new),
        grid=(batch,),
        in_specs=[new, new, new, cache, cache],
        out_specs=pl.BlockSpec((t_new, D_A), lambda b: (b, 0)),
        out_shape=jax.ShapeDtypeStruct((batch * t_new, D_A), F32),
        compiler_params=_cparams(("parallel",)),
        name="attn_sample",
    )(q, k, v, cache_k, cache_v)


def _export_kernel(*refs):
    k_ref, v_ref = refs[0], refs[1]
    pk_ref, pv_ref = refs[-2], refs[-1]
    for h in range(N_HEADS_A):
        cols = slice(h * HEAD_DIM, (h + 1) * HEAD_DIM)
        pk_ref[:, h, :] = k_ref[:, cols]
        pv_ref[:, h, :] = v_ref[:, cols]


def _export_window(k, v, prev, layer, depth, batch, seq, keep):
    tm = TOKEN_TILE
    first = (seq - keep) // tm
    src = pl.BlockSpec((tm, D_A), lambda b, j: (b * (seq // tm) + first + j, 0))
    dst = pl.BlockSpec((None, None, tm, N_HEADS_A, HEAD_DIM), lambda b, j: (layer, b, j, 0, 0))
    shape = jax.ShapeDtypeStruct((depth, batch, keep, N_HEADS_A, HEAD_DIM), F32)
    carried = [] if prev is None else list(prev)
    return pl.pallas_call(
        _export_kernel,
        grid=(batch, keep // tm),
        in_specs=[src, src] + [pl.BlockSpec(memory_space=pl.ANY)] * len(carried),
        out_specs=[dst, dst],
        out_shape=[shape, shape],
        input_output_aliases={2 + i: i for i in range(len(carried))},
        compiler_params=_cparams(("parallel", "parallel")),
        name="export_window",
    )(k, v, *carried)


def _gelu_tanh(x):
    return 0.5 * x * (1.0 + jnp.tanh(0.7978845608028654 * (x + 0.044715 * x * x * x)))


def _mixer_bc_kernel(bc_ref, cb_ref, h0_ref, ws_ref, bs_ref, cw_ref, cbias_ref, wa_ref, ba_ref,
                     wx_ref, bx_ref, lam_ref, ob_ref, oc_ref, hl_ref, xp_s, h_s, *, tt, last_row):
    j = pl.program_id(1)

    @pl.when(j == 0)
    def _():
        xp_s[0:SUBLANES, :] = cb_ref[...]
        h_s[...] = h0_ref[...]

    nch = tt // CHUNK
    vcat = jnp.concatenate([bc_ref[c * CHUNK:(c + 1) * CHUNK, D_B:2 * D_B] for c in range(nch)],
                           axis=1).astype(BF16)
    ri = lax.broadcasted_iota(jnp.int32, (CHUNK, CHUNK), 0)
    ci = lax.broadcasted_iota(jnp.int32, (CHUNK, CHUNK), 1)
    hl = (lax.broadcasted_iota(jnp.int32, (CHUNK, nch * D_B), 1) % D_B) // HEAD_DIM
    mixed = jnp.zeros((CHUNK, nch * D_B), F32)
    for h in range(N_HEADS_B):
        wh = jnp.where(ri >= ci, ws_ref[h], 0.0).astype(BF16)
        mh = jnp.dot(wh, vcat, preferred_element_type=F32)
        mixed = mixed + jnp.where(hl == h, mh, 0.0)
    for c in range(nch):
        rows = slice(c * CHUNK, (c + 1) * CHUNK)
        ob_ref[rows, :] = bc_ref[rows, 0:D_B] * (mixed[:, c * D_B:(c + 1) * D_B] + bs_ref[...])

    xc = bc_ref[:, 2 * D_B:2 * D_B + D_C]
    xp_s[SUBLANES:SUBLANES + tt, :] = xc
    xconv = cbias_ref[...] + cw_ref[CONV_W - 1:CONV_W, :] * xc
    for kk in range(CONV_W - 1):
        off = SUBLANES - (CONV_W - 1) + kk
        xconv = xconv + cw_ref[kk:kk + 1, :] * xp_s[off:off + tt, :]
    xp_s[0:SUBLANES, :] = xp_s[tt:tt + SUBLANES, :]
    xb = xconv.astype(BF16)
    r = jax.nn.sigmoid(jnp.dot(xb, wa_ref[...], preferred_element_type=F32) + ba_ref[...])
    i = jax.nn.sigmoid(jnp.dot(xb, wx_ref[...], preferred_element_type=F32) + bx_ref[...])
    nl = -lam_ref[...]
    softplus = jnp.maximum(nl, 0.0) + jnp.log1p(jnp.exp(-jnp.abs(nl)))
    a = jnp.exp(-LRU_C * r * softplus)
    b = jnp.sqrt(1.0 - a * a) * (i * xconv)
    rowi = lax.broadcasted_iota(jnp.int32, (tt, D_C), 0)
    step = 1
    while step < tt:
        a_sh = pltpu.roll(a, step, axis=0)
        b_sh = pltpu.roll(b, step, axis=0)
        live = rowi >= step
        b = jnp.where(live, a * b_sh + b, b)
        a = jnp.where(live, a * a_sh, a)
        step *= 2
    h = a * h_s[...] + b
    h_s[...] = h[tt - 1:tt, :]
    oc_ref[...] = h * _gelu_tanh(bc_ref[:, 2 * D_B + D_C:])

    @pl.when(j == pl.num_programs(1) - 1)
    def _():
        hl_ref[...] = h[last_row:last_row + 1, :]


def _mixer_bc(bc, conv_buf8, h0, lw, batch, t_len, tt, last_row):
    nt = t_len // tt
    rows = lambda w: pl.BlockSpec((tt, w), lambda b, j: (b * nt + j, 0))
    per_b = lambda s: pl.BlockSpec((None,) + s, lambda b, j: (b,) + (0,) * len(s))
    return pl.pallas_call(
        functools.partial(_mixer_bc_kernel, tt=tt, last_row=last_row),
        grid=(batch, nt),
        in_specs=[rows(2 * D_B + 2 * D_C), per_b((SUBLANES, D_C)), per_b((1, D_C)),
                  _full((N_HEADS_B, CHUNK, CHUNK)), _full((CHUNK, D_B)), _full((CONV_W, D_C)),
                  _full((1, D_C)), _full((D_C, D_C)), _full((1, D_C)), _full((D_C, D_C)),
                  _full((1, D_C)), _full((1, D_C))],
        out_specs=[rows(D_B), rows(D_C), per_b((1, D_C))],
        out_shape=[jax.ShapeDtypeStruct((batch * t_len, D_B), F32),
                   jax.ShapeDtypeStruct((batch * t_len, D_C), F32),
                   jax.ShapeDtypeStruct((batch, 1, D_C), F32)],
        scratch_shapes=[pltpu.VMEM((tt + SUBLANES, D_C), F32), pltpu.VMEM((1, D_C), F32)],
        compiler_params=_cparams(("parallel", "arbitrary")),
        name="mixer_bc",
    )(bc, conv_buf8, h0, lw["w_s"], lw["bias_s"], lw["conv_w"], lw["conv_b"], lw["w_a_bd"],
      lw["b_a"], lw["w_x_bd"], lw["b_x"], lw["lru_lambda"])


def _out_proj_kernel(oa_ref, ob_ref, oc_ref, x_ref, go_ref, wo_ref, g2_ref, wr_ref, br_ref,
                     xn_ref, h2_ref, idx_ref, gate_ref, rank_ref, cnt_ref):
    oa = _rms(oa_ref[...], go_ref[:, 0:D_A]).astype(BF16)
    ob = _rms(ob_ref[...], go_ref[:, D_A:D_A + D_B]).astype(BF16)
    oc = _rms(oc_ref[...], go_ref[:, D_A + D_B:]).astype(BF16)
    y = (jnp.dot(oa, wo_ref[0:D_A, :], preferred_element_type=F32)
         + jnp.dot(ob, wo_ref[D_A:D_A + D_B, :], preferred_element_type=F32)
         + jnp.dot(oc, wo_ref[D_A + D_B:, :], preferred_element_type=F32))
    xn = x_ref[...] + y
    xn_ref[...] = xn
    h2 = _rms(xn, g2_ref[...])
    h2_ref[...] = h2
    logits = jnp.dot(h2.astype(BF16), wr_ref[...], preferred_element_type=F32) + br_ref[...]
    lane = lax.broadcasted_iota(jnp.int32, logits.shape, 1).astype(F32)
    cur = logits
    tops, idxs = [], []
    for _ in range(TOP_K):
        m = jnp.max(cur, axis=-1, keepdims=True)
        ix = jnp.min(jnp.where(cur == m, lane, float(ROUTER_PAD)), axis=-1, keepdims=True)
        tops.append(m)
        idxs.append(ix)
        cur = jnp.where(lane == ix, -jnp.inf, cur)
    es = [jnp.exp(t - tops[0]) for t in tops]
    den = sum(es)
    idx_out = jnp.zeros(logits.shape, F32)
    gate_out = jnp.zeros(logits.shape, F32)
    for kk in range(TOP_K):
        idx_out = jnp.where(lane == kk, idxs[kk], idx_out)
        gate_out = jnp.where(lane == kk, es[kk] / den, gate_out)
    idx_ref[...] = idx_out.astype(jnp.int32)
    gate_ref[...] = gate_out
    tm = logits.shape[0]
    chosen = jnp.zeros(logits.shape, F32)
    for kk in range(TOP_K):
        chosen = jnp.where(lane == idxs[kk], 1.0, chosen)
    below = (lax.broadcasted_iota(jnp.int32, (tm, tm), 0)
             > lax.broadcasted_iota(jnp.int32, (tm, tm), 1)).astype(BF16)
    earlier = jnp.dot(below, chosen.astype(BF16), preferred_element_type=F32)
    rank_out = jnp.zeros(logits.shape, F32)
    for kk in range(TOP_K):
        rk = jnp.sum(jnp.where(lane == idxs[kk], earlier, 0.0), axis=-1, keepdims=True)
        rank_out = jnp.where(lane == kk, rk, rank_out)
    rank_ref[...] = rank_out.astype(jnp.int32)
    cnt_ref[...] = jnp.broadcast_to(jnp.sum(chosen, axis=0, keepdims=True),
                                    cnt_ref.shape).astype(jnp.int32)


def _out_proj(oa, ob, oc, x, lw):
    n = x.shape[0]
    tm = TOKEN_TILE
    row = lambda w: pl.BlockSpec((tm, w), lambda i: (i, 0))
    return pl.pallas_call(
        _out_proj_kernel,
        grid=(n // tm,),
        in_specs=[row(D_A), row(D_B), row(D_C), row(D_MODEL), _full((1, D_MODEL)),
                  _full((D_MODEL, D_MODEL)), _full((1, D_MODEL)), _full((D_MODEL, ROUTER_PAD)),
                  _full((1, ROUTER_PAD))],
        out_specs=[row(D_MODEL), row(D_MODEL), row(ROUTER_PAD), row(ROUTER_PAD), row(ROUTER_PAD),
                   pl.BlockSpec((None, SUBLANES, ROUTER_PAD), lambda i: (i, 0, 0))],
        out_shape=[jax.ShapeDtypeStruct((n, D_MODEL), F32), jax.ShapeDtypeStruct((n, D_MODEL), F32),
                   jax.ShapeDtypeStruct((n, ROUTER_PAD), jnp.int32),
                   jax.ShapeDtypeStruct((n, ROUTER_PAD), F32),
                   jax.ShapeDtypeStruct((n, ROUTER_PAD), jnp.int32),
                   jax.ShapeDtypeStruct((n // tm, SUBLANES, ROUTER_PAD), jnp.int32)],
        compiler_params=_cparams(("parallel",)),
        name="out_proj",
    )(oa, ob, oc, x, lw["g_out"], lw["w_out"], lw["ln2_g"], lw["w_router"], lw["b_router"])


def _moe_kernel(te_ref, nu_ref, x_ref, wgu_ref, bg_ref, bl_ref, wd_ref, bd_ref, y_ref, wg_s, wl_s, wd_s):
    i = pl.program_id(0)
    used = i < nu_ref[0]
    new_expert = jnp.logical_or(i == 0, te_ref[i] != te_ref[jnp.maximum(i - 1, 0)])

    @pl.when(jnp.logical_and(used, new_expert))
    def _():
        rc = CHUNK
        lane = lax.broadcasted_iota(jnp.int32, (rc, LANES), 1)
        low = lane < LANES // 2
        even_first = jnp.where(low, 2 * lane, 2 * lane - LANES + 1)
        odd_first = jnp.where(low, 2 * lane + 1, 2 * lane - LANES)

        def body(r, carry):
            rows = pl.ds(pl.multiple_of(r * rc, rc), rc)
            for c in range(D_FF // LANES):
                a = wgu_ref[rows, 2 * c * LANES:(2 * c + 1) * LANES]
                b = wgu_ref[rows, (2 * c + 1) * LANES:(2 * c + 2) * LANES]
                take = lambda t, ix: jnp.take_along_axis(t, ix, axis=1)
                wg_s[rows, c * LANES:(c + 1) * LANES] = jnp.where(
                    low, take(a, even_first), take(b, odd_first)).astype(BF16)
                wl_s[rows, c * LANES:(c + 1) * LANES] = jnp.where(
                    low, take(a, odd_first), take(b, even_first)).astype(BF16)
            wd_s[rows, :] = wd_ref[rows, :].astype(BF16)
            return carry

        lax.fori_loop(0, D_MODEL // rc, body, 0)

    @pl.when(used)
    def _():
        x = x_ref[...].astype(BF16)
        zg = jnp.dot(x, wg_s[...], preferred_element_type=F32) + bg_ref[...]
        zl = jnp.dot(x, wl_s[...], preferred_element_type=F32) + bl_ref[...]
        glu = jnp.minimum(zg, SWIGLU_LIMIT)
        lin = jnp.clip(zl, -SWIGLU_LIMIT, SWIGLU_LIMIT)
        act = glu * jax.nn.sigmoid(SWIGLU_ALPHA * glu) * (lin + 1.0)
        y_ref[...] = jnp.dot(act.astype(BF16), wd_s[...], preferred_element_type=F32) + bd_ref[...]

    @pl.when(i >= nu_ref[0])
    def _():
        y_ref[...] = jnp.zeros(y_ref.shape, F32)


def _moe_ffn(xs, tile_expert, n_used, lw):
    n_slots = xs.shape[0]
    tm = MOE_TILE
    layer = lw["layer"]
    wspec = lambda r, c: pl.BlockSpec((None, r, c), lambda i, te, nu: (te[i], 0, 0))
    wfull = lambda r, c: pl.BlockSpec((None, None, r, c), lambda i, te, nu: (layer, te[i], 0, 0))
    return pl.pallas_call(
        _moe_kernel,
        grid_spec=pltpu.PrefetchScalarGridSpec(
            num_scalar_prefetch=2,
            grid=(n_slots // tm,),
            in_specs=[pl.BlockSpec((tm, D_MODEL), lambda i, te, nu: (i, 0)),
                      wfull(D_MODEL, 2 * D_FF), wspec(1, D_FF), wspec(1, D_FF),
                      wfull(D_FF, D_MODEL), wspec(1, D_MODEL)],
            out_specs=pl.BlockSpec((tm, D_MODEL), lambda i, te, nu: (i, 0)),
            scratch_shapes=[pltpu.VMEM((D_MODEL, D_FF), BF16), pltpu.VMEM((D_MODEL, D_FF), BF16),
                            pltpu.VMEM((D_FF, D_MODEL), BF16)],
        ),
        out_shape=jax.ShapeDtypeStruct((n_slots, D_MODEL), F32),
        compiler_params=_cparams(("arbitrary",)),
        name="moe_ffn",
    )(tile_expert, n_used, xs, lw["w_gu"], lw["b_glu"], lw["b_lin"], lw["w_down"], lw["b_down"])


def _route(idx, rank, counts):
    n_tok = idx.shape[0]
    experts = jnp.arange(N_EXPERTS, dtype=jnp.int32)
    totals = jnp.sum(counts, axis=0)
    tiles = (totals + MOE_TILE - 1) // MOE_TILE
    tile_end = jnp.sum(jnp.where(experts[None, :] <= experts[:, None], tiles[None, :], 0), axis=1)
    pad_start = (tile_end - tiles) * MOE_TILE
    tt = jnp.arange(counts.shape[0], dtype=jnp.int32)
    before = jnp.sum(jnp.where((tt[None, :] < tt[:, None])[:, :, None], counts[None, :, :], 0), axis=1)
    base = pad_start[None, :] + before
    base_tok = jnp.repeat(base, TOKEN_TILE, axis=0)
    picked = idx[:, :, None] == experts[None, None, :]
    dest = jnp.sum(jnp.where(picked, base_tok[:, None, :], 0), axis=-1) + rank
    n_tiles = -(-(n_tok * TOP_K) // MOE_TILE) + N_EXPERTS
    tile_ids = jnp.arange(n_tiles, dtype=jnp.int32)
    tile_expert = jnp.minimum(jnp.sum((tile_end[None, :] <= tile_ids[:, None]).astype(jnp.int32), axis=1),
                              N_EXPERTS - 1)
    n_used = tile_end[-1:].astype(jnp.int32)
    return dest.astype(jnp.int32), tile_expert, n_used, n_tiles


def _sc_mesh():
    return plsc.VectorSubcoreMesh(core_axis_name="core", subcore_axis_name="subcore")


def _sc_dispatch(h2, dest_g, n_slots):
    ng = dest_g.shape[0]

    @pl.kernel(out_type=jax.ShapeDtypeStruct((n_slots, D_MODEL), F32), mesh=_sc_mesh(),
               scratch_types=[pltpu.VMEM((1, LANES), jnp.int32), pltpu.VMEM((SC_GROUP, D_MODEL), F32)])
    def kernel(h_hbm, d_hbm, o_hbm, i_vmem, buf):
        wid = lax.axis_index("core") * SC_SUBCORES + lax.axis_index("subcore")

        @pl.loop(0, pl.cdiv(ng, SC_WORKERS))
        def _(r):
            g = wid + SC_WORKERS * r

            @pl.when(g < ng)
            def _():
                pltpu.sync_copy(d_hbm.at[pl.ds(g, 1)], i_vmem)
                pltpu.sync_copy(h_hbm.at[pl.ds(g * SC_GROUP, SC_GROUP)], buf)
                for k in range(TOP_K):
                    pltpu.sync_copy(buf, o_hbm.at[i_vmem.at[0, pl.ds(k * SC_GROUP, SC_GROUP)]])

    return kernel(h2, dest_g)


def _sc_collect(y, dest_g):
    ng = dest_g.shape[0]

    @pl.kernel(out_type=jax.ShapeDtypeStruct((ng * LANES, D_MODEL), F32), mesh=_sc_mesh(),
               scratch_types=[pltpu.VMEM((1, LANES), jnp.int32), pltpu.VMEM((SC_GROUP, D_MODEL), F32)])
    def kernel(y_hbm, d_hbm, o_hbm, i_vmem, buf):
        wid = lax.axis_index("core") * SC_SUBCORES + lax.axis_index("subcore")

        @pl.loop(0, pl.cdiv(ng, SC_WORKERS))
        def _(r):
            g = wid + SC_WORKERS * r

            @pl.when(g < ng)
            def _():
                pltpu.sync_copy(d_hbm.at[pl.ds(g, 1)], i_vmem)
                for k in range(TOP_K):
                    pltpu.sync_copy(y_hbm.at[i_vmem.at[0, pl.ds(k * SC_GROUP, SC_GROUP)]], buf)
                    pltpu.sync_copy(buf, o_hbm.at[pl.ds(g * LANES + k * SC_GROUP, SC_GROUP)])

    return kernel(y, dest_g)


def _combine_kernel(xn_ref, g_ref, y_ref, o_ref):
    for gi in range(TOKEN_TILE // SC_GROUP):
        rows = slice(gi * SC_GROUP, (gi + 1) * SC_GROUP)
        acc = xn_ref[rows, :]
        for k in range(TOP_K):
            r0 = gi * LANES + k * SC_GROUP
            acc = acc + g_ref[rows, k:k + 1] * y_ref[r0:r0 + SC_GROUP, :]
        o_ref[rows, :] = acc


def _combine(xn, gates, y4):
    n = xn.shape[0]
    tm = TOKEN_TILE
    row = lambda r, w: pl.BlockSpec((r, w), lambda i: (i, 0))
    return pl.pallas_call(
        _combine_kernel,
        grid=(n // tm,),
        in_specs=[row(tm, D_MODEL), row(tm, ROUTER_PAD), row(tm * TOP_K, D_MODEL)],
        out_specs=row(tm, D_MODEL),
        out_shape=jax.ShapeDtypeStruct((n, D_MODEL), F32),
        compiler_params=_cparams(("parallel",)),
        name="moe_combine",
    )(xn, gates, y4)


def _moe(xn, h2, idx, gates, rank, counts, lw):
    n_tok = xn.shape[0]
    dest, tile_expert, n_used, n_tiles = _route(idx[:, :TOP_K], rank[:, :TOP_K], counts)
    dest_g = dest.reshape(n_tok // SC_GROUP, SC_GROUP, TOP_K).transpose(0, 2, 1).reshape(-1, LANES)
    xs = _sc_dispatch(h2, dest_g, n_tiles * MOE_TILE)
    y = _moe_ffn(xs, tile_expert, n_used, lw)
    return _combine(xn, gates, _sc_collect(y, dest_g))


def _block_diag(w):
    g, a, b = w.shape
    out = jnp.zeros((g * a, g * b), w.dtype)
    for i in range(g):
        out = out.at[i * a:(i + 1) * a, i * b:(i + 1) * b].set(w[i])
    return out


def _layer_weights(l, p):
    wr = jnp.pad(p["w_router"][l], ((0, 0), (0, ROUTER_PAD - N_EXPERTS)))
    hd = jnp.arange(D_A) // HEAD_DIM
    return {
        "ln1_g": p["ln1_g"][l][None], "w_in": p["w_in"][l].astype(BF16),
        "g_q": jnp.tile(p["g_q"][l], N_HEADS_A)[None], "g_k": jnp.tile(p["g_k"][l], N_HEADS_A)[None],
        "g_vb": p["g_vb"][l][None],
        "ones_bd": (hd[:, None] == hd[None, :]).astype(BF16),
        "w_s": p["w_s"][l],
        "bias_s": jnp.repeat(p["b_s"][l].T, HEAD_DIM, axis=1),
        "conv_w": p["conv_w"][l], "conv_b": p["conv_b"][l][None],
        "w_a_bd": _block_diag(p["w_a"][l]).astype(BF16), "b_a": p["b_a"][l][None],
        "w_x_bd": _block_diag(p["w_x"][l]).astype(BF16), "b_x": p["b_x"][l][None],
        "lru_lambda": p["lru_lambda"][l][None],
        "g_out": p["g_out"][l][None], "w_out": p["w_out"][l].astype(BF16),
        "ln2_g": p["ln2_g"][l][None],
        "w_router": wr.astype(BF16),
        "b_router": jnp.pad(p["b_router"][l], (0, ROUTER_PAD - N_EXPERTS),
                            constant_values=NEG_BIG)[None],
        "layer": l, "w_gu": p["w_gu"], "w_down": p["w_down"],
        "b_glu": p["b_gu"][l][:, None, 0::2], "b_lin": p["b_gu"][l][:, None, 1::2],
        "b_down": p["b_down"][l][:, None, :],
    }


def kernel(x_prompt, x_sample, cache_win_k, cache_win_v, state_conv, state_lru, ln1_g, w_in, g_q, g_k,
           g_vb, w_s, b_s, conv_w, conv_b, w_a, b_a, w_x, b_x, lru_lambda, g_out, w_out, ln2_g, w_router,
           b_router, w_gu, b_gu, w_down, b_down):
    params = dict(ln1_g=ln1_g, w_in=w_in, g_q=g_q, g_k=g_k, g_vb=g_vb, w_s=w_s, b_s=b_s, conv_w=conv_w,
                  conv_b=conv_b, w_a=w_a, b_a=b_a, w_x=w_x, b_x=b_x, lru_lambda=lru_lambda, g_out=g_out,
                  w_out=w_out, ln2_g=ln2_g, w_router=w_router, b_router=b_router, w_gu=w_gu, b_gu=b_gu,
                  w_down=w_down, b_down=b_down)
    bp, sp, _ = x_prompt.shape
    bs, ss, _ = x_sample.shape
    depth = w_in.shape[0]
    n_p, n_s = bp * sp, bs * ss
    keep = min(DILATED_PATTERNS[-1][0], sp)
    x = jnp.concatenate([x_prompt.reshape(n_p, D_MODEL), x_sample.reshape(n_s, D_MODEL)], axis=0)
    zero_conv = jnp.zeros((bp, SUBLANES, D_C), F32)
    zero_h = jnp.zeros((bp, 1, D_C), F32)
    outs = {name: [] for name in ("pconv", "plru", "sk", "sv", "sconv", "slru", "svb")}
    window = None
    for l in range(depth):
        lw = _layer_weights(l, params)
        q, k, v, bc = _in_proj(x, lw["ln1_g"], lw["w_in"], lw["g_q"], lw["g_k"], lw["g_vb"], lw["ones_bd"])
        oa_p = _attn_prompt(q, k, v, bp, sp)
        ob_p, oc_p, h_p = _mixer_bc(bc, zero_conv, zero_h, lw, bp, sp, MIX_TILE, MIX_TILE - 1)
        oa_s = _attn_sample(q, k, v, cache_win_k, cache_win_v, l, n_p, bs, ss)
        bc_s = bc[n_p:].reshape(bs, ss, -1)
        bc_s_pad = jnp.pad(bc_s, ((0, 0), (0, CHUNK - ss), (0, 0))).reshape(bs * CHUNK, -1)
        conv8 = jnp.pad(state_conv[l], ((0, 0), (SUBLANES - (CONV_W - 1), 0), (0, 0)))
        ob_s, oc_s, h_s = _mixer_bc(bc_s_pad, conv8, state_lru[l][:, None, :], lw, bs, CHUNK, CHUNK, ss - 1)
        ob_s = ob_s.reshape(bs, CHUNK, D_B)[:, :ss].reshape(n_s, D_B)
        oc_s = oc_s.reshape(bs, CHUNK, D_C)[:, :ss].reshape(n_s, D_C)
        oa = jnp.concatenate([oa_p, oa_s], axis=0)
        ob = jnp.concatenate([ob_p, ob_s], axis=0)
        oc = jnp.concatenate([oc_p, oc_s], axis=0)
        xn, h2, idx, gates, rank, counts = _out_proj(oa, ob, oc, x, lw)
        x = _moe(xn, h2, idx, gates, rank, counts[:, 0, :N_EXPERTS], lw)

        window = _export_window(k, v, window, l, depth, bp, sp, keep)
        xc_p = bc[:n_p, 2 * D_B:2 * D_B + D_C].reshape(bp, sp, D_C)
        outs["pconv"].append(xc_p[:, sp - (CONV_W - 1):])
        outs["plru"].append(h_p[:, 0])
        outs["sk"].append(k[n_p:].reshape(bs, ss, N_HEADS_A, HEAD_DIM))
        outs["sv"].append(v[n_p:].reshape(bs, ss, N_HEADS_A, HEAD_DIM))
        xpad_s = jnp.concatenate([state_conv[l], bc_s[:, :, 2 * D_B:2 * D_B + D_C]], axis=1)
        outs["sconv"].append(xpad_s[:, -(CONV_W - 1):])
        outs["slru"].append(h_s[:, 0])
        outs["svb"].append(bc_s[:, :, D_B:2 * D_B])
    y_p = x[:n_p].reshape(bp, sp, D_MODEL)
    y_s = x[n_p:].reshape(bs, ss, D_MODEL)
    st = lambda name: jnp.stack(outs[name])
    return (y_p, y_s, window[0], window[1], st("pconv"), st("plru"), st("sk"), st("sv"), st("sconv"),
            st("slru"), st("svb"))
```

```python
import functools

import jax
import jax.numpy as jnp
from jax import lax
from jax.experimental import pallas as pl
from jax.experimental.pallas import tpu as pltpu
from jax.experimental.pallas import tpu_sc as plsc

F32 = jnp.float32
BF16 = jnp.bfloat16

D_MODEL = 1024
HEAD_DIM = 64
N_HEADS_A = 8
D_A = N_HEADS_A * HEAD_DIM
N_HEADS_B = 4
D_B = N_HEADS_B * HEAD_DIM
N_GROUPS_C = 4
D_C = N_GROUPS_C * HEAD_DIM
D_IN = 3 * D_A + 2 * D_B + 2 * D_C
DILATED_PATTERNS = ((128, 1), (512, 4), (2048, 16))
N_PATTERNS = len(DILATED_PATTERNS)
CHUNK = 128
CONV_W = 4
LRU_C = 8.0
N_EXPERTS = 32
TOP_K = 4
D_FF = 1024
SWIGLU_LIMIT = 7.0
SWIGLU_ALPHA = 1.702
EPS = 1e-6
ATTN_SCALE = HEAD_DIM ** -0.5
PAST_LEN = 16384

LANES = 128
SUBLANES = 8
VMEM_LIMIT_BYTES = 56 * 1024 * 1024

Q_BLOCK = 128
ATTN_SPAN = 2048
ATTN_GROUP = 4
TOKEN_TILE = 256
MIX_TILE = 512
MOE_TILE = 256
ROUTER_PAD = LANES
SC_SUBCORES = 16
SC_WORKERS = 2 * SC_SUBCORES
SC_GROUP = LANES // TOP_K
NEG_BIG = -1e30


def _cparams(semantics):
    return pltpu.CompilerParams(dimension_semantics=semantics,
                                vmem_limit_bytes=VMEM_LIMIT_BYTES)


def _full(shape):
    return pl.BlockSpec(shape, lambda *_: (0,) * len(shape))


def _rms(t, g):
    ms = jnp.mean(t * t, axis=-1, keepdims=True)
    return t * lax.rsqrt(ms + EPS) * g


def _split_bf16(t):
    hi = t.astype(BF16)
    lo = (t - hi.astype(F32)).astype(BF16)
    return hi, lo


def _in_proj_kernel(x_ref, g1_ref, w_ref, gq_ref, gk_ref, gvb_ref, ones_ref,
                    q_ref, k_ref, v_ref, bc_ref):
    h = _rms(x_ref[...], g1_ref[...]).astype(BF16)
    z = jnp.dot(h, w_ref[...], preferred_element_type=F32)

    def head_norm(t, g):
        hi, lo = _split_bf16(t * t)
        ss = (jnp.dot(hi, ones_ref[...], preferred_element_type=F32)
              + jnp.dot(lo, ones_ref[...], preferred_element_type=F32))
        return t * lax.rsqrt(ss * (1.0 / HEAD_DIM) + EPS) * g

    q_ref[...] = head_norm(z[:, 0:D_A], gq_ref[...]) * ATTN_SCALE
    k_ref[...] = head_norm(z[:, D_A:2 * D_A], gk_ref[...])
    v_ref[...] = z[:, 2 * D_A:3 * D_A]
    o = 3 * D_A
    bc_ref[...] = z[:, o:]
    bc_ref[:, D_B:2 * D_B] = _rms(z[:, o + D_B:o + 2 * D_B], gvb_ref[...])


def _in_proj(x, g1, w_bf16, gq, gk, gvb, ones_bd):
    n = x.shape[0]
    tm = TOKEN_TILE
    row = lambda w: pl.BlockSpec((tm, w), lambda i: (i, 0))
    return pl.pallas_call(
        _in_proj_kernel,
        grid=(n // tm,),
        in_specs=[row(D_MODEL), _full((1, D_MODEL)), _full((D_MODEL, D_IN)), _full((1, D_A)),
                  _full((1, D_A)), _full((1, D_B)), _full((D_A, D_A))],
        out_specs=[row(D_A), row(D_A), row(D_A), row(2 * D_B + 2 * D_C)],
        out_shape=[jax.ShapeDtypeStruct((n, D_A), F32)] * 3
        + [jax.ShapeDtypeStruct((n, 2 * D_B + 2 * D_C), F32)],
        compiler_params=_cparams(("parallel",)),
        name="in_proj",
    )(x, g1, w_bf16, gq, gk, gvb, ones_bd)


def _attn_prompt_kernel(q_ref, kp_ref, kc_ref, vp_ref, vc_ref, o_ref, kq0a, kq0b, kq1a, kq1b, vqh, vql,
                        m_s, l_s, a_s):
    span = pl.program_id(2)
    qb2 = 2 * Q_BLOCK
    lane = lax.broadcasted_iota(jnp.int32, (Q_BLOCK, LANES), 1)
    head0 = lane < HEAD_DIM
    lane2 = lax.broadcasted_iota(jnp.int32, (qb2, LANES), 1)
    head0_2 = lane2 < HEAD_DIM
    swap = lambda t: pltpu.roll(t, HEAD_DIM, axis=1)

    def pack(k_ref, v_ref, base):
        def body(c, carry):
            src = pl.ds(pl.multiple_of(c * qb2, qb2), qb2)
            dst = pl.ds(pl.multiple_of(base + c * qb2, qb2), qb2)
            k = k_ref[src, :]
            kh = k.astype(BF16).astype(F32)
            kl = k - kh
            kq0a[dst, :] = jnp.where(head0_2, kh, swap(kh))
            kq0b[dst, :] = jnp.where(head0_2, kl, 0.0)
            kq1a[dst, :] = jnp.where(head0_2, swap(kh), kh)
            kq1b[dst, :] = jnp.where(head0_2, swap(kl), 0.0)
            v = v_ref[src, :]
            vh = v.astype(BF16).astype(F32)
            vqh[dst, :] = vh
            vql[dst, :] = v - vh
            return carry
        lax.fori_loop(0, ATTN_SPAN // qb2, body, 0)

    pack(kp_ref, vp_ref, 0)
    pack(kc_ref, vc_ref, ATTN_SPAN)

    row = lax.broadcasted_iota(jnp.int32, (qb2, qb2), 0) & (Q_BLOCK - 1)
    col = lax.broadcasted_iota(jnp.int32, (qb2, qb2), 1)
    band = (col >= row) & (col <= row + Q_BLOCK)
    cur = col >= Q_BLOCK
    nt = (((1,), (1,)), ((), ()))

    def ds(start, size, d):
        return pl.ds(start, size) if d == 1 else pl.ds(start, size, stride=d)

    for p, (_, d) in enumerate(DILATED_PATTERNS):
        nblk = ATTN_SPAN // (Q_BLOCK * d)

        def scores(blk, d=d, nblk=nblk):
            r = blk // nblk
            ib = blk % nblk
            qstart = r + d * Q_BLOCK * ib
            if d == 1:
                qstart = pl.multiple_of(Q_BLOCK * blk, Q_BLOCK)
            kstart = ATTN_SPAN + qstart - d * Q_BLOCK
            q = q_ref[ds(qstart, Q_BLOCK, d), :]
            qh = q.astype(BF16).astype(F32)
            ql = q - qh
            lhs0 = jnp.concatenate([jnp.where(head0, qh, swap(ql)), jnp.where(head0, qh, 0.0)], axis=1)
            lhs1 = jnp.concatenate([jnp.where(head0, swap(qh), ql), jnp.where(head0, swap(qh), 0.0)], axis=1)
            keys = ds(kstart, qb2, d)
            k0 = jnp.concatenate([kq0a[keys, :], kq0b[keys, :]], axis=1).astype(BF16)
            k1 = jnp.concatenate([kq1a[keys, :], kq1b[keys, :]], axis=1).astype(BF16)
            s0 = lax.dot_general(lhs0.astype(BF16), k0, nt, preferred_element_type=F32)
            s1 = lax.dot_general(lhs1.astype(BF16), k1, nt, preferred_element_type=F32)
            prev_ok = jnp.logical_or(ib > 0, span > 0)
            s = jnp.concatenate([s0, s1], axis=0)
            return jnp.where(band & (cur | prev_ok), s, -jnp.inf), qstart, keys

        def softmax(s):
            m = jnp.max(s, axis=-1, keepdims=True)
            e = jnp.exp(s - m)
            return m, e, jnp.sum(e, axis=-1, keepdims=True)

        def weighted(e, keys):
            eh, el = _split_bf16(e)
            vh = vqh[keys, :]
            rhs = jnp.concatenate([jnp.concatenate([vh, vql[keys, :]], axis=1),
                                   jnp.concatenate([vh, jnp.zeros_like(vh)], axis=1)], axis=0).astype(BF16)
            out = jnp.dot(jnp.concatenate([eh, el], axis=1), rhs, preferred_element_type=F32)
            return out[:, :LANES] + out[:, LANES:]

        def store(qstart, m, l, acc, p=p, d=d):
            dst = ds(qstart, Q_BLOCK, d)
            shape = (Q_BLOCK, LANES)
            m_s[p, dst, :] = jnp.where(head0, jnp.broadcast_to(m[:Q_BLOCK], shape),
                                       jnp.broadcast_to(m[Q_BLOCK:], shape))
            l_s[p, dst, :] = jnp.where(head0, jnp.broadcast_to(l[:Q_BLOCK], shape),
                                       jnp.broadcast_to(l[Q_BLOCK:], shape))
            a_s[p, dst, :] = jnp.where(head0, acc[:Q_BLOCK], acc[Q_BLOCK:])

        def body(it, carry):
            sc = [scores(it * ATTN_GROUP + g) for g in range(ATTN_GROUP)]
            sm = [softmax(s) for s, _, _ in sc]
            ac = [weighted(e, keys) for (_, e, _), (_, _, keys) in zip(sm, sc)]
            for (_, qstart, _), (m, _, l), acc in zip(sc, sm, ac):
                store(qstart, m, l, acc)
            return carry

        lax.fori_loop(0, ATTN_SPAN // (Q_BLOCK * ATTN_GROUP), body, 0)

    def merge(c, carry):
        rows = pl.ds(pl.multiple_of(c * Q_BLOCK, Q_BLOCK), Q_BLOCK)
        ms = [m_s[p, rows, :] for p in range(N_PATTERNS)]
        m_all = functools.reduce(jnp.maximum, ms)
        ws = [jnp.exp(m - m_all) for m in ms]
        num = sum(w * a_s[p, rows, :] for p, w in enumerate(ws))
        den = sum(w * l_s[p, rows, :] for p, w in enumerate(ws))
        o_ref[rows, :] = num / den
        return carry

    lax.fori_loop(0, ATTN_SPAN // Q_BLOCK, merge, 0)


def _attn_prompt(q, k, v, batch, seq):
    nspan = seq // ATTN_SPAN
    blk = (ATTN_SPAN, LANES)
    cur = pl.BlockSpec(blk, lambda b, hp, s: (b * nspan + s, hp))
    prev = pl.BlockSpec(blk, lambda b, hp, s: (b * nspan + jnp.maximum(s - 1, 0), hp))
    acc = pltpu.VMEM((N_PATTERNS, ATTN_SPAN, LANES), F32)
    packed = pltpu.VMEM((2 * ATTN_SPAN, LANES), F32)
    return pl.pallas_call(
        _attn_prompt_kernel,
        grid=(batch, D_A // LANES, nspan),
        in_specs=[cur, prev, cur, prev, cur],
        out_specs=cur,
        out_shape=jax.ShapeDtypeStruct((batch * seq, D_A), F32),
        scratch_shapes=[packed] * 6 + [acc, acc, acc],
        compiler_params=_cparams(("parallel", "parallel", "arbitrary")),
        name="attn_prompt",
    )(q, k, k, v, v)


def _attn_sample_kernel(q_ref, kn_ref, vn_ref, ckt_ref, cvt_ref, o_ref, *, w_buf, t_new):
    pad = LANES - t_new
    zeros = jnp.zeros((pad, D_A), F32)
    kn = jnp.concatenate([kn_ref[...], zeros], axis=0).astype(BF16)
    vn = jnp.concatenate([vn_ref[...], zeros], axis=0).astype(BF16)
    n_rows = N_HEADS_A * t_new
    hrow = lax.broadcasted_iota(jnp.int32, (n_rows, D_A), 0) // t_new
    hlane = lax.broadcasted_iota(jnp.int32, (n_rows, D_A), 1) // HEAD_DIM
    own = hrow == hlane
    q_rep = jnp.concatenate([q_ref[...]] * N_HEADS_A, axis=0)
    q64 = jnp.where(own, q_rep, 0.0).astype(BF16)
    nt = (((1,), (1,)), ((), ()))
    s_w = jnp.dot(q64, ckt_ref[...].astype(BF16), preferred_element_type=F32)
    s_n = lax.dot_general(q64, kn, nt, preferred_element_type=F32)
    vt = cvt_ref[...].astype(BF16)

    def dist(n_cols, first):
        t = lax.broadcasted_iota(jnp.int32, (n_rows, n_cols), 0) % t_new
        return t - lax.broadcasted_iota(jnp.int32, (n_rows, n_cols), 1) - first

    dist_w, dist_n = dist(w_buf, -w_buf), dist(LANES, 0)
    ms, ls, accs = [], [], []
    for w, d in DILATED_PATTERNS:
        ok = lambda ds: (ds >= 0) & (ds <= w) & ((ds & (d - 1)) == 0)
        sw = jnp.where(ok(dist_w), s_w, -jnp.inf)
        sn = jnp.where(ok(dist_n), s_n, -jnp.inf)
        m = jnp.maximum(jnp.max(sw, axis=-1, keepdims=True), jnp.max(sn, axis=-1, keepdims=True))
        ew = jnp.exp(sw - m)
        en = jnp.exp(sn - m)
        ms.append(m)
        ls.append(jnp.sum(ew, axis=-1, keepdims=True) + jnp.sum(en, axis=-1, keepdims=True))
        accs.append(lax.dot_general(ew.astype(BF16), vt, nt, preferred_element_type=F32)
                    + jnp.dot(en.astype(BF16), vn, preferred_element_type=F32))
    m_all = functools.reduce(jnp.maximum, ms)
    ws = [jnp.exp(m - m_all) for m in ms]
    num = sum(w * a for w, a in zip(ws, accs))
    den = sum(w * l for w, l in zip(ws, ls))
    o = jnp.where(own, num / den, 0.0)
    out = o[0:t_new]
    for h in range(1, N_HEADS_A):
        out = out + o[h * t_new:(h + 1) * t_new]
    o_ref[...] = out


def _attn_sample(q, k, v, cache_kt, cache_vt, layer, row0, batch, t_new):
    w_buf = cache_kt.shape[3]
    assert row0 % t_new == 0 and t_new == SUBLANES
    new = pl.BlockSpec((t_new, D_A), lambda b: (row0 // t_new + b, 0))
    cache = pl.BlockSpec((None, None, D_A, w_buf), lambda b: (layer, b, 0, 0))
    return pl.pallas_call(
        functools.partial(_attn_sample_kernel, w_buf=w_buf, t_new=t_new),
        grid=(batch,),
        in_specs=[new, new, new, cache, cache],
        out_specs=pl.BlockSpec((t_new, D_A), lambda b: (b, 0)),
        out_shape=jax.ShapeDtypeStruct((batch * t_new, D_A), F32),
        compiler_params=_cparams(("parallel",)),
        name="attn_sample",
    )(q, k, v, cache_kt, cache_vt)


def _export_kernel(*refs):
    k_ref, v_ref = refs[0], refs[1]
    pk_ref, pv_ref = refs[-2], refs[-1]
    pk_ref[...] = k_ref[...].T
    pv_ref[...] = v_ref[...].T


def _export_window(k, v, prev, layer, depth, batch, seq, keep):
    tm = TOKEN_TILE
    first = (seq - keep) // tm
    src = pl.BlockSpec((tm, D_A), lambda b, j: (b * (seq // tm) + first + j, 0))
    dst = pl.BlockSpec((None, None, D_A, tm), lambda b, j: (layer, b, 0, j))
    shape = jax.ShapeDtypeStruct((depth, batch, D_A, keep), F32)
    carried = [] if prev is None else list(prev)
    return pl.pallas_call(
        _export_kernel,
        grid=(batch, keep // tm),
        in_specs=[src, src] + [pl.BlockSpec(memory_space=pl.ANY)] * len(carried),
        out_specs=[dst, dst],
        out_shape=[shape, shape],
        input_output_aliases={2 + i: i for i in range(len(carried))},
        compiler_params=_cparams(("parallel", "parallel")),
        name="export_window",
    )(k, v, *carried)


def _gelu_tanh(x):
    return 0.5 * x * (1.0 + jnp.tanh(0.7978845608028654 * (x + 0.044715 * x * x * x)))


def _mixer_bc_kernel(bc_ref, cb_ref, h0_ref, ws_ref, bs_ref, cw_ref, cbias_ref, wa_ref, ba_ref,
                     wx_ref, bx_ref, lam_ref, ob_ref, oc_ref, hl_ref, xp_s, h_s, *, tt, last_row):
    j = pl.program_id(1)

    @pl.when(j == 0)
    def _():
        xp_s[0:SUBLANES, :] = cb_ref[...]
        h_s[...] = h0_ref[...]

    nch = tt // CHUNK
    vcat = jnp.concatenate([bc_ref[c * CHUNK:(c + 1) * CHUNK, D_B:2 * D_B] for c in range(nch)],
                           axis=1).astype(BF16)
    ri = lax.broadcasted_iota(jnp.int32, (CHUNK, CHUNK), 0)
    ci = lax.broadcasted_iota(jnp.int32, (CHUNK, CHUNK), 1)
    hl = (lax.broadcasted_iota(jnp.int32, (CHUNK, nch * D_B), 1) % D_B) // HEAD_DIM
    mixed = jnp.zeros((CHUNK, nch * D_B), F32)
    for h in range(N_HEADS_B):
        wh = jnp.where(ri >= ci, ws_ref[h], 0.0).astype(BF16)
        mh = jnp.dot(wh, vcat, preferred_element_type=F32)
        mixed = mixed + jnp.where(hl == h, mh, 0.0)
    for c in range(nch):
        rows = slice(c * CHUNK, (c + 1) * CHUNK)
        ob_ref[rows, :] = bc_ref[rows, 0:D_B] * (mixed[:, c * D_B:(c + 1) * D_B] + bs_ref[...])

    xc = bc_ref[:, 2 * D_B:2 * D_B + D_C]
    xp_s[SUBLANES:SUBLANES + tt, :] = xc
    xconv = cbias_ref[...] + cw_ref[CONV_W - 1:CONV_W, :] * xc
    for kk in range(CONV_W - 1):
        off = SUBLANES - (CONV_W - 1) + kk
        xconv = xconv + cw_ref[kk:kk + 1, :] * xp_s[off:off + tt, :]
    xp_s[0:SUBLANES, :] = xp_s[tt:tt + SUBLANES, :]
    xb = xconv.astype(BF16)
    r = jax.nn.sigmoid(jnp.dot(xb, wa_ref[...], preferred_element_type=F32) + ba_ref[...])
    i = jax.nn.sigmoid(jnp.dot(xb, wx_ref[...], preferred_element_type=F32) + bx_ref[...])
    nl = -lam_ref[...]
    softplus = jnp.maximum(nl, 0.0) + jnp.log1p(jnp.exp(-jnp.abs(nl)))
    a = jnp.exp(-LRU_C * r * softplus)
    b = jnp.sqrt(1.0 - a * a) * (i * xconv)
    rowi = lax.broadcasted_iota(jnp.int32, (tt, D_C), 0)
    step = 1
    while step < tt:
        a_sh = pltpu.roll(a, step, axis=0)
        b_sh = pltpu.roll(b, step, axis=0)
        live = rowi >= step
        b = jnp.where(live, a * b_sh + b, b)
        a = jnp.where(live, a * a_sh, a)
        step *= 2
    h = a * h_s[...] + b
    h_s[...] = h[tt - 1:tt, :]
    oc_ref[...] = h * _gelu_tanh(bc_ref[:, 2 * D_B + D_C:])

    @pl.when(j == pl.num_programs(1) - 1)
    def _():
        hl_ref[...] = h[last_row:last_row + 1, :]


def _mixer_bc(bc, conv_buf8, h0, lw, batch, t_len, tt, last_row):
    nt = t_len // tt
    rows = lambda w: pl.BlockSpec((tt, w), lambda b, j: (b * nt + j, 0))
    per_b = lambda s: pl.BlockSpec((None,) + s, lambda b, j: (b,) + (0,) * len(s))
    return pl.pallas_call(
        functools.partial(_mixer_bc_kernel, tt=tt, last_row=last_row),
        grid=(batch, nt),
        in_specs=[rows(2 * D_B + 2 * D_C), per_b((SUBLANES, D_C)), per_b((1, D_C)),
                  _full((N_HEADS_B, CHUNK, CHUNK)), _full((CHUNK, D_B)), _full((CONV_W, D_C)),
                  _full((1, D_C)), _full((D_C, D_C)), _full((1, D_C)), _full((D_C, D_C)),
                  _full((1, D_C)), _full((1, D_C))],
        out_specs=[rows(D_B), rows(D_C), per_b((1, D_C))],
        out_shape=[jax.ShapeDtypeStruct((batch * t_len, D_B), F32),
                   jax.ShapeDtypeStruct((batch * t_len, D_C), F32),
                   jax.ShapeDtypeStruct((batch, 1, D_C), F32)],
        scratch_shapes=[pltpu.VMEM((tt + SUBLANES, D_C), F32), pltpu.VMEM((1, D_C), F32)],
        compiler_params=_cparams(("parallel", "arbitrary")),
        name="mixer_bc",
    )(bc, conv_buf8, h0, lw["w_s"], lw["bias_s"], lw["conv_w"], lw["conv_b"], lw["w_a_bd"],
      lw["b_a"], lw["w_x_bd"], lw["b_x"], lw["lru_lambda"])


def _out_proj_kernel(oa_ref, ob_ref, oc_ref, x_ref, go_ref, wo_ref, g2_ref, wr_ref, br_ref,
                     xn_ref, h2_ref, idx_ref, gate_ref, rank_ref, cnt_ref):
    oa = _rms(oa_ref[...], go_ref[:, 0:D_A]).astype(BF16)
    ob = _rms(ob_ref[...], go_ref[:, D_A:D_A + D_B]).astype(BF16)
    oc = _rms(oc_ref[...], go_ref[:, D_A + D_B:]).astype(BF16)
    y = (jnp.dot(oa, wo_ref[0:D_A, :], preferred_element_type=F32)
         + jnp.dot(ob, wo_ref[D_A:D_A + D_B, :], preferred_element_type=F32)
         + jnp.dot(oc, wo_ref[D_A + D_B:, :], preferred_element_type=F32))
    xn = x_ref[...] + y
    xn_ref[...] = xn
    h2 = _rms(xn, g2_ref[...])
    h2_ref[...] = h2
    logits = jnp.dot(h2.astype(BF16), wr_ref[...], preferred_element_type=F32) + br_ref[...]
    lane = lax.broadcasted_iota(jnp.int32, logits.shape, 1).astype(F32)
    cur = logits
    tops, idxs = [], []
    for _ in range(TOP_K):
        m = jnp.max(cur, axis=-1, keepdims=True)
        ix = jnp.min(jnp.where(cur == m, lane, float(ROUTER_PAD)), axis=-1, keepdims=True)
        tops.append(m)
        idxs.append(ix)
        cur = jnp.where(lane == ix, -jnp.inf, cur)
    es = [jnp.exp(t - tops[0]) for t in tops]
    den = sum(es)
    idx_out = jnp.zeros(logits.shape, F32)
    gate_out = jnp.zeros(logits.shape, F32)
    for kk in range(TOP_K):
        idx_out = jnp.where(lane == kk, idxs[kk], idx_out)
        gate_out = jnp.where(lane == kk, es[kk] / den, gate_out)
    idx_ref[...] = idx_out.astype(jnp.int32)
    gate_ref[...] = gate_out
    tm = logits.shape[0]
    chosen = jnp.zeros(logits.shape, F32)
    for kk in range(TOP_K):
        chosen = jnp.where(lane == idxs[kk], 1.0, chosen)
    below = (lax.broadcasted_iota(jnp.int32, (tm, tm), 0)
             > lax.broadcasted_iota(jnp.int32, (tm, tm), 1)).astype(BF16)
    earlier = jnp.dot(below, chosen.astype(BF16), preferred_element_type=F32)
    rank_out = jnp.zeros(logits.shape, F32)
    for kk in range(TOP_K):
        rk = jnp.sum(jnp.where(lane == idxs[kk], earlier, 0.0), axis=-1, keepdims=True)
        rank_out = jnp.where(lane == kk, rk, rank_out)
    rank_ref[...] = rank_out.astype(jnp.int32)
    cnt_ref[...] = jnp.broadcast_to(jnp.sum(chosen, axis=0, keepdims=True),
                                    cnt_ref.shape).astype(jnp.int32)


def _out_proj(oa, ob, oc, x, lw):
    n = x.shape[0]
    tm = TOKEN_TILE
    row = lambda w: pl.BlockSpec((tm, w), lambda i: (i, 0))
    return pl.pallas_call(
        _out_proj_kernel,
        grid=(n // tm,),
        in_specs=[row(D_A), row(D_B), row(D_C), row(D_MODEL), _full((1, D_MODEL)),
                  _full((D_MODEL, D_MODEL)), _full((1, D_MODEL)), _full((D_MODEL, ROUTER_PAD)),
                  _full((1, ROUTER_PAD))],
        out_specs=[row(D_MODEL), row(D_MODEL), row(ROUTER_PAD), row(ROUTER_PAD), row(ROUTER_PAD),
                   pl.BlockSpec((None, SUBLANES, ROUTER_PAD), lambda i: (i, 0, 0))],
        out_shape=[jax.ShapeDtypeStruct((n, D_MODEL), F32), jax.ShapeDtypeStruct((n, D_MODEL), F32),
                   jax.ShapeDtypeStruct((n, ROUTER_PAD), jnp.int32),
                   jax.ShapeDtypeStruct((n, ROUTER_PAD), F32),
                   jax.ShapeDtypeStruct((n, ROUTER_PAD), jnp.int32),
                   jax.ShapeDtypeStruct((n // tm, SUBLANES, ROUTER_PAD), jnp.int32)],
        compiler_params=_cparams(("parallel",)),
        name="out_proj",
    )(oa, ob, oc, x, lw["g_out"], lw["w_out"], lw["ln2_g"], lw["w_router"], lw["b_router"])


def _moe_kernel(te_ref, nu_ref, x_ref, wgu_ref, bg_ref, bl_ref, wd_ref, bd_ref, y_ref, wg_s, wl_s, wd_s):
    i = pl.program_id(0)
    used = i < nu_ref[0]
    new_expert = jnp.logical_or(i == 0, te_ref[i] != te_ref[jnp.maximum(i - 1, 0)])

    @pl.when(jnp.logical_and(used, new_expert))
    def _():
        rc = CHUNK
        lane = lax.broadcasted_iota(jnp.int32, (rc, LANES), 1)
        low = lane < LANES // 2
        even_first = jnp.where(low, 2 * lane, 2 * lane - LANES + 1)
        odd_first = jnp.where(low, 2 * lane + 1, 2 * lane - LANES)

        def body(r, carry):
            rows = pl.ds(pl.multiple_of(r * rc, rc), rc)
            for c in range(D_FF // LANES):
                a = wgu_ref[rows, 2 * c * LANES:(2 * c + 1) * LANES]
                b = wgu_ref[rows, (2 * c + 1) * LANES:(2 * c + 2) * LANES]
                take = lambda t, ix: jnp.take_along_axis(t, ix, axis=1)
                wg_s[rows, c * LANES:(c + 1) * LANES] = jnp.where(
                    low, take(a, even_first), take(b, odd_first)).astype(BF16)
                wl_s[rows, c * LANES:(c + 1) * LANES] = jnp.where(
                    low, take(a, odd_first), take(b, even_first)).astype(BF16)
            wd_s[rows, :] = wd_ref[rows, :].astype(BF16)
            return carry

        lax.fori_loop(0, D_MODEL // rc, body, 0)

    @pl.when(used)
    def _():
        x = x_ref[...].astype(BF16)
        zg = jnp.dot(x, wg_s[...], preferred_element_type=F32) + bg_ref[...]
        zl = jnp.dot(x, wl_s[...], preferred_element_type=F32) + bl_ref[...]
        glu = jnp.minimum(zg, SWIGLU_LIMIT)
        lin = jnp.clip(zl, -SWIGLU_LIMIT, SWIGLU_LIMIT)
        act = glu * jax.nn.sigmoid(SWIGLU_ALPHA * glu) * (lin + 1.0)
        y_ref[...] = jnp.dot(act.astype(BF16), wd_s[...], preferred_element_type=F32) + bd_ref[...]

    @pl.when(i >= nu_ref[0])
    def _():
        y_ref[...] = jnp.zeros(y_ref.shape, F32)


def _moe_ffn(xs, tile_expert, n_used, lw):
    n_slots = xs.shape[0]
    tm = MOE_TILE
    layer = lw["layer"]
    wspec = lambda r, c: pl.BlockSpec((None, r, c), lambda i, te, nu: (te[i], 0, 0))
    wfull = lambda r, c: pl.BlockSpec((None, None, r, c), lambda i, te, nu: (layer, te[i], 0, 0))
    return pl.pallas_call(
        _moe_kernel,
        grid_spec=pltpu.PrefetchScalarGridSpec(
            num_scalar_prefetch=2,
            grid=(n_slots // tm,),
            in_specs=[pl.BlockSpec((tm, D_MODEL), lambda i, te, nu: (i, 0)),
                      wfull(D_MODEL, 2 * D_FF), wspec(1, D_FF), wspec(1, D_FF),
                      wfull(D_FF, D_MODEL), wspec(1, D_MODEL)],
            out_specs=pl.BlockSpec((tm, D_MODEL), lambda i, te, nu: (i, 0)),
            scratch_shapes=[pltpu.VMEM((D_MODEL, D_FF), BF16), pltpu.VMEM((D_MODEL, D_FF), BF16),
                            pltpu.VMEM((D_FF, D_MODEL), BF16)],
        ),
        out_shape=jax.ShapeDtypeStruct((n_slots, D_MODEL), F32),
        compiler_params=_cparams(("arbitrary",)),
        name="moe_ffn",
    )(tile_expert, n_used, xs, lw["w_gu"], lw["b_glu"], lw["b_lin"], lw["w_down"], lw["b_down"])


def _route(idx, rank, counts):
    n_tok = idx.shape[0]
    experts = jnp.arange(N_EXPERTS, dtype=jnp.int32)
    totals = jnp.sum(counts, axis=0)
    tiles = (totals + MOE_TILE - 1) // MOE_TILE
    tile_end = jnp.sum(jnp.where(experts[None, :] <= experts[:, None], tiles[None, :], 0), axis=1)
    pad_start = (tile_end - tiles) * MOE_TILE
    tt = jnp.arange(counts.shape[0], dtype=jnp.int32)
    before = jnp.sum(jnp.where((tt[None, :] < tt[:, None])[:, :, None], counts[None, :, :], 0), axis=1)
    base = pad_start[None, :] + before
    base_tok = jnp.repeat(base, TOKEN_TILE, axis=0)
    picked = idx[:, :, None] == experts[None, None, :]
    dest = jnp.sum(jnp.where(picked, base_tok[:, None, :], 0), axis=-1) + rank
    n_tiles = -(-(n_tok * TOP_K) // MOE_TILE) + N_EXPERTS
    tile_ids = jnp.arange(n_tiles, dtype=jnp.int32)
    tile_expert = jnp.minimum(jnp.sum((tile_end[None, :] <= tile_ids[:, None]).astype(jnp.int32), axis=1),
                              N_EXPERTS - 1)
    n_used = tile_end[-1:].astype(jnp.int32)
    return dest.astype(jnp.int32), tile_expert, n_used, n_tiles


def _sc_mesh():
    return plsc.VectorSubcoreMesh(core_axis_name="core", subcore_axis_name="subcore")


def _sc_dispatch(h2, dest_g, n_slots):
    ng = dest_g.shape[0]

    @pl.kernel(out_type=jax.ShapeDtypeStruct((n_slots, D_MODEL), F32), mesh=_sc_mesh(),
               scratch_types=[pltpu.VMEM((1, LANES), jnp.int32), pltpu.VMEM((SC_GROUP, D_MODEL), F32)])
    def kernel(h_hbm, d_hbm, o_hbm, i_vmem, buf):
        wid = lax.axis_index("core") * SC_SUBCORES + lax.axis_index("subcore")

        @pl.loop(0, pl.cdiv(ng, SC_WORKERS))
        def _(r):
            g = wid + SC_WORKERS * r

            @pl.when(g < ng)
            def _():
                pltpu.sync_copy(d_hbm.at[pl.ds(g, 1)], i_vmem)
                pltpu.sync_copy(h_hbm.at[pl.ds(g * SC_GROUP, SC_GROUP)], buf)
                for k in range(TOP_K):
                    pltpu.sync_copy(buf, o_hbm.at[i_vmem.at[0, pl.ds(k * SC_GROUP, SC_GROUP)]])

    return kernel(h2, dest_g)


def _sc_collect(y, dest_g):
    ng = dest_g.shape[0]

    @pl.kernel(out_type=jax.ShapeDtypeStruct((ng * LANES, D_MODEL), F32), mesh=_sc_mesh(),
               scratch_types=[pltpu.VMEM((1, LANES), jnp.int32), pltpu.VMEM((SC_GROUP, D_MODEL), F32)])
    def kernel(y_hbm, d_hbm, o_hbm, i_vmem, buf):
        wid = lax.axis_index("core") * SC_SUBCORES + lax.axis_index("subcore")

        @pl.loop(0, pl.cdiv(ng, SC_WORKERS))
        def _(r):
            g = wid + SC_WORKERS * r

            @pl.when(g < ng)
            def _():
                pltpu.sync_copy(d_hbm.at[pl.ds(g, 1)], i_vmem)
                for k in range(TOP_K):
                    pltpu.sync_copy(y_hbm.at[i_vmem.at[0, pl.ds(k * SC_GROUP, SC_GROUP)]], buf)
                    pltpu.sync_copy(buf, o_hbm.at[pl.ds(g * LANES + k * SC_GROUP, SC_GROUP)])

    return kernel(y, dest_g)


def _combine_kernel(xn_ref, g_ref, y_ref, o_ref):
    for gi in range(TOKEN_TILE // SC_GROUP):
        rows = slice(gi * SC_GROUP, (gi + 1) * SC_GROUP)
        acc = xn_ref[rows, :]
        for k in range(TOP_K):
            r0 = gi * LANES + k * SC_GROUP
            acc = acc + g_ref[rows, k:k + 1] * y_ref[r0:r0 + SC_GROUP, :]
        o_ref[rows, :] = acc


def _combine(xn, gates, y4):
    n = xn.shape[0]
    tm = TOKEN_TILE
    row = lambda r, w: pl.BlockSpec((r, w), lambda i: (i, 0))
    return pl.pallas_call(
        _combine_kernel,
        grid=(n // tm,),
        in_specs=[row(tm, D_MODEL), row(tm, ROUTER_PAD), row(tm * TOP_K, D_MODEL)],
        out_specs=row(tm, D_MODEL),
        out_shape=jax.ShapeDtypeStruct((n, D_MODEL), F32),
        compiler_params=_cparams(("parallel",)),
        name="moe_combine",
    )(xn, gates, y4)


def _moe(xn, h2, idx, gates, rank, counts, lw):
    n_tok = xn.shape[0]
    dest, tile_expert, n_used, n_tiles = _route(idx[:, :TOP_K], rank[:, :TOP_K], counts)
    dest_g = dest.reshape(n_tok // SC_GROUP, SC_GROUP, TOP_K).transpose(0, 2, 1).reshape(-1, LANES)
    xs = _sc_dispatch(h2, dest_g, n_tiles * MOE_TILE)
    y = _moe_ffn(xs, tile_expert, n_used, lw)
    return _combine(xn, gates, _sc_collect(y, dest_g))


def _block_diag(w):
    g, a, b = w.shape
    out = jnp.zeros((g * a, g * b), w.dtype)
    for i in range(g):
        out = out.at[i * a:(i + 1) * a, i * b:(i + 1) * b].set(w[i])
    return out


def _layer_weights(l, p):
    wr = jnp.pad(p["w_router"][l], ((0, 0), (0, ROUTER_PAD - N_EXPERTS)))
    hd = jnp.arange(D_A) // HEAD_DIM
    return {
        "ln1_g": p["ln1_g"][l][None], "w_in": p["w_in"][l].astype(BF16),
        "g_q": jnp.tile(p["g_q"][l], N_HEADS_A)[None], "g_k": jnp.tile(p["g_k"][l], N_HEADS_A)[None],
        "g_vb": p["g_vb"][l][None],
        "ones_bd": (hd[:, None] == hd[None, :]).astype(BF16),
        "w_s": p["w_s"][l],
        "bias_s": jnp.repeat(p["b_s"][l].T, HEAD_DIM, axis=1),
        "conv_w": p["conv_w"][l], "conv_b": p["conv_b"][l][None],
        "w_a_bd": _block_diag(p["w_a"][l]).astype(BF16), "b_a": p["b_a"][l][None],
        "w_x_bd": _block_diag(p["w_x"][l]).astype(BF16), "b_x": p["b_x"][l][None],
        "lru_lambda": p["lru_lambda"][l][None],
        "g_out": p["g_out"][l][None], "w_out": p["w_out"][l].astype(BF16),
        "ln2_g": p["ln2_g"][l][None],
        "w_router": wr.astype(BF16),
        "b_router": jnp.pad(p["b_router"][l], (0, ROUTER_PAD - N_EXPERTS),
                            constant_values=NEG_BIG)[None],
        "layer": l, "w_gu": p["w_gu"], "w_down": p["w_down"],
        "b_glu": p["b_gu"][l][:, None, 0::2], "b_lin": p["b_gu"][l][:, None, 1::2],
        "b_down": p["b_down"][l][:, None, :],
    }


def kernel(x_prompt, x_sample, cache_win_k, cache_win_v, state_conv, state_lru, ln1_g, w_in, g_q, g_k,
           g_vb, w_s, b_s, conv_w, conv_b, w_a, b_a, w_x, b_x, lru_lambda, g_out, w_out, ln2_g, w_router,
           b_router, w_gu, b_gu, w_down, b_down):
    params = dict(ln1_g=ln1_g, w_in=w_in, g_q=g_q, g_k=g_k, g_vb=g_vb, w_s=w_s, b_s=b_s, conv_w=conv_w,
                  conv_b=conv_b, w_a=w_a, b_a=b_a, w_x=w_x, b_x=b_x, lru_lambda=lru_lambda, g_out=g_out,
                  w_out=w_out, ln2_g=ln2_g, w_router=w_router, b_router=b_router, w_gu=w_gu, b_gu=b_gu,
                  w_down=w_down, b_down=b_down)
    bp, sp, _ = x_prompt.shape
    bs, ss, _ = x_sample.shape
    depth = w_in.shape[0]
    n_p, n_s = bp * sp, bs * ss
    keep = min(DILATED_PATTERNS[-1][0], sp)
    w_buf = cache_win_k.shape[2]
    ckt = cache_win_k.transpose(0, 1, 3, 4, 2).reshape(depth, bs, D_A, w_buf)
    cvt = cache_win_v.transpose(0, 1, 3, 4, 2).reshape(depth, bs, D_A, w_buf)
    x = jnp.concatenate([x_prompt.reshape(n_p, D_MODEL), x_sample.reshape(n_s, D_MODEL)], axis=0)
    zero_conv = jnp.zeros((bp, SUBLANES, D_C), F32)
    zero_h = jnp.zeros((bp, 1, D_C), F32)
    outs = {name: [] for name in ("pconv", "plru", "sk", "sv", "sconv", "slru", "svb")}
    window = None
    for l in range(depth):
        lw = _layer_weights(l, params)
        q, k, v, bc = _in_proj(x, lw["ln1_g"], lw["w_in"], lw["g_q"], lw["g_k"], lw["g_vb"], lw["ones_bd"])
        oa_p = _attn_prompt(q, k, v, bp, sp)
        ob_p, oc_p, h_p = _mixer_bc(bc, zero_conv, zero_h, lw, bp, sp, MIX_TILE, MIX_TILE - 1)
        oa_s = _attn_sample(q, k, v, ckt, cvt, l, n_p, bs, ss)
        bc_s = bc[n_p:].reshape(bs, ss, -1)
        bc_s_pad = jnp.pad(bc_s, ((0, 0), (0, CHUNK - ss), (0, 0))).reshape(bs * CHUNK, -1)
        conv8 = jnp.pad(state_conv[l], ((0, 0), (SUBLANES - (CONV_W - 1), 0), (0, 0)))
        ob_s, oc_s, h_s = _mixer_bc(bc_s_pad, conv8, state_lru[l][:, None, :], lw, bs, CHUNK, CHUNK, ss - 1)
        ob_s = ob_s.reshape(bs, CHUNK, D_B)[:, :ss].reshape(n_s, D_B)
        oc_s = oc_s.reshape(bs, CHUNK, D_C)[:, :ss].reshape(n_s, D_C)
        oa = jnp.concatenate([oa_p, oa_s], axis=0)
        ob = jnp.concatenate([ob_p, ob_s], axis=0)
        oc = jnp.concatenate([oc_p, oc_s], axis=0)
        xn, h2, idx, gates, rank, counts = _out_proj(oa, ob, oc, x, lw)
        x = _moe(xn, h2, idx, gates, rank, counts[:, 0, :N_EXPERTS], lw)

        window = _export_window(k, v, window, l, depth, bp, sp, keep)
        xc_p = bc[:n_p, 2 * D_B:2 * D_B + D_C].reshape(bp, sp, D_C)
        outs["pconv"].append(xc_p[:, sp - (CONV_W - 1):])
        outs["plru"].append(h_p[:, 0])
        outs["sk"].append(k[n_p:].reshape(bs, ss, N_HEADS_A, HEAD_DIM))
        outs["sv"].append(v[n_p:].reshape(bs, ss, N_HEADS_A, HEAD_DIM))
        xpad_s = jnp.concatenate([state_conv[l], bc_s[:, :, 2 * D_B:2 * D_B + D_C]], axis=1)
        outs["sconv"].append(xpad_s[:, -(CONV_W - 1):])
        outs["slru"].append(h_s[:, 0])
        outs["svb"].append(bc_s[:, :, D_B:2 * D_B])
    y_p = x[:n_p].reshape(bp, sp, D_MODEL)
    y_s = x[n_p:].reshape(bs, ss, D_MODEL)
    st = lambda name: jnp.stack(outs[name])
    heads_last = lambda t: t.reshape(depth, bp, N_HEADS_A, HEAD_DIM, keep).transpose(0, 1, 4, 2, 3)
    return (y_p, y_s, heads_last(window[0]), heads_last(window[1]), st("pconv"), st("plru"), st("sk"), st("sv"), st("sconv"),
            st("slru"), st("svb"))
```

```python
import functools

import jax
import jax.numpy as jnp
from jax import lax
from jax.experimental import pallas as pl
from jax.experimental.pallas import tpu as pltpu
from jax.experimental.pallas import tpu_sc as plsc

F32 = jnp.float32
BF16 = jnp.bfloat16

D_MODEL = 1024
HEAD_DIM = 64
N_HEADS_A = 8
D_A = N_HEADS_A * HEAD_DIM
N_HEADS_B = 4
D_B = N_HEADS_B * HEAD_DIM
N_GROUPS_C = 4
D_C = N_GROUPS_C * HEAD_DIM
D_IN = 3 * D_A + 2 * D_B + 2 * D_C
DILATED_PATTERNS = ((128, 1), (512, 4), (2048, 16))
N_PATTERNS = len(DILATED_PATTERNS)
CHUNK = 128
CONV_W = 4
LRU_C = 8.0
N_EXPERTS = 32
TOP_K = 4
D_FF = 1024
SWIGLU_LIMIT = 7.0
SWIGLU_ALPHA = 1.702
EPS = 1e-6
ATTN_SCALE = HEAD_DIM ** -0.5
PAST_LEN = 16384

LANES = 128
SUBLANES = 8
VMEM_LIMIT_BYTES = 56 * 1024 * 1024

Q_BLOCK = 128
ATTN_SPAN = 2048
ATTN_GROUP = 4
RAW_KEY_STRIDE = 16
TOKEN_TILE = 256
MIX_TILE = 512
MOE_TILE = 256
ROUTER_PAD = LANES
SC_SUBCORES = 16
SC_WORKERS = 2 * SC_SUBCORES
SC_GROUP = LANES // TOP_K
NEG_BIG = -1e30


def _cparams(semantics):
    return pltpu.CompilerParams(dimension_semantics=semantics,
                                vmem_limit_bytes=VMEM_LIMIT_BYTES)


def _full(shape):
    return pl.BlockSpec(shape, lambda *_: (0,) * len(shape))


def _rms(t, g):
    ms = jnp.mean(t * t, axis=-1, keepdims=True)
    return t * lax.rsqrt(ms + EPS) * g


def _split_bf16(t):
    hi = t.astype(BF16)
    lo = (t - hi.astype(F32)).astype(BF16)
    return hi, lo


def _in_proj_kernel(x_ref, g1_ref, w_ref, gq_ref, gk_ref, gvb_ref, ones_ref,
                    q_ref, k_ref, v_ref, bc_ref):
    h = _rms(x_ref[...], g1_ref[...]).astype(BF16)
    z = jnp.dot(h, w_ref[...], preferred_element_type=F32)

    def head_norm(t, g):
        hi, lo = _split_bf16(t * t)
        ss = (jnp.dot(hi, ones_ref[...], preferred_element_type=F32)
              + jnp.dot(lo, ones_ref[...], preferred_element_type=F32))
        return t * lax.rsqrt(ss * (1.0 / HEAD_DIM) + EPS) * g

    q_ref[...] = head_norm(z[:, 0:D_A], gq_ref[...]) * ATTN_SCALE
    k_ref[...] = head_norm(z[:, D_A:2 * D_A], gk_ref[...])
    v_ref[...] = z[:, 2 * D_A:3 * D_A]
    o = 3 * D_A
    bc_ref[...] = z[:, o:]
    bc_ref[:, D_B:2 * D_B] = _rms(z[:, o + D_B:o + 2 * D_B], gvb_ref[...])


def _in_proj(x, g1, w_bf16, gq, gk, gvb, ones_bd):
    n = x.shape[0]
    tm = TOKEN_TILE
    row = lambda w: pl.BlockSpec((tm, w), lambda i: (i, 0))
    return pl.pallas_call(
        _in_proj_kernel,
        grid=(n // tm,),
        in_specs=[row(D_MODEL), _full((1, D_MODEL)), _full((D_MODEL, D_IN)), _full((1, D_A)),
                  _full((1, D_A)), _full((1, D_B)), _full((D_A, D_A))],
        out_specs=[row(D_A), row(D_A), row(D_A), row(2 * D_B + 2 * D_C)],
        out_shape=[jax.ShapeDtypeStruct((n, D_A), F32)] * 3
        + [jax.ShapeDtypeStruct((n, 2 * D_B + 2 * D_C), F32)],
        compiler_params=_cparams(("parallel",)),
        name="in_proj",
    )(x, g1, w_bf16, gq, gk, gvb, ones_bd)


def _attn_prompt_kernel(q_ref, kp_ref, kc_ref, vp_ref, vc_ref, o_ref, kq0a, kq0b, kq1a, kq1b, vqh, vql,
                        m_s, l_s, a_s):
    span = pl.program_id(2)
    qb2 = 2 * Q_BLOCK
    lane = lax.broadcasted_iota(jnp.int32, (Q_BLOCK, LANES), 1)
    head0 = lane < HEAD_DIM
    lane2 = lax.broadcasted_iota(jnp.int32, (qb2, LANES), 1)
    head0_2 = lane2 < HEAD_DIM
    swap = lambda t: pltpu.roll(t, HEAD_DIM, axis=1)

    def pack(k_ref, v_ref, base):
        def body(c, carry):
            src = pl.ds(pl.multiple_of(c * qb2, qb2), qb2)
            dst = pl.ds(pl.multiple_of(base + c * qb2, qb2), qb2)
            k = k_ref[src, :]
            kh = k.astype(BF16).astype(F32)
            kl = k - kh
            kq0a[dst, :] = jnp.where(head0_2, kh, swap(kh))
            kq0b[dst, :] = jnp.where(head0_2, kl, 0.0)
            kq1a[dst, :] = jnp.where(head0_2, swap(kh), kh)
            kq1b[dst, :] = jnp.where(head0_2, swap(kl), 0.0)
            v = v_ref[src, :]
            vh = v.astype(BF16).astype(F32)
            vqh[dst, :] = vh
            vql[dst, :] = v - vh
            return carry
        lax.fori_loop(0, ATTN_SPAN // qb2, body, 0)

    pack(kp_ref, vp_ref, 0)
    pack(kc_ref, vc_ref, ATTN_SPAN)

    row = lax.broadcasted_iota(jnp.int32, (qb2, qb2), 0) & (Q_BLOCK - 1)
    col = lax.broadcasted_iota(jnp.int32, (qb2, qb2), 1)
    band = (col >= row) & (col <= row + Q_BLOCK)
    cur = col >= Q_BLOCK
    nt = (((1,), (1,)), ((), ()))

    def ds(start, size, d):
        return pl.ds(start, size) if d == 1 else pl.ds(start, size, stride=d)

    for p, (_, d) in enumerate(DILATED_PATTERNS):
        nblk = ATTN_SPAN // (Q_BLOCK * d)
        assert d < RAW_KEY_STRIDE or nblk == 1

        def scores(blk, d=d, nblk=nblk):
            r = blk // nblk
            ib = blk % nblk
            qstart = r + d * Q_BLOCK * ib
            if d == 1:
                qstart = pl.multiple_of(Q_BLOCK * blk, Q_BLOCK)
            kstart = ATTN_SPAN + qstart - d * Q_BLOCK
            q = q_ref[ds(qstart, Q_BLOCK, d), :]
            qh = q.astype(BF16).astype(F32)
            ql = q - qh
            lhs0 = jnp.concatenate([jnp.where(head0, qh, swap(ql)), jnp.where(head0, qh, 0.0)], axis=1)
            lhs1 = jnp.concatenate([jnp.where(head0, swap(qh), ql), jnp.where(head0, swap(qh), 0.0)], axis=1)
            if d < RAW_KEY_STRIDE:
                keys = ds(kstart, qb2, d)
                k0 = jnp.concatenate([kq0a[keys, :], kq0b[keys, :]], axis=1).astype(BF16)
                k1 = jnp.concatenate([kq1a[keys, :], kq1b[keys, :]], axis=1).astype(BF16)
                vh, vl = vqh[keys, :], vql[keys, :]
            else:
                both = lambda p_ref, c_ref: jnp.concatenate(
                    [p_ref[ds(r, Q_BLOCK, d), :], c_ref[ds(r, Q_BLOCK, d), :]], axis=0)
                k = both(kp_ref, kc_ref)
                kh = k.astype(BF16).astype(F32)
                kl = k - kh
                k0 = jnp.concatenate([jnp.where(head0_2, kh, swap(kh)), jnp.where(head0_2, kl, 0.0)],
                                     axis=1).astype(BF16)
                k1 = jnp.concatenate([jnp.where(head0_2, swap(kh), kh), jnp.where(head0_2, swap(kl), 0.0)],
                                     axis=1).astype(BF16)
                v = both(vp_ref, vc_ref)
                vh = v.astype(BF16).astype(F32)
                vl = v - vh
            s0 = lax.dot_general(lhs0.astype(BF16), k0, nt, preferred_element_type=F32)
            s1 = lax.dot_general(lhs1.astype(BF16), k1, nt, preferred_element_type=F32)
            prev_ok = jnp.logical_or(ib > 0, span > 0)
            s = jnp.concatenate([s0, s1], axis=0)
            return jnp.where(band & (cur | prev_ok), s, -jnp.inf), qstart, (vh, vl)

        def softmax(s):
            m = jnp.max(s, axis=-1, keepdims=True)
            e = jnp.exp(s - m)
            return m, e, jnp.sum(e, axis=-1, keepdims=True)

        def weighted(e, v_parts):
            eh, el = _split_bf16(e)
            vh, vl = v_parts
            rhs = jnp.concatenate([jnp.concatenate([vh, vl], axis=1),
                                   jnp.concatenate([vh, jnp.zeros_like(vh)], axis=1)], axis=0).astype(BF16)
            out = jnp.dot(jnp.concatenate([eh, el], axis=1), rhs, preferred_element_type=F32)
            return out[:, :LANES] + out[:, LANES:]

        def store(qstart, m, l, acc, p=p, d=d):
            dst = ds(qstart, Q_BLOCK, d)
            shape = (Q_BLOCK, LANES)
            m_s[p, dst, :] = jnp.where(head0, jnp.broadcast_to(m[:Q_BLOCK], shape),
                                       jnp.broadcast_to(m[Q_BLOCK:], shape))
            l_s[p, dst, :] = jnp.where(head0, jnp.broadcast_to(l[:Q_BLOCK], shape),
                                       jnp.broadcast_to(l[Q_BLOCK:], shape))
            a_s[p, dst, :] = jnp.where(head0, acc[:Q_BLOCK], acc[Q_BLOCK:])

        def body(it, carry):
            sc = [scores(it * ATTN_GROUP + g) for g in range(ATTN_GROUP)]
            sm = [softmax(s) for s, _, _ in sc]
            ac = [weighted(e, v_parts) for (_, e, _), (_, _, v_parts) in zip(sm, sc)]
            for (_, qstart, _), (m, _, l), acc in zip(sc, sm, ac):
                store(qstart, m, l, acc)
            return carry

        lax.fori_loop(0, ATTN_SPAN // (Q_BLOCK * ATTN_GROUP), body, 0)

    def merge(c, carry):
        rows = pl.ds(pl.multiple_of(c * Q_BLOCK, Q_BLOCK), Q_BLOCK)
        ms = [m_s[p, rows, :] for p in range(N_PATTERNS)]
        m_all = functools.reduce(jnp.maximum, ms)
        ws = [jnp.exp(m - m_all) for m in ms]
        num = sum(w * a_s[p, rows, :] for p, w in enumerate(ws))
        den = sum(w * l_s[p, rows, :] for p, w in enumerate(ws))
        o_ref[rows, :] = num / den
        return carry

    lax.fori_loop(0, ATTN_SPAN // Q_BLOCK, merge, 0)


def _attn_prompt(q, k, v, batch, seq):
    nspan = seq // ATTN_SPAN
    blk = (ATTN_SPAN, LANES)
    cur = pl.BlockSpec(blk, lambda b, hp, s: (b * nspan + s, hp))
    prev = pl.BlockSpec(blk, lambda b, hp, s: (b * nspan + jnp.maximum(s - 1, 0), hp))
    acc = pltpu.VMEM((N_PATTERNS, ATTN_SPAN, LANES), F32)
    packed = pltpu.VMEM((2 * ATTN_SPAN, LANES), F32)
    return pl.pallas_call(
        _attn_prompt_kernel,
        grid=(batch, D_A // LANES, nspan),
        in_specs=[cur, prev, cur, prev, cur],
        out_specs=cur,
        out_shape=jax.ShapeDtypeStruct((batch * seq, D_A), F32),
        scratch_shapes=[packed] * 6 + [acc, acc, acc],
        compiler_params=_cparams(("parallel", "parallel", "arbitrary")),
        name="attn_prompt",
    )(q, k, k, v, v)


def _attn_sample_kernel(q_ref, kn_ref, vn_ref, ckt_ref, cvt_ref, o_ref, *, w_buf, t_new):
    pad = LANES - t_new
    zeros = jnp.zeros((pad, D_A), F32)
    kn = jnp.concatenate([kn_ref[...], zeros], axis=0).astype(BF16)
    vn = jnp.concatenate([vn_ref[...], zeros], axis=0).astype(BF16)
    n_rows = N_HEADS_A * t_new
    hrow = lax.broadcasted_iota(jnp.int32, (n_rows, D_A), 0) // t_new
    hlane = lax.broadcasted_iota(jnp.int32, (n_rows, D_A), 1) // HEAD_DIM
    own = hrow == hlane
    q_rep = jnp.concatenate([q_ref[...]] * N_HEADS_A, axis=0)
    q64 = jnp.where(own, q_rep, 0.0).astype(BF16)
    nt = (((1,), (1,)), ((), ()))
    s_w = jnp.dot(q64, ckt_ref[...].astype(BF16), preferred_element_type=F32)
    s_n = lax.dot_general(q64, kn, nt, preferred_element_type=F32)
    vt = cvt_ref[...].astype(BF16)

    def dist(n_cols, first):
        t = lax.broadcasted_iota(jnp.int32, (n_rows, n_cols), 0) % t_new
        return t - lax.broadcasted_iota(jnp.int32, (n_rows, n_cols), 1) - first

    dist_w, dist_n = dist(w_buf, -w_buf), dist(LANES, 0)
    ms, ls, accs = [], [], []
    for w, d in DILATED_PATTERNS:
        ok = lambda ds: (ds >= 0) & (ds <= w) & ((ds & (d - 1)) == 0)
        sw = jnp.where(ok(dist_w), s_w, -jnp.inf)
        sn = jnp.where(ok(dist_n), s_n, -jnp.inf)
        m = jnp.maximum(jnp.max(sw, axis=-1, keepdims=True), jnp.max(sn, axis=-1, keepdims=True))
        ew = jnp.exp(sw - m)
        en = jnp.exp(sn - m)
        ms.append(m)
        ls.append(jnp.sum(ew, axis=-1, keepdims=True) + jnp.sum(en, axis=-1, keepdims=True))
        accs.append(lax.dot_general(ew.astype(BF16), vt, nt, preferred_element_type=F32)
                    + jnp.dot(en.astype(BF16), vn, preferred_element_type=F32))
    m_all = functools.reduce(jnp.maximum, ms)
    ws = [jnp.exp(m - m_all) for m in ms]
    num = sum(w * a for w, a in zip(ws, accs))
    den = sum(w * l for w, l in zip(ws, ls))
    o = jnp.where(own, num / den, 0.0)
    out = o[0:t_new]
    for h in range(1, N_HEADS_A):
        out = out + o[h * t_new:(h + 1) * t_new]
    o_ref[...] = out


def _attn_sample(q, k, v, cache_kt, cache_vt, layer, row0, batch, t_new):
    w_buf = cache_kt.shape[3]
    assert row0 % t_new == 0 and t_new == SUBLANES
    new = pl.BlockSpec((t_new, D_A), lambda b: (row0 // t_new + b, 0))
    cache = pl.BlockSpec((None, None, D_A, w_buf), lambda b: (layer, b, 0, 0))
    return pl.pallas_call(
        functools.partial(_attn_sample_kernel, w_buf=w_buf, t_new=t_new),
        grid=(batch,),
        in_specs=[new, new, new, cache, cache],
        out_specs=pl.BlockSpec((t_new, D_A), lambda b: (b, 0)),
        out_shape=jax.ShapeDtypeStruct((batch * t_new, D_A), F32),
        compiler_params=_cparams(("parallel",)),
        name="attn_sample",
    )(q, k, v, cache_kt, cache_vt)


def _export_kernel(*refs):
    k_ref, v_ref = refs[0], refs[1]
    pk_ref, pv_ref = refs[-2], refs[-1]
    pk_ref[...] = k_ref[...].T
    pv_ref[...] = v_ref[...].T


def _export_window(k, v, prev, layer, depth, batch, seq, keep):
    tm = TOKEN_TILE
    first = (seq - keep) // tm
    src = pl.BlockSpec((tm, D_A), lambda b, j: (b * (seq // tm) + first + j, 0))
    dst = pl.BlockSpec((None, None, D_A, tm), lambda b, j: (layer, b, 0, j))
    shape = jax.ShapeDtypeStruct((depth, batch, D_A, keep), F32)
    carried = [] if prev is None else list(prev)
    return pl.pallas_call(
        _export_kernel,
        grid=(batch, keep // tm),
        in_specs=[src, src] + [pl.BlockSpec(memory_space=pl.ANY)] * len(carried),
        out_specs=[dst, dst],
        out_shape=[shape, shape],
        input_output_aliases={2 + i: i for i in range(len(carried))},
        compiler_params=_cparams(("parallel", "parallel")),
        name="export_window",
    )(k, v, *carried)


def _gelu_tanh(x):
    return 0.5 * x * (1.0 + jnp.tanh(0.7978845608028654 * (x + 0.044715 * x * x * x)))


def _mixer_bc_kernel(bc_ref, cb_ref, h0_ref, ws_ref, bs_ref, cw_ref, cbias_ref, wa_ref, ba_ref,
                     wx_ref, bx_ref, lam_ref, ob_ref, oc_ref, hl_ref, xp_s, h_s, *, tt, last_row):
    j = pl.program_id(1)

    @pl.when(j == 0)
    def _():
        xp_s[0:SUBLANES, :] = cb_ref[...]
        h_s[...] = h0_ref[...]

    nch = tt // CHUNK
    vcat = jnp.concatenate([bc_ref[c * CHUNK:(c + 1) * CHUNK, D_B:2 * D_B] for c in range(nch)],
                           axis=1).astype(BF16)
    ri = lax.broadcasted_iota(jnp.int32, (CHUNK, CHUNK), 0)
    ci = lax.broadcasted_iota(jnp.int32, (CHUNK, CHUNK), 1)
    hl = (lax.broadcasted_iota(jnp.int32, (CHUNK, nch * D_B), 1) % D_B) // HEAD_DIM
    mixed = jnp.zeros((CHUNK, nch * D_B), F32)
    for h in range(N_HEADS_B):
        wh = jnp.where(ri >= ci, ws_ref[h], 0.0).astype(BF16)
        mh = jnp.dot(wh, vcat, preferred_element_type=F32)
        mixed = mixed + jnp.where(hl == h, mh, 0.0)
    for c in range(nch):
        rows = slice(c * CHUNK, (c + 1) * CHUNK)
        ob_ref[rows, :] = bc_ref[rows, 0:D_B] * (mixed[:, c * D_B:(c + 1) * D_B] + bs_ref[...])

    xc = bc_ref[:, 2 * D_B:2 * D_B + D_C]
    xp_s[SUBLANES:SUBLANES + tt, :] = xc
    xconv = cbias_ref[...] + cw_ref[CONV_W - 1:CONV_W, :] * xc
    for kk in range(CONV_W - 1):
        off = SUBLANES - (CONV_W - 1) + kk
        xconv = xconv + cw_ref[kk:kk + 1, :] * xp_s[off:off + tt, :]
    xp_s[0:SUBLANES, :] = xp_s[tt:tt + SUBLANES, :]
    xb = xconv.astype(BF16)
    r = jax.nn.sigmoid(jnp.dot(xb, wa_ref[...], preferred_element_type=F32) + ba_ref[...])
    i = jax.nn.sigmoid(jnp.dot(xb, wx_ref[...], preferred_element_type=F32) + bx_ref[...])
    nl = -lam_ref[...]
    softplus = jnp.maximum(nl, 0.0) + jnp.log1p(jnp.exp(-jnp.abs(nl)))
    a = jnp.exp(-LRU_C * r * softplus)
    b = jnp.sqrt(1.0 - a * a) * (i * xconv)
    rowi = lax.broadcasted_iota(jnp.int32, (tt, D_C), 0)
    step = 1
    while step < tt:
        a_sh = pltpu.roll(a, step, axis=0)
        b_sh = pltpu.roll(b, step, axis=0)
        live = rowi >= step
        b = jnp.where(live, a * b_sh + b, b)
        a = jnp.where(live, a * a_sh, a)
        step *= 2
    h = a * h_s[...] + b
    h_s[...] = h[tt - 1:tt, :]
    oc_ref[...] = h * _gelu_tanh(bc_ref[:, 2 * D_B + D_C:])

    @pl.when(j == pl.num_programs(1) - 1)
    def _():
        hl_ref[...] = h[last_row:last_row + 1, :]


def _mixer_bc(bc, conv_buf8, h0, lw, batch, t_len, tt, last_row):
    nt = t_len // tt
    rows = lambda w: pl.BlockSpec((tt, w), lambda b, j: (b * nt + j, 0))
    per_b = lambda s: pl.BlockSpec((None,) + s, lambda b, j: (b,) + (0,) * len(s))
    return pl.pallas_call(
        functools.partial(_mixer_bc_kernel, tt=tt, last_row=last_row),
        grid=(batch, nt),
        in_specs=[rows(2 * D_B + 2 * D_C), per_b((SUBLANES, D_C)), per_b((1, D_C)),
                  _full((N_HEADS_B, CHUNK, CHUNK)), _full((CHUNK, D_B)), _full((CONV_W, D_C)),
                  _full((1, D_C)), _full((D_C, D_C)), _full((1, D_C)), _full((D_C, D_C)),
                  _full((1, D_C)), _full((1, D_C))],
        out_specs=[rows(D_B), rows(D_C), per_b((1, D_C))],
        out_shape=[jax.ShapeDtypeStruct((batch * t_len, D_B), F32),
                   jax.ShapeDtypeStruct((batch * t_len, D_C), F32),
                   jax.ShapeDtypeStruct((batch, 1, D_C), F32)],
        scratch_shapes=[pltpu.VMEM((tt + SUBLANES, D_C), F32), pltpu.VMEM((1, D_C), F32)],
        compiler_params=_cparams(("parallel", "arbitrary")),
        name="mixer_bc",
    )(bc, conv_buf8, h0, lw["w_s"], lw["bias_s"], lw["conv_w"], lw["conv_b"], lw["w_a_bd"],
      lw["b_a"], lw["w_x_bd"], lw["b_x"], lw["lru_lambda"])


def _out_proj_kernel(oa_ref, ob_ref, oc_ref, x_ref, go_ref, wo_ref, g2_ref, wr_ref, br_ref,
                     xn_ref, h2_ref, idx_ref, gate_ref, rank_ref, cnt_ref):
    oa = _rms(oa_ref[...], go_ref[:, 0:D_A]).astype(BF16)
    ob = _rms(ob_ref[...], go_ref[:, D_A:D_A + D_B]).astype(BF16)
    oc = _rms(oc_ref[...], go_ref[:, D_A + D_B:]).astype(BF16)
    y = (jnp.dot(oa, wo_ref[0:D_A, :], preferred_element_type=F32)
         + jnp.dot(ob, wo_ref[D_A:D_A + D_B, :], preferred_element_type=F32)
         + jnp.dot(oc, wo_ref[D_A + D_B:, :], preferred_element_type=F32))
    xn = x_ref[...] + y
    xn_ref[...] = xn
    h2 = _rms(xn, g2_ref[...])
    h2_ref[...] = h2
    logits = jnp.dot(h2.astype(BF16), wr_ref[...], preferred_element_type=F32) + br_ref[...]
    lane = lax.broadcasted_iota(jnp.int32, logits.shape, 1).astype(F32)
    cur = logits
    tops, idxs = [], []
    for _ in range(TOP_K):
        m = jnp.max(cur, axis=-1, keepdims=True)
        ix = jnp.min(jnp.where(cur == m, lane, float(ROUTER_PAD)), axis=-1, keepdims=True)
        tops.append(m)
        idxs.append(ix)
        cur = jnp.where(lane == ix, -jnp.inf, cur)
    es = [jnp.exp(t - tops[0]) for t in tops]
    den = sum(es)
    idx_out = jnp.zeros(logits.shape, F32)
    gate_out = jnp.zeros(logits.shape, F32)
    for kk in range(TOP_K):
        idx_out = jnp.where(lane == kk, idxs[kk], idx_out)
        gate_out = jnp.where(lane == kk, es[kk] / den, gate_out)
    idx_ref[...] = idx_out.astype(jnp.int32)
    gate_ref[...] = gate_out
    tm = logits.shape[0]
    chosen = jnp.zeros(logits.shape, F32)
    for kk in range(TOP_K):
        chosen = jnp.where(lane == idxs[kk], 1.0, chosen)
    below = (lax.broadcasted_iota(jnp.int32, (tm, tm), 0)
             > lax.broadcasted_iota(jnp.int32, (tm, tm), 1)).astype(BF16)
    earlier = jnp.dot(below, chosen.astype(BF16), preferred_element_type=F32)
    rank_out = jnp.zeros(logits.shape, F32)
    for kk in range(TOP_K):
        rk = jnp.sum(jnp.where(lane == idxs[kk], earlier, 0.0), axis=-1, keepdims=True)
        rank_out = jnp.where(lane == kk, rk, rank_out)
    rank_ref[...] = rank_out.astype(jnp.int32)
    cnt_ref[...] = jnp.broadcast_to(jnp.sum(chosen, axis=0, keepdims=True),
                                    cnt_ref.shape).astype(jnp.int32)


def _out_proj(oa, ob, oc, x, lw):
    n = x.shape[0]
    tm = TOKEN_TILE
    row = lambda w: pl.BlockSpec((tm, w), lambda i: (i, 0))
    return pl.pallas_call(
        _out_proj_kernel,
        grid=(n // tm,),
        in_specs=[row(D_A), row(D_B), row(D_C), row(D_MODEL), _full((1, D_MODEL)),
                  _full((D_MODEL, D_MODEL)), _full((1, D_MODEL)), _full((D_MODEL, ROUTER_PAD)),
                  _full((1, ROUTER_PAD))],
        out_specs=[row(D_MODEL), row(D_MODEL), row(ROUTER_PAD), row(ROUTER_PAD), row(ROUTER_PAD),
                   pl.BlockSpec((None, SUBLANES, ROUTER_PAD), lambda i: (i, 0, 0))],
        out_shape=[jax.ShapeDtypeStruct((n, D_MODEL), F32), jax.ShapeDtypeStruct((n, D_MODEL), F32),
                   jax.ShapeDtypeStruct((n, ROUTER_PAD), jnp.int32),
                   jax.ShapeDtypeStruct((n, ROUTER_PAD), F32),
                   jax.ShapeDtypeStruct((n, ROUTER_PAD), jnp.int32),
                   jax.ShapeDtypeStruct((n // tm, SUBLANES, ROUTER_PAD), jnp.int32)],
        compiler_params=_cparams(("parallel",)),
        name="out_proj",
    )(oa, ob, oc, x, lw["g_out"], lw["w_out"], lw["ln2_g"], lw["w_router"], lw["b_router"])


def _moe_kernel(te_ref, nu_ref, x_ref, wgu_ref, bg_ref, bl_ref, wd_ref, bd_ref, y_ref, wg_s, wl_s, wd_s):
    i = pl.program_id(0)
    used = i < nu_ref[0]
    new_expert = jnp.logical_or(i == 0, te_ref[i] != te_ref[jnp.maximum(i - 1, 0)])

    @pl.when(jnp.logical_and(used, new_expert))
    def _():
        w2 = 2 * LANES
        src = lax.broadcasted_iota(jnp.int32, (w2, w2), 0)
        col = lax.broadcasted_iota(jnp.int32, (w2, w2), 1)
        pick = (src == jnp.where(col < LANES, 2 * col, 2 * (col - LANES) + 1)).astype(BF16)
        for c in range(2 * D_FF // w2):
            out = jnp.dot(wgu_ref[:, c * w2:(c + 1) * w2].astype(BF16), pick, preferred_element_type=F32)
            wg_s[:, c * LANES:(c + 1) * LANES] = out[:, :LANES].astype(BF16)
            wl_s[:, c * LANES:(c + 1) * LANES] = out[:, LANES:].astype(BF16)
        wd_s[...] = wd_ref[...].astype(BF16)

    @pl.when(used)
    def _():
        x = x_ref[...].astype(BF16)
        zg = jnp.dot(x, wg_s[...], preferred_element_type=F32) + bg_ref[...]
        zl = jnp.dot(x, wl_s[...], preferred_element_type=F32) + bl_ref[...]
        glu = jnp.minimum(zg, SWIGLU_LIMIT)
        lin = jnp.clip(zl, -SWIGLU_LIMIT, SWIGLU_LIMIT)
        act = glu * jax.nn.sigmoid(SWIGLU_ALPHA * glu) * (lin + 1.0)
        y_ref[...] = jnp.dot(act.astype(BF16), wd_s[...], preferred_element_type=F32) + bd_ref[...]

    @pl.when(i >= nu_ref[0])
    def _():
        y_ref[...] = jnp.zeros(y_ref.shape, F32)


def _moe_ffn(xs, tile_expert, n_used, lw):
    n_slots = xs.shape[0]
    tm = MOE_TILE
    layer = lw["layer"]
    wspec = lambda r, c: pl.BlockSpec((None, r, c), lambda i, te, nu: (te[i], 0, 0))
    wfull = lambda r, c: pl.BlockSpec((None, None, r, c), lambda i, te, nu: (layer, te[i], 0, 0))
    return pl.pallas_call(
        _moe_kernel,
        grid_spec=pltpu.PrefetchScalarGridSpec(
            num_scalar_prefetch=2,
            grid=(n_slots // tm,),
            in_specs=[pl.BlockSpec((tm, D_MODEL), lambda i, te, nu: (i, 0)),
                      wfull(D_MODEL, 2 * D_FF), wspec(1, D_FF), wspec(1, D_FF),
                      wfull(D_FF, D_MODEL), wspec(1, D_MODEL)],
            out_specs=pl.BlockSpec((tm, D_MODEL), lambda i, te, nu: (i, 0)),
            scratch_shapes=[pltpu.VMEM((D_MODEL, D_FF), BF16), pltpu.VMEM((D_MODEL, D_FF), BF16),
                            pltpu.VMEM((D_FF, D_MODEL), BF16)],
        ),
        out_shape=jax.ShapeDtypeStruct((n_slots, D_MODEL), F32),
        compiler_params=_cparams(("arbitrary",)),
        name="moe_ffn",
    )(tile_expert, n_used, xs, lw["w_gu"], lw["b_glu"], lw["b_lin"], lw["w_down"], lw["b_down"])


def _route(idx, rank, counts):
    n_tok = idx.shape[0]
    experts = jnp.arange(N_EXPERTS, dtype=jnp.int32)
    totals = jnp.sum(counts, axis=0)
    tiles = (totals + MOE_TILE - 1) // MOE_TILE
    tile_end = jnp.sum(jnp.where(experts[None, :] <= experts[:, None], tiles[None, :], 0), axis=1)
    pad_start = (tile_end - tiles) * MOE_TILE
    tt = jnp.arange(counts.shape[0], dtype=jnp.int32)
    before = jnp.sum(jnp.where((tt[None, :] < tt[:, None])[:, :, None], counts[None, :, :], 0), axis=1)
    base = pad_start[None, :] + before
    base_tok = jnp.repeat(base, TOKEN_TILE, axis=0)
    picked = idx[:, :, None] == experts[None, None, :]
    dest = jnp.sum(jnp.where(picked, base_tok[:, None, :], 0), axis=-1) + rank
    n_tiles = -(-(n_tok * TOP_K) // MOE_TILE) + N_EXPERTS
    tile_ids = jnp.arange(n_tiles, dtype=jnp.int32)
    tile_expert = jnp.minimum(jnp.sum((tile_end[None, :] <= tile_ids[:, None]).astype(jnp.int32), axis=1),
                              N_EXPERTS - 1)
    n_used = tile_end[-1:].astype(jnp.int32)
    return dest.astype(jnp.int32), tile_expert, n_used, n_tiles


def _sc_mesh():
    return plsc.VectorSubcoreMesh(core_axis_name="core", subcore_axis_name="subcore")


def _sc_dispatch(h2, dest_g, n_slots):
    ng = dest_g.shape[0]

    @pl.kernel(out_type=jax.ShapeDtypeStruct((n_slots, D_MODEL), F32), mesh=_sc_mesh(),
               scratch_types=[pltpu.VMEM((1, LANES), jnp.int32), pltpu.VMEM((SC_GROUP, D_MODEL), F32)])
    def kernel(h_hbm, d_hbm, o_hbm, i_vmem, buf):
        wid = lax.axis_index("core") * SC_SUBCORES + lax.axis_index("subcore")

        @pl.loop(0, pl.cdiv(ng, SC_WORKERS))
        def _(r):
            g = wid + SC_WORKERS * r

            @pl.when(g < ng)
            def _():
                pltpu.sync_copy(d_hbm.at[pl.ds(g, 1)], i_vmem)
                pltpu.sync_copy(h_hbm.at[pl.ds(g * SC_GROUP, SC_GROUP)], buf)
                for k in range(TOP_K):
                    pltpu.sync_copy(buf, o_hbm.at[i_vmem.at[0, pl.ds(k * SC_GROUP, SC_GROUP)]])

    return kernel(h2, dest_g)


def _sc_collect(y, dest_g):
    ng = dest_g.shape[0]

    @pl.kernel(out_type=jax.ShapeDtypeStruct((ng * LANES, D_MODEL), F32), mesh=_sc_mesh(),
               scratch_types=[pltpu.VMEM((1, LANES), jnp.int32), pltpu.VMEM((SC_GROUP, D_MODEL), F32)])
    def kernel(y_hbm, d_hbm, o_hbm, i_vmem, buf):
        wid = lax.axis_index("core") * SC_SUBCORES + lax.axis_index("subcore")

        @pl.loop(0, pl.cdiv(ng, SC_WORKERS))
        def _(r):
            g = wid + SC_WORKERS * r

            @pl.when(g < ng)
            def _():
                pltpu.sync_copy(d_hbm.at[pl.ds(g, 1)], i_vmem)
                for k in range(TOP_K):
                    pltpu.sync_copy(y_hbm.at[i_vmem.at[0, pl.ds(k * SC_GROUP, SC_GROUP)]], buf)
                    pltpu.sync_copy(buf, o_hbm.at[pl.ds(g * LANES + k * SC_GROUP, SC_GROUP)])

    return kernel(y, dest_g)


def _combine_kernel(xn_ref, g_ref, y_ref, o_ref):
    for gi in range(TOKEN_TILE // SC_GROUP):
        rows = slice(gi * SC_GROUP, (gi + 1) * SC_GROUP)
        acc = xn_ref[rows, :]
        for k in range(TOP_K):
            r0 = gi * LANES + k * SC_GROUP
            acc = acc + g_ref[rows, k:k + 1] * y_ref[r0:r0 + SC_GROUP, :]
        o_ref[rows, :] = acc


def _combine(xn, gates, y4):
    n = xn.shape[0]
    tm = TOKEN_TILE
    row = lambda r, w: pl.BlockSpec((r, w), lambda i: (i, 0))
    return pl.pallas_call(
        _combine_kernel,
        grid=(n // tm,),
        in_specs=[row(tm, D_MODEL), row(tm, ROUTER_PAD), row(tm * TOP_K, D_MODEL)],
        out_specs=row(tm, D_MODEL),
        out_shape=jax.ShapeDtypeStruct((n, D_MODEL), F32),
        compiler_params=_cparams(("parallel",)),
        name="moe_combine",
    )(xn, gates, y4)


def _moe(xn, h2, idx, gates, rank, counts, lw):
    n_tok = xn.shape[0]
    dest, tile_expert, n_used, n_tiles = _route(idx[:, :TOP_K], rank[:, :TOP_K], counts)
    dest_g = dest.reshape(n_tok // SC_GROUP, SC_GROUP, TOP_K).transpose(0, 2, 1).reshape(-1, LANES)
    xs = _sc_dispatch(h2, dest_g, n_tiles * MOE_TILE)
    y = _moe_ffn(xs, tile_expert, n_used, lw)
    return _combine(xn, gates, _sc_collect(y, dest_g))


def _block_diag(w):
    g, a, b = w.shape
    out = jnp.zeros((g * a, g * b), w.dtype)
    for i in range(g):
        out = out.at[i * a:(i + 1) * a, i * b:(i + 1) * b].set(w[i])
    return out


def _layer_weights(l, p):
    wr = jnp.pad(p["w_router"][l], ((0, 0), (0, ROUTER_PAD - N_EXPERTS)))
    hd = jnp.arange(D_A) // HEAD_DIM
    return {
        "ln1_g": p["ln1_g"][l][None], "w_in": p["w_in"][l].astype(BF16),
        "g_q": jnp.tile(p["g_q"][l], N_HEADS_A)[None], "g_k": jnp.tile(p["g_k"][l], N_HEADS_A)[None],
        "g_vb": p["g_vb"][l][None],
        "ones_bd": (hd[:, None] == hd[None, :]).astype(BF16),
        "w_s": p["w_s"][l],
        "bias_s": jnp.repeat(p["b_s"][l].T, HEAD_DIM, axis=1),
        "conv_w": p["conv_w"][l], "conv_b": p["conv_b"][l][None],
        "w_a_bd": _block_diag(p["w_a"][l]).astype(BF16), "b_a": p["b_a"][l][None],
        "w_x_bd": _block_diag(p["w_x"][l]).astype(BF16), "b_x": p["b_x"][l][None],
        "lru_lambda": p["lru_lambda"][l][None],
        "g_out": p["g_out"][l][None], "w_out": p["w_out"][l].astype(BF16),
        "ln2_g": p["ln2_g"][l][None],
        "w_router": wr.astype(BF16),
        "b_router": jnp.pad(p["b_router"][l], (0, ROUTER_PAD - N_EXPERTS),
                            constant_values=NEG_BIG)[None],
        "layer": l, "w_gu": p["w_gu"], "w_down": p["w_down"],
        "b_glu": p["b_gu"][l][:, None, 0::2], "b_lin": p["b_gu"][l][:, None, 1::2],
        "b_down": p["b_down"][l][:, None, :],
    }


def kernel(x_prompt, x_sample, cache_win_k, cache_win_v, state_conv, state_lru, ln1_g, w_in, g_q, g_k,
           g_vb, w_s, b_s, conv_w, conv_b, w_a, b_a, w_x, b_x, lru_lambda, g_out, w_out, ln2_g, w_router,
           b_router, w_gu, b_gu, w_down, b_down):
    params = dict(ln1_g=ln1_g, w_in=w_in, g_q=g_q, g_k=g_k, g_vb=g_vb, w_s=w_s, b_s=b_s, conv_w=conv_w,
                  conv_b=conv_b, w_a=w_a, b_a=b_a, w_x=w_x, b_x=b_x, lru_lambda=lru_lambda, g_out=g_out,
                  w_out=w_out, ln2_g=ln2_g, w_router=w_router, b_router=b_router, w_gu=w_gu, b_gu=b_gu,
                  w_down=w_down, b_down=b_down)
    bp, sp, _ = x_prompt.shape
    bs, ss, _ = x_sample.shape
    depth = w_in.shape[0]
    n_p, n_s = bp * sp, bs * ss
    keep = min(DILATED_PATTERNS[-1][0], sp)
    w_buf = cache_win_k.shape[2]
    ckt = cache_win_k.transpose(0, 1, 3, 4, 2).reshape(depth, bs, D_A, w_buf)
    cvt = cache_win_v.transpose(0, 1, 3, 4, 2).reshape(depth, bs, D_A, w_buf)
    x = jnp.concatenate([x_prompt.reshape(n_p, D_MODEL), x_sample.reshape(n_s, D_MODEL)], axis=0)
    zero_conv = jnp.zeros((bp, SUBLANES, D_C), F32)
    zero_h = jnp.zeros((bp, 1, D_C), F32)
    outs = {name: [] for name in ("pconv", "plru", "sk", "sv", "sconv", "slru", "svb")}
    window = None
    for l in range(depth):
        lw = _layer_weights(l, params)
        q, k, v, bc = _in_proj(x, lw["ln1_g"], lw["w_in"], lw["g_q"], lw["g_k"], lw["g_vb"], lw["ones_bd"])
        oa_p = _attn_prompt(q, k, v, bp, sp)
        ob_p, oc_p, h_p = _mixer_bc(bc, zero_conv, zero_h, lw, bp, sp, MIX_TILE, MIX_TILE - 1)
        oa_s = _attn_sample(q, k, v, ckt, cvt, l, n_p, bs, ss)
        bc_s = bc[n_p:].reshape(bs, ss, -1)
        bc_s_pad = jnp.pad(bc_s, ((0, 0), (0, CHUNK - ss), (0, 0))).reshape(bs * CHUNK, -1)
        conv8 = jnp.pad(state_conv[l], ((0, 0), (SUBLANES - (CONV_W - 1), 0), (0, 0)))
        ob_s, oc_s, h_s = _mixer_bc(bc_s_pad, conv8, state_lru[l][:, None, :], lw, bs, CHUNK, CHUNK, ss - 1)
        ob_s = ob_s.reshape(bs, CHUNK, D_B)[:, :ss].reshape(n_s, D_B)
        oc_s = oc_s.reshape(bs, CHUNK, D_C)[:, :ss].reshape(n_s, D_C)
        oa = jnp.concatenate([oa_p, oa_s], axis=0)
        ob = jnp.concatenate([ob_p, ob_s], axis=0)
        oc = jnp.concatenate([oc_p, oc_s], axis=0)
        xn, h2, idx, gates, rank, counts = _out_proj(oa, ob, oc, x, lw)
        x = _moe(xn, h2, idx, gates, rank, counts[:, 0, :N_EXPERTS], lw)

        window = _export_window(k, v, window, l, depth, bp, sp, keep)
        xc_p = bc[:n_p, 2 * D_B:2 * D_B + D_C].reshape(bp, sp, D_C)
        outs["pconv"].append(xc_p[:, sp - (CONV_W - 1):])
        outs["plru"].append(h_p[:, 0])
        outs["sk"].append(k[n_p:].reshape(bs, ss, N_HEADS_A, HEAD_DIM))
        outs["sv"].append(v[n_p:].reshape(bs, ss, N_HEADS_A, HEAD_DIM))
        xpad_s = jnp.concatenate([state_conv[l], bc_s[:, :, 2 * D_B:2 * D_B + D_C]], axis=1)
        outs["sconv"].append(xpad_s[:, -(CONV_W - 1):])
        outs["slru"].append(h_s[:, 0])
        outs["svb"].append(bc_s[:, :, D_B:2 * D_B])
    y_p = x[:n_p].reshape(bp, sp, D_MODEL)
    y_s = x[n_p:].reshape(bs, ss, D_MODEL)
    st = lambda name: jnp.stack(outs[name])
    heads_last = lambda t: t.reshape(depth, bp, N_HEADS_A, HEAD_DIM, keep).transpose(0, 1, 4, 2, 3)
    return (y_p, y_s, heads_last(window[0]), heads_last(window[1]), st("pconv"), st("plru"), st("sk"), st("sv"), st("sconv"),
            st("slru"), st("svb"))
```

```python
import functools

import jax
import jax.numpy as jnp
from jax import lax
from jax.experimental import pallas as pl
from jax.experimental.pallas import tpu as pltpu
from jax.experimental.pallas import tpu_sc as plsc

F32 = jnp.float32
BF16 = jnp.bfloat16

D_MODEL = 1024
HEAD_DIM = 64
N_HEADS_A = 8
D_A = N_HEADS_A * HEAD_DIM
N_HEADS_B = 4
D_B = N_HEADS_B * HEAD_DIM
N_GROUPS_C = 4
D_C = N_GROUPS_C * HEAD_DIM
D_IN = 3 * D_A + 2 * D_B + 2 * D_C
DILATED_PATTERNS = ((128, 1), (512, 4), (2048, 16))
N_PATTERNS = len(DILATED_PATTERNS)
CHUNK = 128
CONV_W = 4
LRU_C = 8.0
N_EXPERTS = 32
TOP_K = 4
D_FF = 1024
SWIGLU_LIMIT = 7.0
SWIGLU_ALPHA = 1.702
EPS = 1e-6
ATTN_SCALE = HEAD_DIM ** -0.5
PAST_LEN = 16384

LANES = 128
SUBLANES = 8
VMEM_LIMIT_BYTES = 56 * 1024 * 1024

Q_BLOCK = 128
ATTN_SPAN = 2048
ATTN_GROUP = 4
RAW_KEY_STRIDE = 16
TOKEN_TILE = 256
MIX_TILE = 512
MOE_TILE = 256
ROUTER_PAD = LANES
SC_SUBCORES = 16
SC_WORKERS = 2 * SC_SUBCORES
SC_GROUP = LANES // TOP_K
NEG_BIG = -1e30


def _cparams(semantics):
    return pltpu.CompilerParams(dimension_semantics=semantics,
                                vmem_limit_bytes=VMEM_LIMIT_BYTES)


def _full(shape):
    return pl.BlockSpec(shape, lambda *_: (0,) * len(shape))


def _rms(t, g):
    ms = jnp.mean(t * t, axis=-1, keepdims=True)
    return t * lax.rsqrt(ms + EPS) * g


def _pair_specs(tm, w, n_head):
    nh = n_head // tm
    return [pl.BlockSpec((tm, w), lambda i: (jnp.minimum(i, nh - 1), 0)),
            pl.BlockSpec((tm, w), lambda i: (jnp.maximum(i - nh, 0), 0))]


def _pick(head_ref, tail_ref, n_head_tiles):
    return jnp.where(pl.program_id(0) >= n_head_tiles, tail_ref[...], head_ref[...])


def _split_bf16(t):
    hi = t.astype(BF16)
    lo = (t - hi.astype(F32)).astype(BF16)
    return hi, lo


def _in_proj_kernel(xh_ref, xt_ref, g1_ref, w_ref, gq_ref, gk_ref, gvb_ref, ones_ref,
                    q_ref, k_ref, v_ref, bc_ref, *, n_head_tiles):
    h = _rms(_pick(xh_ref, xt_ref, n_head_tiles), g1_ref[...]).astype(BF16)
    z = jnp.dot(h, w_ref[...], preferred_element_type=F32)

    def head_norm(t, g):
        hi, lo = _split_bf16(t * t)
        ss = (jnp.dot(hi, ones_ref[...], preferred_element_type=F32)
              + jnp.dot(lo, ones_ref[...], preferred_element_type=F32))
        return t * lax.rsqrt(ss * (1.0 / HEAD_DIM) + EPS) * g

    q_ref[...] = head_norm(z[:, 0:D_A], gq_ref[...]) * ATTN_SCALE
    k_ref[...] = head_norm(z[:, D_A:2 * D_A], gk_ref[...])
    v_ref[...] = z[:, 2 * D_A:3 * D_A]
    o = 3 * D_A
    bc_ref[...] = z[:, o:]
    bc_ref[:, D_B:2 * D_B] = _rms(z[:, o + D_B:o + 2 * D_B], gvb_ref[...])


def _in_proj(x_pair, g1, w_bf16, gq, gk, gvb, ones_bd):
    n_head = x_pair[0].shape[0]
    n = n_head + x_pair[1].shape[0]
    tm = TOKEN_TILE
    row = lambda w: pl.BlockSpec((tm, w), lambda i: (i, 0))
    return pl.pallas_call(
        functools.partial(_in_proj_kernel, n_head_tiles=n_head // tm),
        grid=(n // tm,),
        in_specs=_pair_specs(tm, D_MODEL, n_head) + [_full((1, D_MODEL)), _full((D_MODEL, D_IN)), _full((1, D_A)),
                  _full((1, D_A)), _full((1, D_B)), _full((D_A, D_A))],
        out_specs=[row(D_A), row(D_A), row(D_A), row(2 * D_B + 2 * D_C)],
        out_shape=[jax.ShapeDtypeStruct((n, D_A), F32)] * 3
        + [jax.ShapeDtypeStruct((n, 2 * D_B + 2 * D_C), F32)],
        compiler_params=_cparams(("parallel",)),
        name="in_proj",
    )(*x_pair, g1, w_bf16, gq, gk, gvb, ones_bd)


def _attn_prompt_kernel(q_ref, kp_ref, kc_ref, vp_ref, vc_ref, o_ref, kq0a, kq0b, kq1a, kq1b, vqh, vql,
                        m_s, l_s, a_s):
    span = pl.program_id(2)
    qb2 = 2 * Q_BLOCK
    lane = lax.broadcasted_iota(jnp.int32, (Q_BLOCK, LANES), 1)
    head0 = lane < HEAD_DIM
    lane2 = lax.broadcasted_iota(jnp.int32, (qb2, LANES), 1)
    head0_2 = lane2 < HEAD_DIM
    swap = lambda t: pltpu.roll(t, HEAD_DIM, axis=1)

    def pack(k_ref, v_ref, base):
        def body(c, carry):
            src = pl.ds(pl.multiple_of(c * qb2, qb2), qb2)
            dst = pl.ds(pl.multiple_of(base + c * qb2, qb2), qb2)
            k = k_ref[src, :]
            kh = k.astype(BF16).astype(F32)
            kl = k - kh
            kq0a[dst, :] = jnp.where(head0_2, kh, swap(kh))
            kq0b[dst, :] = jnp.where(head0_2, kl, 0.0)
            kq1a[dst, :] = jnp.where(head0_2, swap(kh), kh)
            kq1b[dst, :] = jnp.where(head0_2, swap(kl), 0.0)
            v = v_ref[src, :]
            vh = v.astype(BF16).astype(F32)
            vqh[dst, :] = vh
            vql[dst, :] = v - vh
            return carry
        lax.fori_loop(0, ATTN_SPAN // qb2, body, 0)

    pack(kp_ref, vp_ref, 0)
    pack(kc_ref, vc_ref, ATTN_SPAN)

    row = lax.broadcasted_iota(jnp.int32, (qb2, qb2), 0) & (Q_BLOCK - 1)
    col = lax.broadcasted_iota(jnp.int32, (qb2, qb2), 1)
    band = (col >= row) & (col <= row + Q_BLOCK)
    cur = col >= Q_BLOCK
    nt = (((1,), (1,)), ((), ()))

    def ds(start, size, d):
        return pl.ds(start, size) if d == 1 else pl.ds(start, size, stride=d)

    for p, (_, d) in enumerate(DILATED_PATTERNS):
        nblk = ATTN_SPAN // (Q_BLOCK * d)
        assert d < RAW_KEY_STRIDE or nblk == 1

        def scores(blk, d=d, nblk=nblk):
            r = blk // nblk
            ib = blk % nblk
            qstart = r + d * Q_BLOCK * ib
            if d == 1:
                qstart = pl.multiple_of(Q_BLOCK * blk, Q_BLOCK)
            kstart = ATTN_SPAN + qstart - d * Q_BLOCK
            q = q_ref[ds(qstart, Q_BLOCK, d), :]
            qh = q.astype(BF16).astype(F32)
            ql = q - qh
            lhs0 = jnp.concatenate([jnp.where(head0, qh, swap(ql)), jnp.where(head0, qh, 0.0)], axis=1)
            lhs1 = jnp.concatenate([jnp.where(head0, swap(qh), ql), jnp.where(head0, swap(qh), 0.0)], axis=1)
            if d < RAW_KEY_STRIDE:
                keys = ds(kstart, qb2, d)
                k0 = jnp.concatenate([kq0a[keys, :], kq0b[keys, :]], axis=1).astype(BF16)
                k1 = jnp.concatenate([kq1a[keys, :], kq1b[keys, :]], axis=1).astype(BF16)
                vh, vl = vqh[keys, :], vql[keys, :]
            else:
                both = lambda p_ref, c_ref: jnp.concatenate(
                    [p_ref[ds(r, Q_BLOCK, d), :], c_ref[ds(r, Q_BLOCK, d), :]], axis=0)
                k = both(kp_ref, kc_ref)
                kh = k.astype(BF16).astype(F32)
                kl = k - kh
                k0 = jnp.concatenate([jnp.where(head0_2, kh, swap(kh)), jnp.where(head0_2, kl, 0.0)],
                                     axis=1).astype(BF16)
                k1 = jnp.concatenate([jnp.where(head0_2, swap(kh), kh), jnp.where(head0_2, swap(kl), 0.0)],
                                     axis=1).astype(BF16)
                v = both(vp_ref, vc_ref)
                vh = v.astype(BF16).astype(F32)
                vl = v - vh
            s0 = lax.dot_general(lhs0.astype(BF16), k0, nt, preferred_element_type=F32)
            s1 = lax.dot_general(lhs1.astype(BF16), k1, nt, preferred_element_type=F32)
            prev_ok = jnp.logical_or(ib > 0, span > 0)
            s = jnp.concatenate([s0, s1], axis=0)
            return jnp.where(band & (cur | prev_ok), s, -jnp.inf), qstart, (vh, vl)

        def softmax(s):
            m = jnp.max(s, axis=-1, keepdims=True)
            e = jnp.exp(s - m)
            return m, e, jnp.sum(e, axis=-1, keepdims=True)

        def weighted(e, v_parts):
            eh, el = _split_bf16(e)
            vh, vl = v_parts
            rhs = jnp.concatenate([jnp.concatenate([vh, vl], axis=1),
                                   jnp.concatenate([vh, jnp.zeros_like(vh)], axis=1)], axis=0).astype(BF16)
            out = jnp.dot(jnp.concatenate([eh, el], axis=1), rhs, preferred_element_type=F32)
            return out[:, :LANES] + out[:, LANES:]

        def store(qstart, m, l, acc, p=p, d=d):
            dst = ds(qstart, Q_BLOCK, d)
            shape = (Q_BLOCK, LANES)
            m_s[p, dst, :] = jnp.where(head0, jnp.broadcast_to(m[:Q_BLOCK], shape),
                                       jnp.broadcast_to(m[Q_BLOCK:], shape))
            l_s[p, dst, :] = jnp.where(head0, jnp.broadcast_to(l[:Q_BLOCK], shape),
                                       jnp.broadcast_to(l[Q_BLOCK:], shape))
            a_s[p, dst, :] = jnp.where(head0, acc[:Q_BLOCK], acc[Q_BLOCK:])

        def body(it, carry):
            sc = [scores(it * ATTN_GROUP + g) for g in range(ATTN_GROUP)]
            sm = [softmax(s) for s, _, _ in sc]
            ac = [weighted(e, v_parts) for (_, e, _), (_, _, v_parts) in zip(sm, sc)]
            for (_, qstart, _), (m, _, l), acc in zip(sc, sm, ac):
                store(qstart, m, l, acc)
            return carry

        lax.fori_loop(0, ATTN_SPAN // (Q_BLOCK * ATTN_GROUP), body, 0)

    def merge(c, carry):
        rows = pl.ds(pl.multiple_of(c * Q_BLOCK, Q_BLOCK), Q_BLOCK)
        ms = [m_s[p, rows, :] for p in range(N_PATTERNS)]
        m_all = functools.reduce(jnp.maximum, ms)
        ws = [jnp.exp(m - m_all) for m in ms]
        num = sum(w * a_s[p, rows, :] for p, w in enumerate(ws))
        den = sum(w * l_s[p, rows, :] for p, w in enumerate(ws))
        o_ref[rows, :] = num / den
        return carry

    lax.fori_loop(0, ATTN_SPAN // Q_BLOCK, merge, 0)


def _attn_prompt(q, k, v, batch, seq):
    nspan = seq // ATTN_SPAN
    blk = (ATTN_SPAN, LANES)
    cur = pl.BlockSpec(blk, lambda b, hp, s: (b * nspan + s, hp))
    prev = pl.BlockSpec(blk, lambda b, hp, s: (b * nspan + jnp.maximum(s - 1, 0), hp))
    acc = pltpu.VMEM((N_PATTERNS, ATTN_SPAN, LANES), F32)
    packed = pltpu.VMEM((2 * ATTN_SPAN, LANES), F32)
    return pl.pallas_call(
        _attn_prompt_kernel,
        grid=(batch, D_A // LANES, nspan),
        in_specs=[cur, prev, cur, prev, cur],
        out_specs=cur,
        out_shape=jax.ShapeDtypeStruct((batch * seq, D_A), F32),
        scratch_shapes=[packed] * 6 + [acc, acc, acc],
        compiler_params=_cparams(("parallel", "parallel", "arbitrary")),
        name="attn_prompt",
    )(q, k, k, v, v)


def _attn_sample_kernel(q_ref, kn_ref, vn_ref, ckt_ref, cvt_ref, o_ref, *, w_buf, t_new):
    pad = LANES - t_new
    zeros = jnp.zeros((pad, D_A), F32)
    kn = jnp.concatenate([kn_ref[...], zeros], axis=0).astype(BF16)
    vn = jnp.concatenate([vn_ref[...], zeros], axis=0).astype(BF16)
    n_rows = N_HEADS_A * t_new
    hrow = lax.broadcasted_iota(jnp.int32, (n_rows, D_A), 0) // t_new
    hlane = lax.broadcasted_iota(jnp.int32, (n_rows, D_A), 1) // HEAD_DIM
    own = hrow == hlane
    q_rep = jnp.concatenate([q_ref[...]] * N_HEADS_A, axis=0)
    q64 = jnp.where(own, q_rep, 0.0).astype(BF16)
    nt = (((1,), (1,)), ((), ()))
    s_w = jnp.dot(q64, ckt_ref[...].astype(BF16), preferred_element_type=F32)
    s_n = lax.dot_general(q64, kn, nt, preferred_element_type=F32)
    vt = cvt_ref[...].astype(BF16)

    def dist(n_cols, first):
        t = lax.broadcasted_iota(jnp.int32, (n_rows, n_cols), 0) % t_new
        return t - lax.broadcasted_iota(jnp.int32, (n_rows, n_cols), 1) - first

    dist_w, dist_n = dist(w_buf, -w_buf), dist(LANES, 0)
    ms, ls, accs = [], [], []
    for w, d in DILATED_PATTERNS:
        ok = lambda ds: (ds >= 0) & (ds <= w) & ((ds & (d - 1)) == 0)
        sw = jnp.where(ok(dist_w), s_w, -jnp.inf)
        sn = jnp.where(ok(dist_n), s_n, -jnp.inf)
        m = jnp.maximum(jnp.max(sw, axis=-1, keepdims=True), jnp.max(sn, axis=-1, keepdims=True))
        ew = jnp.exp(sw - m)
        en = jnp.exp(sn - m)
        ms.append(m)
        ls.append(jnp.sum(ew, axis=-1, keepdims=True) + jnp.sum(en, axis=-1, keepdims=True))
        accs.append(lax.dot_general(ew.astype(BF16), vt, nt, preferred_element_type=F32)
                    + jnp.dot(en.astype(BF16), vn, preferred_element_type=F32))
    m_all = functools.reduce(jnp.maximum, ms)
    ws = [jnp.exp(m - m_all) for m in ms]
    num = sum(w * a for w, a in zip(ws, accs))
    den = sum(w * l for w, l in zip(ws, ls))
    o = jnp.where(own, num / den, 0.0)
    out = o[0:t_new]
    for h in range(1, N_HEADS_A):
        out = out + o[h * t_new:(h + 1) * t_new]
    o_ref[...] = out


def _attn_sample(q, k, v, cache_kt, cache_vt, layer, row0, batch, t_new):
    w_buf = cache_kt.shape[3]
    assert row0 % t_new == 0 and t_new == SUBLANES
    new = pl.BlockSpec((t_new, D_A), lambda b: (row0 // t_new + b, 0))
    cache = pl.BlockSpec((None, None, D_A, w_buf), lambda b: (layer, b, 0, 0))
    return pl.pallas_call(
        functools.partial(_attn_sample_kernel, w_buf=w_buf, t_new=t_new),
        grid=(batch,),
        in_specs=[new, new, new, cache, cache],
        out_specs=pl.BlockSpec((t_new, D_A), lambda b: (b, 0)),
        out_shape=jax.ShapeDtypeStruct((batch * t_new, D_A), F32),
        compiler_params=_cparams(("parallel",)),
        name="attn_sample",
    )(q, k, v, cache_kt, cache_vt)


def _export_kernel(*refs):
    k_ref, v_ref = refs[0], refs[1]
    pk_ref, pv_ref = refs[-2], refs[-1]
    pk_ref[...] = k_ref[...].T
    pv_ref[...] = v_ref[...].T


def _export_window(k, v, prev, layer, depth, batch, seq, keep):
    tm = TOKEN_TILE
    first = (seq - keep) // tm
    src = pl.BlockSpec((tm, D_A), lambda b, j: (b * (seq // tm) + first + j, 0))
    dst = pl.BlockSpec((None, None, D_A, tm), lambda b, j: (layer, b, 0, j))
    shape = jax.ShapeDtypeStruct((depth, batch, D_A, keep), F32)
    carried = [] if prev is None else list(prev)
    return pl.pallas_call(
        _export_kernel,
        grid=(batch, keep // tm),
        in_specs=[src, src] + [pl.BlockSpec(memory_space=pl.ANY)] * len(carried),
        out_specs=[dst, dst],
        out_shape=[shape, shape],
        input_output_aliases={2 + i: i for i in range(len(carried))},
        compiler_params=_cparams(("parallel", "parallel")),
        name="export_window",
    )(k, v, *carried)


def _gelu_tanh(x):
    return 0.5 * x * (1.0 + jnp.tanh(0.7978845608028654 * (x + 0.044715 * x * x * x)))


def _mixer_bc_kernel(bc_ref, cb_ref, h0_ref, ws_ref, bs_ref, cw_ref, cbias_ref, wa_ref, ba_ref,
                     wx_ref, bx_ref, lam_ref, ob_ref, oc_ref, hl_ref, xp_s, h_s, *, tt, last_row):
    j = pl.program_id(1)

    @pl.when(j == 0)
    def _():
        xp_s[0:SUBLANES, :] = cb_ref[...]
        h_s[...] = h0_ref[...]

    nch = tt // CHUNK
    vcat = jnp.concatenate([bc_ref[c * CHUNK:(c + 1) * CHUNK, D_B:2 * D_B] for c in range(nch)],
                           axis=1).astype(BF16)
    ri = lax.broadcasted_iota(jnp.int32, (CHUNK, CHUNK), 0)
    ci = lax.broadcasted_iota(jnp.int32, (CHUNK, CHUNK), 1)
    hl = (lax.broadcasted_iota(jnp.int32, (CHUNK, nch * D_B), 1) % D_B) // HEAD_DIM
    mixed = jnp.zeros((CHUNK, nch * D_B), F32)
    for h in range(N_HEADS_B):
        wh = jnp.where(ri >= ci, ws_ref[h], 0.0).astype(BF16)
        mh = jnp.dot(wh, vcat, preferred_element_type=F32)
        mixed = mixed + jnp.where(hl == h, mh, 0.0)
    for c in range(nch):
        rows = slice(c * CHUNK, (c + 1) * CHUNK)
        ob_ref[rows, :] = bc_ref[rows, 0:D_B] * (mixed[:, c * D_B:(c + 1) * D_B] + bs_ref[...])

    xc = bc_ref[:, 2 * D_B:2 * D_B + D_C]
    xp_s[SUBLANES:SUBLANES + tt, :] = xc
    xconv = cbias_ref[...] + cw_ref[CONV_W - 1:CONV_W, :] * xc
    for kk in range(CONV_W - 1):
        off = SUBLANES - (CONV_W - 1) + kk
        xconv = xconv + cw_ref[kk:kk + 1, :] * xp_s[off:off + tt, :]
    xp_s[0:SUBLANES, :] = xp_s[tt:tt + SUBLANES, :]
    xb = xconv.astype(BF16)
    r = jax.nn.sigmoid(jnp.dot(xb, wa_ref[...], preferred_element_type=F32) + ba_ref[...])
    i = jax.nn.sigmoid(jnp.dot(xb, wx_ref[...], preferred_element_type=F32) + bx_ref[...])
    nl = -lam_ref[...]
    softplus = jnp.maximum(nl, 0.0) + jnp.log1p(jnp.exp(-jnp.abs(nl)))
    a = jnp.exp(-LRU_C * r * softplus)
    b = jnp.sqrt(1.0 - a * a) * (i * xconv)
    rowi = lax.broadcasted_iota(jnp.int32, (tt, D_C), 0)
    step = 1
    while step < tt:
        a_sh = pltpu.roll(a, step, axis=0)
        b_sh = pltpu.roll(b, step, axis=0)
        live = rowi >= step
        b = jnp.where(live, a * b_sh + b, b)
        a = jnp.where(live, a * a_sh, a)
        step *= 2
    h = a * h_s[...] + b
    h_s[...] = h[tt - 1:tt, :]
    oc_ref[...] = h * _gelu_tanh(bc_ref[:, 2 * D_B + D_C:])

    @pl.when(j == pl.num_programs(1) - 1)
    def _():
        hl_ref[...] = h[last_row:last_row + 1, :]


def _mixer_bc(bc, conv_buf8, h0, lw, batch, t_len, tt, last_row):
    nt = t_len // tt
    rows = lambda w: pl.BlockSpec((tt, w), lambda b, j: (b * nt + j, 0))
    per_b = lambda s: pl.BlockSpec((None,) + s, lambda b, j: (b,) + (0,) * len(s))
    return pl.pallas_call(
        functools.partial(_mixer_bc_kernel, tt=tt, last_row=last_row),
        grid=(batch, nt),
        in_specs=[rows(2 * D_B + 2 * D_C), per_b((SUBLANES, D_C)), per_b((1, D_C)),
                  _full((N_HEADS_B, CHUNK, CHUNK)), _full((CHUNK, D_B)), _full((CONV_W, D_C)),
                  _full((1, D_C)), _full((D_C, D_C)), _full((1, D_C)), _full((D_C, D_C)),
                  _full((1, D_C)), _full((1, D_C))],
        out_specs=[rows(D_B), rows(D_C), per_b((1, D_C))],
        out_shape=[jax.ShapeDtypeStruct((batch * t_len, D_B), F32),
                   jax.ShapeDtypeStruct((batch * t_len, D_C), F32),
                   jax.ShapeDtypeStruct((batch, 1, D_C), F32)],
        scratch_shapes=[pltpu.VMEM((tt + SUBLANES, D_C), F32), pltpu.VMEM((1, D_C), F32)],
        compiler_params=_cparams(("parallel", "arbitrary")),
        name="mixer_bc",
    )(bc, conv_buf8, h0, lw["w_s"], lw["bias_s"], lw["conv_w"], lw["conv_b"], lw["w_a_bd"],
      lw["b_a"], lw["w_x_bd"], lw["b_x"], lw["lru_lambda"])


def _out_proj_kernel(oah_ref, oat_ref, obh_ref, obt_ref, och_ref, oct_ref, xh_ref, xt_ref, go_ref, wo_ref,
                     g2_ref, wr_ref, br_ref, xn_ref, h2_ref, idx_ref, gate_ref, rank_ref, cnt_ref, *,
                     n_head_tiles):
    pick = functools.partial(_pick, n_head_tiles=n_head_tiles)
    oa = _rms(pick(oah_ref, oat_ref), go_ref[:, 0:D_A]).astype(BF16)
    ob = _rms(pick(obh_ref, obt_ref), go_ref[:, D_A:D_A + D_B]).astype(BF16)
    oc = _rms(pick(och_ref, oct_ref), go_ref[:, D_A + D_B:]).astype(BF16)
    y = (jnp.dot(oa, wo_ref[0:D_A, :], preferred_element_type=F32)
         + jnp.dot(ob, wo_ref[D_A:D_A + D_B, :], preferred_element_type=F32)
         + jnp.dot(oc, wo_ref[D_A + D_B:, :], preferred_element_type=F32))
    xn = pick(xh_ref, xt_ref) + y
    xn_ref[...] = xn
    h2 = _rms(xn, g2_ref[...])
    h2_ref[...] = h2
    logits = jnp.dot(h2.astype(BF16), wr_ref[...], preferred_element_type=F32) + br_ref[...]
    lane = lax.broadcasted_iota(jnp.int32, logits.shape, 1).astype(F32)
    cur = logits
    tops, idxs = [], []
    for _ in range(TOP_K):
        m = jnp.max(cur, axis=-1, keepdims=True)
        ix = jnp.min(jnp.where(cur == m, lane, float(ROUTER_PAD)), axis=-1, keepdims=True)
        tops.append(m)
        idxs.append(ix)
        cur = jnp.where(lane == ix, -jnp.inf, cur)
    es = [jnp.exp(t - tops[0]) for t in tops]
    den = sum(es)
    idx_out = jnp.zeros(logits.shape, F32)
    gate_out = jnp.zeros(logits.shape, F32)
    for kk in range(TOP_K):
        idx_out = jnp.where(lane == kk, idxs[kk], idx_out)
        gate_out = jnp.where(lane == kk, es[kk] / den, gate_out)
    idx_ref[...] = idx_out.astype(jnp.int32)
    gate_ref[...] = gate_out
    tm = logits.shape[0]
    chosen = jnp.zeros(logits.shape, F32)
    for kk in range(TOP_K):
        chosen = jnp.where(lane == idxs[kk], 1.0, chosen)
    below = (lax.broadcasted_iota(jnp.int32, (tm, tm), 0)
             > lax.broadcasted_iota(jnp.int32, (tm, tm), 1)).astype(BF16)
    earlier = jnp.dot(below, chosen.astype(BF16), preferred_element_type=F32)
    rank_out = jnp.zeros(logits.shape, F32)
    for kk in range(TOP_K):
        rk = jnp.sum(jnp.where(lane == idxs[kk], earlier, 0.0), axis=-1, keepdims=True)
        rank_out = jnp.where(lane == kk, rk, rank_out)
    rank_ref[...] = rank_out.astype(jnp.int32)
    cnt_ref[...] = jnp.broadcast_to(jnp.sum(chosen, axis=0, keepdims=True),
                                    cnt_ref.shape).astype(jnp.int32)


def _out_proj(oa_pair, ob_pair, oc_pair, x_pair, lw):
    n_head = x_pair[0].shape[0]
    n = n_head + x_pair[1].shape[0]
    tm = TOKEN_TILE
    row = lambda w: pl.BlockSpec((tm, w), lambda i: (i, 0))
    pair = lambda w: _pair_specs(tm, w, n_head)
    return pl.pallas_call(
        functools.partial(_out_proj_kernel, n_head_tiles=n_head // tm),
        grid=(n // tm,),
        in_specs=pair(D_A) + pair(D_B) + pair(D_C) + pair(D_MODEL) + [_full((1, D_MODEL)),
                  _full((D_MODEL, D_MODEL)), _full((1, D_MODEL)), _full((D_MODEL, ROUTER_PAD)),
                  _full((1, ROUTER_PAD))],
        out_specs=[row(D_MODEL), row(D_MODEL), row(ROUTER_PAD), row(ROUTER_PAD), row(ROUTER_PAD),
                   pl.BlockSpec((None, SUBLANES, ROUTER_PAD), lambda i: (i, 0, 0))],
        out_shape=[jax.ShapeDtypeStruct((n, D_MODEL), F32), jax.ShapeDtypeStruct((n, D_MODEL), F32),
                   jax.ShapeDtypeStruct((n, ROUTER_PAD), jnp.int32),
                   jax.ShapeDtypeStruct((n, ROUTER_PAD), F32),
                   jax.ShapeDtypeStruct((n, ROUTER_PAD), jnp.int32),
                   jax.ShapeDtypeStruct((n // tm, SUBLANES, ROUTER_PAD), jnp.int32)],
        compiler_params=_cparams(("parallel",)),
        name="out_proj",
    )(*oa_pair, *ob_pair, *oc_pair, *x_pair, lw["g_out"], lw["w_out"], lw["ln2_g"], lw["w_router"],
      lw["b_router"])


def _moe_kernel(te_ref, nu_ref, x_ref, wgu_ref, bg_ref, bl_ref, wd_ref, bd_ref, y_ref, wg_s, wl_s, wd_s):
    i = pl.program_id(0)
    used = i < nu_ref[0]
    new_expert = jnp.logical_or(i == 0, te_ref[i] != te_ref[jnp.maximum(i - 1, 0)])

    @pl.when(jnp.logical_and(used, new_expert))
    def _():
        w2 = 2 * LANES
        src = lax.broadcasted_iota(jnp.int32, (w2, w2), 0)
        col = lax.broadcasted_iota(jnp.int32, (w2, w2), 1)
        pick = (src == jnp.where(col < LANES, 2 * col, 2 * (col - LANES) + 1)).astype(BF16)
        for c in range(2 * D_FF // w2):
            out = jnp.dot(wgu_ref[:, c * w2:(c + 1) * w2].astype(BF16), pick, preferred_element_type=F32)
            wg_s[:, c * LANES:(c + 1) * LANES] = out[:, :LANES].astype(BF16)
            wl_s[:, c * LANES:(c + 1) * LANES] = out[:, LANES:].astype(BF16)
        wd_s[...] = wd_ref[...].astype(BF16)

    @pl.when(used)
    def _():
        x = x_ref[...].astype(BF16)
        zg = jnp.dot(x, wg_s[...], preferred_element_type=F32) + bg_ref[...]
        zl = jnp.dot(x, wl_s[...], preferred_element_type=F32) + bl_ref[...]
        glu = jnp.minimum(zg, SWIGLU_LIMIT)
        lin = jnp.clip(zl, -SWIGLU_LIMIT, SWIGLU_LIMIT)
        act = glu * jax.nn.sigmoid(SWIGLU_ALPHA * glu) * (lin + 1.0)
        y_ref[...] = jnp.dot(act.astype(BF16), wd_s[...], preferred_element_type=F32) + bd_ref[...]

    @pl.when(i >= nu_ref[0])
    def _():
        y_ref[...] = jnp.zeros(y_ref.shape, F32)


def _moe_ffn(xs, tile_expert, n_used, lw):
    n_slots = xs.shape[0]
    tm = MOE_TILE
    layer = lw["layer"]
    wspec = lambda r, c: pl.BlockSpec((None, r, c), lambda i, te, nu: (te[i], 0, 0))
    wfull = lambda r, c: pl.BlockSpec((None, None, r, c), lambda i, te, nu: (layer, te[i], 0, 0))
    return pl.pallas_call(
        _moe_kernel,
        grid_spec=pltpu.PrefetchScalarGridSpec(
            num_scalar_prefetch=2,
            grid=(n_slots // tm,),
            in_specs=[pl.BlockSpec((tm, D_MODEL), lambda i, te, nu: (i, 0)),
                      wfull(D_MODEL, 2 * D_FF), wspec(1, D_FF), wspec(1, D_FF),
                      wfull(D_FF, D_MODEL), wspec(1, D_MODEL)],
            out_specs=pl.BlockSpec((tm, D_MODEL), lambda i, te, nu: (i, 0)),
            scratch_shapes=[pltpu.VMEM((D_MODEL, D_FF), BF16), pltpu.VMEM((D_MODEL, D_FF), BF16),
                            pltpu.VMEM((D_FF, D_MODEL), BF16)],
        ),
        out_shape=jax.ShapeDtypeStruct((n_slots, D_MODEL), F32),
        compiler_params=_cparams(("arbitrary",)),
        name="moe_ffn",
    )(tile_expert, n_used, xs, lw["w_gu"], lw["b_glu"], lw["b_lin"], lw["w_down"], lw["b_down"])


def _route(idx, rank, counts):
    n_tok = idx.shape[0]
    experts = jnp.arange(N_EXPERTS, dtype=jnp.int32)
    totals = jnp.sum(counts, axis=0)
    tiles = (totals + MOE_TILE - 1) // MOE_TILE
    tile_end = jnp.sum(jnp.where(experts[None, :] <= experts[:, None], tiles[None, :], 0), axis=1)
    pad_start = (tile_end - tiles) * MOE_TILE
    tt = jnp.arange(counts.shape[0], dtype=jnp.int32)
    before = jnp.sum(jnp.where((tt[None, :] < tt[:, None])[:, :, None], counts[None, :, :], 0), axis=1)
    base = pad_start[None, :] + before
    base_tok = jnp.repeat(base, TOKEN_TILE, axis=0)
    picked = idx[:, :, None] == experts[None, None, :]
    dest = jnp.sum(jnp.where(picked, base_tok[:, None, :], 0), axis=-1) + rank
    n_tiles = -(-(n_tok * TOP_K) // MOE_TILE) + N_EXPERTS
    tile_ids = jnp.arange(n_tiles, dtype=jnp.int32)
    tile_expert = jnp.minimum(jnp.sum((tile_end[None, :] <= tile_ids[:, None]).astype(jnp.int32), axis=1),
                              N_EXPERTS - 1)
    n_used = tile_end[-1:].astype(jnp.int32)
    return dest.astype(jnp.int32), tile_expert, n_used, n_tiles


def _sc_mesh():
    return plsc.VectorSubcoreMesh(core_axis_name="core", subcore_axis_name="subcore")


def _sc_dispatch(h2, dest_g, n_slots):
    ng = dest_g.shape[0]

    @pl.kernel(out_type=jax.ShapeDtypeStruct((n_slots, D_MODEL), F32), mesh=_sc_mesh(),
               scratch_types=[pltpu.VMEM((1, LANES), jnp.int32), pltpu.VMEM((SC_GROUP, D_MODEL), F32)]
               + [pltpu.SemaphoreType.DMA] * TOP_K)
    def kernel(h_hbm, d_hbm, o_hbm, i_vmem, buf, *sems):
        wid = lax.axis_index("core") * SC_SUBCORES + lax.axis_index("subcore")

        @pl.loop(0, pl.cdiv(ng, SC_WORKERS))
        def _(r):
            g = wid + SC_WORKERS * r

            @pl.when(g < ng)
            def _():
                pltpu.sync_copy(d_hbm.at[pl.ds(g, 1)], i_vmem)
                pltpu.sync_copy(h_hbm.at[pl.ds(g * SC_GROUP, SC_GROUP)], buf)
                puts = [pltpu.async_copy(buf, o_hbm.at[i_vmem.at[0, pl.ds(k * SC_GROUP, SC_GROUP)]], sems[k])
                        for k in range(TOP_K)]
                for put in puts:
                    put.wait()

    return kernel(h2, dest_g)


def _sc_collect(y, dest_g):
    ng = dest_g.shape[0]

    @pl.kernel(out_type=jax.ShapeDtypeStruct((ng * LANES, D_MODEL), F32), mesh=_sc_mesh(),
               scratch_types=[pltpu.VMEM((1, LANES), jnp.int32)] + [pltpu.VMEM((SC_GROUP, D_MODEL), F32)] * 2
               + [pltpu.SemaphoreType.DMA] * 4)
    def kernel(y_hbm, d_hbm, o_hbm, i_vmem, buf0, buf1, g0, g1, w0, w1):
        wid = lax.axis_index("core") * SC_SUBCORES + lax.axis_index("subcore")
        bufs, gsem, wsem = (buf0, buf1), (g0, g1), (w0, w1)

        @pl.loop(0, pl.cdiv(ng, SC_WORKERS))
        def _(r):
            g = wid + SC_WORKERS * r

            @pl.when(g < ng)
            def _():
                pltpu.sync_copy(d_hbm.at[pl.ds(g, 1)], i_vmem)
                get = lambda k: pltpu.async_copy(
                    y_hbm.at[i_vmem.at[0, pl.ds(k * SC_GROUP, SC_GROUP)]], bufs[k % 2], gsem[k % 2])
                put = lambda k: pltpu.async_copy(
                    bufs[k % 2], o_hbm.at[pl.ds(g * LANES + k * SC_GROUP, SC_GROUP)], wsem[k % 2])
                gets = [get(0), get(1)]
                puts = []
                for k in range(TOP_K):
                    gets[k].wait()
                    puts.append(put(k))
                    if k + 2 < TOP_K:
                        puts[k].wait()
                        gets.append(get(k + 2))
                for k in range(TOP_K - 2, TOP_K):
                    puts[k].wait()

    return kernel(y, dest_g)


def _combine_kernel(xn_ref, g_ref, y_ref, oh_ref, ot_ref, *, n_head_tiles):
    def emit(o_ref):
        for gi in range(TOKEN_TILE // SC_GROUP):
            rows = slice(gi * SC_GROUP, (gi + 1) * SC_GROUP)
            acc = xn_ref[rows, :]
            for k in range(TOP_K):
                r0 = gi * LANES + k * SC_GROUP
                acc = acc + g_ref[rows, k:k + 1] * y_ref[r0:r0 + SC_GROUP, :]
            o_ref[rows, :] = acc

    pl.when(pl.program_id(0) < n_head_tiles)(lambda: emit(oh_ref))
    pl.when(pl.program_id(0) >= n_head_tiles)(lambda: emit(ot_ref))


def _combine(xn, gates, y4, n_head):
    n = xn.shape[0]
    tm = TOKEN_TILE
    row = lambda r, w: pl.BlockSpec((r, w), lambda i: (i, 0))
    return pl.pallas_call(
        functools.partial(_combine_kernel, n_head_tiles=n_head // tm),
        grid=(n // tm,),
        in_specs=[row(tm, D_MODEL), row(tm, ROUTER_PAD), row(tm * TOP_K, D_MODEL)],
        out_specs=_pair_specs(tm, D_MODEL, n_head),
        out_shape=[jax.ShapeDtypeStruct((n_head, D_MODEL), F32),
                   jax.ShapeDtypeStruct((n - n_head, D_MODEL), F32)],
        compiler_params=_cparams(("arbitrary",)),
        name="moe_combine",
    )(xn, gates, y4)


def _moe(xn, h2, idx, gates, rank, counts, lw, n_head):
    n_tok = xn.shape[0]
    dest, tile_expert, n_used, n_tiles = _route(idx[:, :TOP_K], rank[:, :TOP_K], counts)
    dest_g = dest.reshape(n_tok // SC_GROUP, SC_GROUP, TOP_K).transpose(0, 2, 1).reshape(-1, LANES)
    xs = _sc_dispatch(h2, dest_g, n_tiles * MOE_TILE)
    y = _moe_ffn(xs, tile_expert, n_used, lw)
    return _combine(xn, gates, _sc_collect(y, dest_g), n_head)


def _block_diag(w):
    g, a, b = w.shape
    out = jnp.zeros((g * a, g * b), w.dtype)
    for i in range(g):
        out = out.at[i * a:(i + 1) * a, i * b:(i + 1) * b].set(w[i])
    return out


def _layer_weights(l, p):
    wr = jnp.pad(p["w_router"][l], ((0, 0), (0, ROUTER_PAD - N_EXPERTS)))
    hd = jnp.arange(D_A) // HEAD_DIM
    return {
        "ln1_g": p["ln1_g"][l][None], "w_in": p["w_in"][l].astype(BF16),
        "g_q": jnp.tile(p["g_q"][l], N_HEADS_A)[None], "g_k": jnp.tile(p["g_k"][l], N_HEADS_A)[None],
        "g_vb": p["g_vb"][l][None],
        "ones_bd": (hd[:, None] == hd[None, :]).astype(BF16),
        "w_s": p["w_s"][l],
        "bias_s": jnp.repeat(p["b_s"][l].T, HEAD_DIM, axis=1),
        "conv_w": p["conv_w"][l], "conv_b": p["conv_b"][l][None],
        "w_a_bd": _block_diag(p["w_a"][l]).astype(BF16), "b_a": p["b_a"][l][None],
        "w_x_bd": _block_diag(p["w_x"][l]).astype(BF16), "b_x": p["b_x"][l][None],
        "lru_lambda": p["lru_lambda"][l][None],
        "g_out": p["g_out"][l][None], "w_out": p["w_out"][l].astype(BF16),
        "ln2_g": p["ln2_g"][l][None],
        "w_router": wr.astype(BF16),
        "b_router": jnp.pad(p["b_router"][l], (0, ROUTER_PAD - N_EXPERTS),
                            constant_values=NEG_BIG)[None],
        "layer": l, "w_gu": p["w_gu"], "w_down": p["w_down"],
        "b_glu": p["b_gu"][l][:, None, 0::2], "b_lin": p["b_gu"][l][:, None, 1::2],
        "b_down": p["b_down"][l][:, None, :],
    }


def kernel(x_prompt, x_sample, cache_win_k, cache_win_v, state_conv, state_lru, ln1_g, w_in, g_q, g_k,
           g_vb, w_s, b_s, conv_w, conv_b, w_a, b_a, w_x, b_x, lru_lambda, g_out, w_out, ln2_g, w_router,
           b_router, w_gu, b_gu, w_down, b_down):
    params = dict(ln1_g=ln1_g, w_in=w_in, g_q=g_q, g_k=g_k, g_vb=g_vb, w_s=w_s, b_s=b_s, conv_w=conv_w,
                  conv_b=conv_b, w_a=w_a, b_a=b_a, w_x=w_x, b_x=b_x, lru_lambda=lru_lambda, g_out=g_out,
                  w_out=w_out, ln2_g=ln2_g, w_router=w_router, b_router=b_router, w_gu=w_gu, b_gu=b_gu,
                  w_down=w_down, b_down=b_down)
    bp, sp, _ = x_prompt.shape
    bs, ss, _ = x_sample.shape
    depth = w_in.shape[0]
    n_p, n_s = bp * sp, bs * ss
    keep = min(DILATED_PATTERNS[-1][0], sp)
    w_buf = cache_win_k.shape[2]
    ckt = cache_win_k.transpose(0, 1, 3, 4, 2).reshape(depth, bs, D_A, w_buf)
    cvt = cache_win_v.transpose(0, 1, 3, 4, 2).reshape(depth, bs, D_A, w_buf)
    x = (x_prompt.reshape(n_p, D_MODEL), x_sample.reshape(n_s, D_MODEL))
    zero_conv = jnp.zeros((bp, SUBLANES, D_C), F32)
    zero_h = jnp.zeros((bp, 1, D_C), F32)
    outs = {name: [] for name in ("pconv", "plru", "sk", "sv", "sconv", "slru", "svb")}
    window = None
    for l in range(depth):
        lw = _layer_weights(l, params)
        q, k, v, bc = _in_proj(x, lw["ln1_g"], lw["w_in"], lw["g_q"], lw["g_k"], lw["g_vb"], lw["ones_bd"])
        oa_p = _attn_prompt(q, k, v, bp, sp)
        ob_p, oc_p, h_p = _mixer_bc(bc, zero_conv, zero_h, lw, bp, sp, MIX_TILE, MIX_TILE - 1)
        oa_s = _attn_sample(q, k, v, ckt, cvt, l, n_p, bs, ss)
        bc_s = bc[n_p:].reshape(bs, ss, -1)
        bc_s_pad = jnp.pad(bc_s, ((0, 0), (0, CHUNK - ss), (0, 0))).reshape(bs * CHUNK, -1)
        conv8 = jnp.pad(state_conv[l], ((0, 0), (SUBLANES - (CONV_W - 1), 0), (0, 0)))
        ob_s, oc_s, h_s = _mixer_bc(bc_s_pad, conv8, state_lru[l][:, None, :], lw, bs, CHUNK, CHUNK, ss - 1)
        ob_s = ob_s.reshape(bs, CHUNK, D_B)[:, :ss].reshape(n_s, D_B)
        oc_s = oc_s.reshape(bs, CHUNK, D_C)[:, :ss].reshape(n_s, D_C)
        xn, h2, idx, gates, rank, counts = _out_proj((oa_p, oa_s), (ob_p, ob_s), (oc_p, oc_s), x, lw)
        x = _moe(xn, h2, idx, gates, rank, counts[:, 0, :N_EXPERTS], lw, n_p)

        window = _export_window(k, v, window, l, depth, bp, sp, keep)
        xc_p = bc[:n_p, 2 * D_B:2 * D_B + D_C].reshape(bp, sp, D_C)
        outs["pconv"].append(xc_p[:, sp - (CONV_W - 1):])
        outs["plru"].append(h_p[:, 0])
        outs["sk"].append(k[n_p:].reshape(bs, ss, N_HEADS_A, HEAD_DIM))
        outs["sv"].append(v[n_p:].reshape(bs, ss, N_HEADS_A, HEAD_DIM))
        xpad_s = jnp.concatenate([state_conv[l], bc_s[:, :, 2 * D_B:2 * D_B + D_C]], axis=1)
        outs["sconv"].append(xpad_s[:, -(CONV_W - 1):])
        outs["slru"].append(h_s[:, 0])
        outs["svb"].append(bc_s[:, :, D_B:2 * D_B])
    y_p = x[0].reshape(bp, sp, D_MODEL)
    y_s = x[1].reshape(bs, ss, D_MODEL)
    st = lambda name: jnp.stack(outs[name])
    heads_last = lambda t: t.reshape(depth, bp, N_HEADS_A, HEAD_DIM, keep).transpose(0, 1, 4, 2, 3)
    return (y_p, y_s, heads_last(window[0]), heads_last(window[1]), st("pconv"), st("plru"), st("sk"), st("sv"), st("sconv"),
            st("slru"), st("svb"))
```

```python
import functools

import jax
import jax.numpy as jnp
from jax import lax
from jax.experimental import pallas as pl
from jax.experimental.pallas import tpu as pltpu
from jax.experimental.pallas import tpu_sc as plsc

F32 = jnp.float32
BF16 = jnp.bfloat16

D_MODEL = 1024
HEAD_DIM = 64
N_HEADS_A = 8
D_A = N_HEADS_A * HEAD_DIM
N_HEADS_B = 4
D_B = N_HEADS_B * HEAD_DIM
N_GROUPS_C = 4
D_C = N_GROUPS_C * HEAD_DIM
D_IN = 3 * D_A + 2 * D_B + 2 * D_C
DILATED_PATTERNS = ((128, 1), (512, 4), (2048, 16))
N_PATTERNS = len(DILATED_PATTERNS)
CHUNK = 128
CONV_W = 4
LRU_C = 8.0
N_EXPERTS = 32
TOP_K = 4
D_FF = 1024
SWIGLU_LIMIT = 7.0
SWIGLU_ALPHA = 1.702
EPS = 1e-6
ATTN_SCALE = HEAD_DIM ** -0.5
PAST_LEN = 16384

LANES = 128
SUBLANES = 8
VMEM_LIMIT_BYTES = 56 * 1024 * 1024

Q_BLOCK = 128
ATTN_SPAN = 2048
ATTN_GROUP = 4
RAW_KEY_STRIDE = 16
TOKEN_TILE = 256
MIX_TILE = 512
MOE_TILE = 256
ROUTER_PAD = LANES
SC_SUBCORES = 16
SC_WORKERS = 2 * SC_SUBCORES
SC_GROUP = LANES // TOP_K
NEG_BIG = -1e30


def _cparams(semantics):
    return pltpu.CompilerParams(dimension_semantics=semantics,
                                vmem_limit_bytes=VMEM_LIMIT_BYTES)


def _full(shape):
    return pl.BlockSpec(shape, lambda *_: (0,) * len(shape))


def _rms(t, g):
    ms = jnp.mean(t * t, axis=-1, keepdims=True)
    return t * lax.rsqrt(ms + EPS) * g


def _pair_specs(tm, w, n_head):
    nh = n_head // tm
    return [pl.BlockSpec((tm, w), lambda i: (jnp.minimum(i, nh - 1), 0)),
            pl.BlockSpec((tm, w), lambda i: (jnp.maximum(i - nh, 0), 0))]


def _pick(head_ref, tail_ref, n_head_tiles):
    return jnp.where(pl.program_id(0) >= n_head_tiles, tail_ref[...], head_ref[...])


def _split_bf16(t):
    hi = t.astype(BF16)
    lo = (t - hi.astype(F32)).astype(BF16)
    return hi, lo


def _in_proj_kernel(xh_ref, xt_ref, g1_ref, w_ref, gq_ref, gk_ref, gvb_ref, ones_ref,
                    q_ref, k_ref, v_ref, bc_ref, *, n_head_tiles):
    h = _rms(_pick(xh_ref, xt_ref, n_head_tiles), g1_ref[...]).astype(BF16)
    z = jnp.dot(h, w_ref[...], preferred_element_type=F32)

    def head_norm(t, g):
        hi, lo = _split_bf16(t * t)
        ss = (jnp.dot(hi, ones_ref[...], preferred_element_type=F32)
              + jnp.dot(lo, ones_ref[...], preferred_element_type=F32))
        return t * lax.rsqrt(ss * (1.0 / HEAD_DIM) + EPS) * g

    q_ref[...] = head_norm(z[:, 0:D_A], gq_ref[...]) * ATTN_SCALE
    k_ref[...] = head_norm(z[:, D_A:2 * D_A], gk_ref[...])
    v_ref[...] = z[:, 2 * D_A:3 * D_A]
    o = 3 * D_A
    bc_ref[...] = z[:, o:]
    bc_ref[:, D_B:2 * D_B] = _rms(z[:, o + D_B:o + 2 * D_B], gvb_ref[...])


def _in_proj(x_pair, g1, w_bf16, gq, gk, gvb, ones_bd):
    n_head = x_pair[0].shape[0]
    n = n_head + x_pair[1].shape[0]
    tm = TOKEN_TILE
    row = lambda w: pl.BlockSpec((tm, w), lambda i: (i, 0))
    return pl.pallas_call(
        functools.partial(_in_proj_kernel, n_head_tiles=n_head // tm),
        grid=(n // tm,),
        in_specs=_pair_specs(tm, D_MODEL, n_head) + [_full((1, D_MODEL)), _full((D_MODEL, D_IN)), _full((1, D_A)),
                  _full((1, D_A)), _full((1, D_B)), _full((D_A, D_A))],
        out_specs=[row(D_A), row(D_A), row(D_A), row(2 * D_B + 2 * D_C)],
        out_shape=[jax.ShapeDtypeStruct((n, D_A), F32)] * 3
        + [jax.ShapeDtypeStruct((n, 2 * D_B + 2 * D_C), F32)],
        compiler_params=_cparams(("parallel",)),
        name="in_proj",
    )(*x_pair, g1, w_bf16, gq, gk, gvb, ones_bd)


def _attn_prompt_kernel(q_ref, kp_ref, kc_ref, vp_ref, vc_ref, o_ref, kq0a, kq0b, kq1a, kq1b, vqh, vql,
                        m_s, l_s, a_s):
    span = pl.program_id(2)
    qb2 = 2 * Q_BLOCK
    lane = lax.broadcasted_iota(jnp.int32, (Q_BLOCK, LANES), 1)
    head0 = lane < HEAD_DIM
    lane2 = lax.broadcasted_iota(jnp.int32, (qb2, LANES), 1)
    head0_2 = lane2 < HEAD_DIM
    swap = lambda t: pltpu.roll(t, HEAD_DIM, axis=1)

    def pack(k_ref, v_ref, base):
        def body(c, carry):
            src = pl.ds(pl.multiple_of(c * qb2, qb2), qb2)
            dst = pl.ds(pl.multiple_of(base + c * qb2, qb2), qb2)
            k = k_ref[src, :]
            kh = k.astype(BF16).astype(F32)
            kl = k - kh
            kq0a[dst, :] = jnp.where(head0_2, kh, swap(kh))
            kq0b[dst, :] = jnp.where(head0_2, kl, 0.0)
            kq1a[dst, :] = jnp.where(head0_2, swap(kh), kh)
            kq1b[dst, :] = jnp.where(head0_2, swap(kl), 0.0)
            v = v_ref[src, :]
            vh = v.astype(BF16).astype(F32)
            vqh[dst, :] = vh
            vql[dst, :] = v - vh
            return carry
        lax.fori_loop(0, ATTN_SPAN // qb2, body, 0)

    pack(kp_ref, vp_ref, 0)
    pack(kc_ref, vc_ref, ATTN_SPAN)

    row = lax.broadcasted_iota(jnp.int32, (qb2, qb2), 0) & (Q_BLOCK - 1)
    col = lax.broadcasted_iota(jnp.int32, (qb2, qb2), 1)
    band = (col >= row) & (col <= row + Q_BLOCK)
    cur = col >= Q_BLOCK
    nt = (((1,), (1,)), ((), ()))

    def ds(start, size, d):
        return pl.ds(start, size) if d == 1 else pl.ds(start, size, stride=d)

    for p, (_, d) in enumerate(DILATED_PATTERNS):
        nblk = ATTN_SPAN // (Q_BLOCK * d)
        assert d < RAW_KEY_STRIDE or nblk == 1

        def scores(blk, d=d, nblk=nblk):
            r = blk // nblk
            ib = blk % nblk
            qstart = r + d * Q_BLOCK * ib
            if d == 1:
                qstart = pl.multiple_of(Q_BLOCK * blk, Q_BLOCK)
            kstart = ATTN_SPAN + qstart - d * Q_BLOCK
            q = q_ref[ds(qstart, Q_BLOCK, d), :]
            qh = q.astype(BF16).astype(F32)
            ql = q - qh
            lhs0 = jnp.concatenate([jnp.where(head0, qh, swap(ql)), jnp.where(head0, qh, 0.0)], axis=1)
            lhs1 = jnp.concatenate([jnp.where(head0, swap(qh), ql), jnp.where(head0, swap(qh), 0.0)], axis=1)
            if d < RAW_KEY_STRIDE:
                keys = ds(kstart, qb2, d)
                k0 = jnp.concatenate([kq0a[keys, :], kq0b[keys, :]], axis=1).astype(BF16)
                k1 = jnp.concatenate([kq1a[keys, :], kq1b[keys, :]], axis=1).astype(BF16)
                vh, vl = vqh[keys, :], vql[keys, :]
            else:
                both = lambda p_ref, c_ref: jnp.concatenate(
                    [p_ref[ds(r, Q_BLOCK, d), :], c_ref[ds(r, Q_BLOCK, d), :]], axis=0)
                k = both(kp_ref, kc_ref)
                kh = k.astype(BF16).astype(F32)
                kl = k - kh
                k0 = jnp.concatenate([jnp.where(head0_2, kh, swap(kh)), jnp.where(head0_2, kl, 0.0)],
                                     axis=1).astype(BF16)
                k1 = jnp.concatenate([jnp.where(head0_2, swap(kh), kh), jnp.where(head0_2, swap(kl), 0.0)],
                                     axis=1).astype(BF16)
                v = both(vp_ref, vc_ref)
                vh = v.astype(BF16).astype(F32)
                vl = v - vh
            s0 = lax.dot_general(lhs0.astype(BF16), k0, nt, preferred_element_type=F32)
            s1 = lax.dot_general(lhs1.astype(BF16), k1, nt, preferred_element_type=F32)
            prev_ok = jnp.logical_or(ib > 0, span > 0)
            s = jnp.concatenate([s0, s1], axis=0)
            return jnp.where(band & (cur | prev_ok), s, -jnp.inf), qstart, (vh, vl)

        def softmax(s):
            m = jnp.max(s, axis=-1, keepdims=True)
            e = jnp.exp(s - m)
            return m, e, jnp.sum(e, axis=-1, keepdims=True)

        def weighted(e, v_parts):
            eh, el = _split_bf16(e)
            vh, vl = v_parts
            rhs = jnp.concatenate([jnp.concatenate([vh, vl], axis=1),
                                   jnp.concatenate([vh, jnp.zeros_like(vh)], axis=1)], axis=0).astype(BF16)
            out = jnp.dot(jnp.concatenate([eh, el], axis=1), rhs, preferred_element_type=F32)
            return out[:, :LANES] + out[:, LANES:]

        def store(qstart, m, l, acc, p=p, d=d):
            dst = ds(qstart, Q_BLOCK, d)
            shape = (Q_BLOCK, LANES)
            m_s[p, dst, :] = jnp.where(head0, jnp.broadcast_to(m[:Q_BLOCK], shape),
                                       jnp.broadcast_to(m[Q_BLOCK:], shape))
            l_s[p, dst, :] = jnp.where(head0, jnp.broadcast_to(l[:Q_BLOCK], shape),
                                       jnp.broadcast_to(l[Q_BLOCK:], shape))
            a_s[p, dst, :] = jnp.where(head0, acc[:Q_BLOCK], acc[Q_BLOCK:])

        def body(it, carry):
            sc = [scores(it * ATTN_GROUP + g) for g in range(ATTN_GROUP)]
            sm = [softmax(s) for s, _, _ in sc]
            ac = [weighted(e, v_parts) for (_, e, _), (_, _, v_parts) in zip(sm, sc)]
            for (_, qstart, _), (m, _, l), acc in zip(sc, sm, ac):
                store(qstart, m, l, acc)
            return carry

        lax.fori_loop(0, ATTN_SPAN // (Q_BLOCK * ATTN_GROUP), body, 0)

    def merge(c, carry):
        rows = pl.ds(pl.multiple_of(c * Q_BLOCK, Q_BLOCK), Q_BLOCK)
        ms = [m_s[p, rows, :] for p in range(N_PATTERNS)]
        m_all = functools.reduce(jnp.maximum, ms)
        ws = [jnp.exp(m - m_all) for m in ms]
        num = sum(w * a_s[p, rows, :] for p, w in enumerate(ws))
        den = sum(w * l_s[p, rows, :] for p, w in enumerate(ws))
        o_ref[rows, :] = num / den
        return carry

    lax.fori_loop(0, ATTN_SPAN // Q_BLOCK, merge, 0)


def _attn_prompt(q, k, v, batch, seq):
    nspan = seq // ATTN_SPAN
    blk = (ATTN_SPAN, LANES)
    cur = pl.BlockSpec(blk, lambda b, hp, s: (b * nspan + s, hp))
    prev = pl.BlockSpec(blk, lambda b, hp, s: (b * nspan + jnp.maximum(s - 1, 0), hp))
    acc = pltpu.VMEM((N_PATTERNS, ATTN_SPAN, LANES), F32)
    packed = pltpu.VMEM((2 * ATTN_SPAN, LANES), F32)
    return pl.pallas_call(
        _attn_prompt_kernel,
        grid=(batch, D_A // LANES, nspan),
        in_specs=[cur, prev, cur, prev, cur],
        out_specs=cur,
        out_shape=jax.ShapeDtypeStruct((batch * seq, D_A), F32),
        scratch_shapes=[packed] * 6 + [acc, acc, acc],
        compiler_params=_cparams(("parallel", "parallel", "arbitrary")),
        name="attn_prompt",
    )(q, k, k, v, v)


def _attn_sample_kernel(q_ref, kn_ref, vn_ref, ckt_ref, cvt_ref, o_ref, *, w_buf, t_new):
    pad = LANES - t_new
    zeros = jnp.zeros((pad, D_A), F32)
    kn = jnp.concatenate([kn_ref[...], zeros], axis=0).astype(BF16)
    vn = jnp.concatenate([vn_ref[...], zeros], axis=0).astype(BF16)
    n_rows = N_HEADS_A * t_new
    hrow = lax.broadcasted_iota(jnp.int32, (n_rows, D_A), 0) // t_new
    hlane = lax.broadcasted_iota(jnp.int32, (n_rows, D_A), 1) // HEAD_DIM
    own = hrow == hlane
    q_rep = jnp.concatenate([q_ref[...]] * N_HEADS_A, axis=0)
    q64 = jnp.where(own, q_rep, 0.0).astype(BF16)
    nt = (((1,), (1,)), ((), ()))
    s_w = jnp.dot(q64, ckt_ref[...].astype(BF16), preferred_element_type=F32)
    s_n = lax.dot_general(q64, kn, nt, preferred_element_type=F32)
    vt = cvt_ref[...].astype(BF16)

    def dist(n_cols, first):
        t = lax.broadcasted_iota(jnp.int32, (n_rows, n_cols), 0) % t_new
        return t - lax.broadcasted_iota(jnp.int32, (n_rows, n_cols), 1) - first

    dist_w, dist_n = dist(w_buf, -w_buf), dist(LANES, 0)
    ms, ls, accs = [], [], []
    for w, d in DILATED_PATTERNS:
        ok = lambda ds: (ds >= 0) & (ds <= w) & ((ds & (d - 1)) == 0)
        sw = jnp.where(ok(dist_w), s_w, -jnp.inf)
        sn = jnp.where(ok(dist_n), s_n, -jnp.inf)
        m = jnp.maximum(jnp.max(sw, axis=-1, keepdims=True), jnp.max(sn, axis=-1, keepdims=True))
        ew = jnp.exp(sw - m)
        en = jnp.exp(sn - m)
        ms.append(m)
        ls.append(jnp.sum(ew, axis=-1, keepdims=True) + jnp.sum(en, axis=-1, keepdims=True))
        accs.append(lax.dot_general(ew.astype(BF16), vt, nt, preferred_element_type=F32)
                    + jnp.dot(en.astype(BF16), vn, preferred_element_type=F32))
    m_all = functools.reduce(jnp.maximum, ms)
    ws = [jnp.exp(m - m_all) for m in ms]
    num = sum(w * a for w, a in zip(ws, accs))
    den = sum(w * l for w, l in zip(ws, ls))
    o = jnp.where(own, num / den, 0.0)
    out = o[0:t_new]
    for h in range(1, N_HEADS_A):
        out = out + o[h * t_new:(h + 1) * t_new]
    o_ref[...] = out


def _attn_sample(q, k, v, cache_kt, cache_vt, layer, row0, batch, t_new):
    w_buf = cache_kt.shape[3]
    assert row0 % t_new == 0 and t_new == SUBLANES
    new = pl.BlockSpec((t_new, D_A), lambda b: (row0 // t_new + b, 0))
    cache = pl.BlockSpec((None, None, D_A, w_buf), lambda b: (layer, b, 0, 0))
    return pl.pallas_call(
        functools.partial(_attn_sample_kernel, w_buf=w_buf, t_new=t_new),
        grid=(batch,),
        in_specs=[new, new, new, cache, cache],
        out_specs=pl.BlockSpec((t_new, D_A), lambda b: (b, 0)),
        out_shape=jax.ShapeDtypeStruct((batch * t_new, D_A), F32),
        compiler_params=_cparams(("parallel",)),
        name="attn_sample",
    )(q, k, v, cache_kt, cache_vt)


def _export_kernel(*refs):
    k_ref, v_ref = refs[0], refs[1]
    pk_ref, pv_ref = refs[-2], refs[-1]
    pk_ref[...] = k_ref[...].T
    pv_ref[...] = v_ref[...].T


def _export_window(k, v, prev, layer, depth, batch, seq, keep):
    tm = TOKEN_TILE
    first = (seq - keep) // tm
    src = pl.BlockSpec((tm, D_A), lambda b, j: (b * (seq // tm) + first + j, 0))
    dst = pl.BlockSpec((None, None, D_A, tm), lambda b, j: (layer, b, 0, j))
    shape = jax.ShapeDtypeStruct((depth, batch, D_A, keep), F32)
    carried = [] if prev is None else list(prev)
    return pl.pallas_call(
        _export_kernel,
        grid=(batch, keep // tm),
        in_specs=[src, src] + [pl.BlockSpec(memory_space=pl.ANY)] * len(carried),
        out_specs=[dst, dst],
        out_shape=[shape, shape],
        input_output_aliases={2 + i: i for i in range(len(carried))},
        compiler_params=_cparams(("parallel", "parallel")),
        name="export_window",
    )(k, v, *carried)


def _gelu_tanh(x):
    return 0.5 * x * (1.0 + jnp.tanh(0.7978845608028654 * (x + 0.044715 * x * x * x)))


def _mixer_bc_kernel(bc_ref, cb_ref, h0_ref, ws_ref, bs_ref, cw_ref, cbias_ref, wa_ref, ba_ref,
                     wx_ref, bx_ref, lam_ref, ob_ref, oc_ref, hl_ref, xp_s, h_s, *, tt, last_row):
    j = pl.program_id(1)

    @pl.when(j == 0)
    def _():
        xp_s[0:SUBLANES, :] = cb_ref[...]
        h_s[...] = h0_ref[...]

    nch = tt // CHUNK
    vcat = jnp.concatenate([bc_ref[c * CHUNK:(c + 1) * CHUNK, D_B:2 * D_B] for c in range(nch)],
                           axis=1).astype(BF16)
    ri = lax.broadcasted_iota(jnp.int32, (CHUNK, CHUNK), 0)
    ci = lax.broadcasted_iota(jnp.int32, (CHUNK, CHUNK), 1)
    hl = (lax.broadcasted_iota(jnp.int32, (CHUNK, nch * D_B), 1) % D_B) // HEAD_DIM
    mixed = jnp.zeros((CHUNK, nch * D_B), F32)
    for h in range(N_HEADS_B):
        wh = jnp.where(ri >= ci, ws_ref[h], 0.0).astype(BF16)
        mh = jnp.dot(wh, vcat, preferred_element_type=F32)
        mixed = mixed + jnp.where(hl == h, mh, 0.0)
    for c in range(nch):
        rows = slice(c * CHUNK, (c + 1) * CHUNK)
        ob_ref[rows, :] = bc_ref[rows, 0:D_B] * (mixed[:, c * D_B:(c + 1) * D_B] + bs_ref[...])

    xc = bc_ref[:, 2 * D_B:2 * D_B + D_C]
    xp_s[SUBLANES:SUBLANES + tt, :] = xc
    xconv = cbias_ref[...] + cw_ref[CONV_W - 1:CONV_W, :] * xc
    for kk in range(CONV_W - 1):
        off = SUBLANES - (CONV_W - 1) + kk
        xconv = xconv + cw_ref[kk:kk + 1, :] * xp_s[off:off + tt, :]
    xp_s[0:SUBLANES, :] = xp_s[tt:tt + SUBLANES, :]
    xb = xconv.astype(BF16)
    r = jax.nn.sigmoid(jnp.dot(xb, wa_ref[...], preferred_element_type=F32) + ba_ref[...])
    i = jax.nn.sigmoid(jnp.dot(xb, wx_ref[...], preferred_element_type=F32) + bx_ref[...])
    nl = -lam_ref[...]
    softplus = jnp.maximum(nl, 0.0) + jnp.log1p(jnp.exp(-jnp.abs(nl)))
    a = jnp.exp(-LRU_C * r * softplus)
    b = jnp.sqrt(1.0 - a * a) * (i * xconv)
    rowi = lax.broadcasted_iota(jnp.int32, (tt, D_C), 0)
    step = 1
    while step < tt:
        a_sh = pltpu.roll(a, step, axis=0)
        b_sh = pltpu.roll(b, step, axis=0)
        live = rowi >= step
        b = jnp.where(live, a * b_sh + b, b)
        a = jnp.where(live, a * a_sh, a)
        step *= 2
    h = a * h_s[...] + b
    h_s[...] = h[tt - 1:tt, :]
    oc_ref[...] = h * _gelu_tanh(bc_ref[:, 2 * D_B + D_C:])

    @pl.when(j == pl.num_programs(1) - 1)
    def _():
        hl_ref[...] = h[last_row:last_row + 1, :]


def _mixer_bc(bc, conv_buf8, h0, lw, batch, t_len, tt, last_row):
    nt = t_len // tt
    rows = lambda w: pl.BlockSpec((tt, w), lambda b, j: (b * nt + j, 0))
    per_b = lambda s: pl.BlockSpec((None,) + s, lambda b, j: (b,) + (0,) * len(s))
    return pl.pallas_call(
        functools.partial(_mixer_bc_kernel, tt=tt, last_row=last_row),
        grid=(batch, nt),
        in_specs=[rows(2 * D_B + 2 * D_C), per_b((SUBLANES, D_C)), per_b((1, D_C)),
                  _full((N_HEADS_B, CHUNK, CHUNK)), _full((CHUNK, D_B)), _full((CONV_W, D_C)),
                  _full((1, D_C)), _full((D_C, D_C)), _full((1, D_C)), _full((D_C, D_C)),
                  _full((1, D_C)), _full((1, D_C))],
        out_specs=[rows(D_B), rows(D_C), per_b((1, D_C))],
        out_shape=[jax.ShapeDtypeStruct((batch * t_len, D_B), F32),
                   jax.ShapeDtypeStruct((batch * t_len, D_C), F32),
                   jax.ShapeDtypeStruct((batch, 1, D_C), F32)],
        scratch_shapes=[pltpu.VMEM((tt + SUBLANES, D_C), F32), pltpu.VMEM((1, D_C), F32)],
        compiler_params=_cparams(("parallel", "arbitrary")),
        name="mixer_bc",
    )(bc, conv_buf8, h0, lw["w_s"], lw["bias_s"], lw["conv_w"], lw["conv_b"], lw["w_a_bd"],
      lw["b_a"], lw["w_x_bd"], lw["b_x"], lw["lru_lambda"])


def _out_proj_kernel(oah_ref, oat_ref, obh_ref, obt_ref, och_ref, oct_ref, xh_ref, xt_ref, go_ref, wo_ref,
                     g2_ref, wr_ref, br_ref, xn_ref, h2_ref, idx_ref, gate_ref, rank_ref, cnt_ref, *,
                     n_head_tiles):
    pick = functools.partial(_pick, n_head_tiles=n_head_tiles)
    oa = _rms(pick(oah_ref, oat_ref), go_ref[:, 0:D_A]).astype(BF16)
    ob = _rms(pick(obh_ref, obt_ref), go_ref[:, D_A:D_A + D_B]).astype(BF16)
    oc = _rms(pick(och_ref, oct_ref), go_ref[:, D_A + D_B:]).astype(BF16)
    y = (jnp.dot(oa, wo_ref[0:D_A, :], preferred_element_type=F32)
         + jnp.dot(ob, wo_ref[D_A:D_A + D_B, :], preferred_element_type=F32)
         + jnp.dot(oc, wo_ref[D_A + D_B:, :], preferred_element_type=F32))
    xn = pick(xh_ref, xt_ref) + y
    xn_ref[...] = xn
    hb = _rms(xn, g2_ref[...]).astype(BF16)
    bits = pltpu.bitcast(hb.astype(F32), jnp.uint32)
    half = D_MODEL // 2
    h2_ref[...] = (bits[:, :half] >> 16) | (bits[:, half:] & jnp.uint32(0xFFFF0000))
    logits = jnp.dot(hb, wr_ref[...], preferred_element_type=F32) + br_ref[...]
    lane = lax.broadcasted_iota(jnp.int32, logits.shape, 1).astype(F32)
    cur = logits
    tops, idxs = [], []
    for _ in range(TOP_K):
        m = jnp.max(cur, axis=-1, keepdims=True)
        ix = jnp.min(jnp.where(cur == m, lane, float(ROUTER_PAD)), axis=-1, keepdims=True)
        tops.append(m)
        idxs.append(ix)
        cur = jnp.where(lane == ix, -jnp.inf, cur)
    es = [jnp.exp(t - tops[0]) for t in tops]
    den = sum(es)
    idx_out = jnp.zeros(logits.shape, F32)
    gate_out = jnp.zeros(logits.shape, F32)
    for kk in range(TOP_K):
        idx_out = jnp.where(lane == kk, idxs[kk], idx_out)
        gate_out = jnp.where(lane == kk, es[kk] / den, gate_out)
    idx_ref[...] = idx_out.astype(jnp.int32)
    gate_ref[...] = gate_out
    tm = logits.shape[0]
    chosen = jnp.zeros(logits.shape, F32)
    for kk in range(TOP_K):
        chosen = jnp.where(lane == idxs[kk], 1.0, chosen)
    below = (lax.broadcasted_iota(jnp.int32, (tm, tm), 0)
             > lax.broadcasted_iota(jnp.int32, (tm, tm), 1)).astype(BF16)
    earlier = jnp.dot(below, chosen.astype(BF16), preferred_element_type=F32)
    rank_out = jnp.zeros(logits.shape, F32)
    for kk in range(TOP_K):
        rk = jnp.sum(jnp.where(lane == idxs[kk], earlier, 0.0), axis=-1, keepdims=True)
        rank_out = jnp.where(lane == kk, rk, rank_out)
    rank_ref[...] = rank_out.astype(jnp.int32)
    cnt_ref[...] = jnp.broadcast_to(jnp.sum(chosen, axis=0, keepdims=True),
                                    cnt_ref.shape).astype(jnp.int32)


def _out_proj(oa_pair, ob_pair, oc_pair, x_pair, lw):
    n_head = x_pair[0].shape[0]
    n = n_head + x_pair[1].shape[0]
    tm = TOKEN_TILE
    row = lambda w: pl.BlockSpec((tm, w), lambda i: (i, 0))
    pair = lambda w: _pair_specs(tm, w, n_head)
    return pl.pallas_call(
        functools.partial(_out_proj_kernel, n_head_tiles=n_head // tm),
        grid=(n // tm,),
        in_specs=pair(D_A) + pair(D_B) + pair(D_C) + pair(D_MODEL) + [_full((1, D_MODEL)),
                  _full((D_MODEL, D_MODEL)), _full((1, D_MODEL)), _full((D_MODEL, ROUTER_PAD)),
                  _full((1, ROUTER_PAD))],
        out_specs=[row(D_MODEL), row(D_MODEL // 2), row(ROUTER_PAD), row(ROUTER_PAD), row(ROUTER_PAD),
                   pl.BlockSpec((None, SUBLANES, ROUTER_PAD), lambda i: (i, 0, 0))],
        out_shape=[jax.ShapeDtypeStruct((n, D_MODEL), F32), jax.ShapeDtypeStruct((n, D_MODEL // 2), jnp.uint32),
                   jax.ShapeDtypeStruct((n, ROUTER_PAD), jnp.int32),
                   jax.ShapeDtypeStruct((n, ROUTER_PAD), F32),
                   jax.ShapeDtypeStruct((n, ROUTER_PAD), jnp.int32),
                   jax.ShapeDtypeStruct((n // tm, SUBLANES, ROUTER_PAD), jnp.int32)],
        compiler_params=_cparams(("parallel",)),
        name="out_proj",
    )(*oa_pair, *ob_pair, *oc_pair, *x_pair, lw["g_out"], lw["w_out"], lw["ln2_g"], lw["w_router"],
      lw["b_router"])


def _moe_kernel(te_ref, nu_ref, x_ref, wgu_ref, bg_ref, bl_ref, wd_ref, bd_ref, y_ref, wg_s, wl_s, wd_s):
    i = pl.program_id(0)
    used = i < nu_ref[0]
    new_expert = jnp.logical_or(i == 0, te_ref[i] != te_ref[jnp.maximum(i - 1, 0)])

    @pl.when(jnp.logical_and(used, new_expert))
    def _():
        w2 = 2 * LANES
        src = lax.broadcasted_iota(jnp.int32, (w2, w2), 0)
        col = lax.broadcasted_iota(jnp.int32, (w2, w2), 1)
        pick = (src == jnp.where(col < LANES, 2 * col, 2 * (col - LANES) + 1)).astype(BF16)
        for c in range(2 * D_FF // w2):
            out = jnp.dot(wgu_ref[:, c * w2:(c + 1) * w2].astype(BF16), pick, preferred_element_type=F32)
            wg_s[:, c * LANES:(c + 1) * LANES] = out[:, :LANES].astype(BF16)
            wl_s[:, c * LANES:(c + 1) * LANES] = out[:, LANES:].astype(BF16)
        wd_s[...] = wd_ref[...].astype(BF16)

    @pl.when(used)
    def _():
        u = x_ref[...]
        x = jnp.concatenate([pltpu.bitcast(u << 16, F32).astype(BF16),
                             pltpu.bitcast(u & jnp.uint32(0xFFFF0000), F32).astype(BF16)], axis=1)
        zg = jnp.dot(x, wg_s[...], preferred_element_type=F32) + bg_ref[...]
        zl = jnp.dot(x, wl_s[...], preferred_element_type=F32) + bl_ref[...]
        glu = jnp.minimum(zg, SWIGLU_LIMIT)
        lin = jnp.clip(zl, -SWIGLU_LIMIT, SWIGLU_LIMIT)
        act = glu * jax.nn.sigmoid(SWIGLU_ALPHA * glu) * (lin + 1.0)
        y_ref[...] = jnp.dot(act.astype(BF16), wd_s[...], preferred_element_type=F32) + bd_ref[...]

    @pl.when(i >= nu_ref[0])
    def _():
        y_ref[...] = jnp.zeros(y_ref.shape, F32)


def _moe_ffn(xs, tile_expert, n_used, lw):
    n_slots = xs.shape[0]
    tm = MOE_TILE
    layer = lw["layer"]
    wspec = lambda r, c: pl.BlockSpec((None, r, c), lambda i, te, nu: (te[i], 0, 0))
    wfull = lambda r, c: pl.BlockSpec((None, None, r, c), lambda i, te, nu: (layer, te[i], 0, 0))
    return pl.pallas_call(
        _moe_kernel,
        grid_spec=pltpu.PrefetchScalarGridSpec(
            num_scalar_prefetch=2,
            grid=(n_slots // tm,),
            in_specs=[pl.BlockSpec((tm, D_MODEL // 2), lambda i, te, nu: (i, 0)),
                      wfull(D_MODEL, 2 * D_FF), wspec(1, D_FF), wspec(1, D_FF),
                      wfull(D_FF, D_MODEL), wspec(1, D_MODEL)],
            out_specs=pl.BlockSpec((tm, D_MODEL), lambda i, te, nu: (i, 0)),
            scratch_shapes=[pltpu.VMEM((D_MODEL, D_FF), BF16), pltpu.VMEM((D_MODEL, D_FF), BF16),
                            pltpu.VMEM((D_FF, D_MODEL), BF16)],
        ),
        out_shape=jax.ShapeDtypeStruct((n_slots, D_MODEL), F32),
        compiler_params=_cparams(("arbitrary",)),
        name="moe_ffn",
    )(tile_expert, n_used, xs, lw["w_gu"], lw["b_glu"], lw["b_lin"], lw["w_down"], lw["b_down"])


def _route(idx, rank, counts):
    n_tok = idx.shape[0]
    experts = jnp.arange(N_EXPERTS, dtype=jnp.int32)
    totals = jnp.sum(counts, axis=0)
    tiles = (totals + MOE_TILE - 1) // MOE_TILE
    tile_end = jnp.sum(jnp.where(experts[None, :] <= experts[:, None], tiles[None, :], 0), axis=1)
    pad_start = (tile_end - tiles) * MOE_TILE
    tt = jnp.arange(counts.shape[0], dtype=jnp.int32)
    before = jnp.sum(jnp.where((tt[None, :] < tt[:, None])[:, :, None], counts[None, :, :], 0), axis=1)
    base = pad_start[None, :] + before
    base_tok = jnp.repeat(base, TOKEN_TILE, axis=0)
    picked = idx[:, :, None] == experts[None, None, :]
    dest = jnp.sum(jnp.where(picked, base_tok[:, None, :], 0), axis=-1) + rank
    n_tiles = -(-(n_tok * TOP_K) // MOE_TILE) + N_EXPERTS
    tile_ids = jnp.arange(n_tiles, dtype=jnp.int32)
    tile_expert = jnp.minimum(jnp.sum((tile_end[None, :] <= tile_ids[:, None]).astype(jnp.int32), axis=1),
                              N_EXPERTS - 1)
    n_used = tile_end[-1:].astype(jnp.int32)
    return dest.astype(jnp.int32), tile_expert, n_used, n_tiles


def _sc_mesh():
    return plsc.VectorSubcoreMesh(core_axis_name="core", subcore_axis_name="subcore")


def _sc_dispatch(h2, dest_g, n_slots):
    ng = dest_g.shape[0]
    width = h2.shape[1]

    @pl.kernel(out_type=jax.ShapeDtypeStruct((n_slots, width), h2.dtype), mesh=_sc_mesh(),
               scratch_types=[pltpu.VMEM((1, LANES), jnp.int32), pltpu.VMEM((SC_GROUP, width), h2.dtype)]
               + [pltpu.SemaphoreType.DMA] * TOP_K)
    def kernel(h_hbm, d_hbm, o_hbm, i_vmem, buf, *sems):
        wid = lax.axis_index("core") * SC_SUBCORES + lax.axis_index("subcore")

        @pl.loop(0, pl.cdiv(ng, SC_WORKERS))
        def _(r):
            g = wid + SC_WORKERS * r

            @pl.when(g < ng)
            def _():
                pltpu.sync_copy(d_hbm.at[pl.ds(g, 1)], i_vmem)
                pltpu.sync_copy(h_hbm.at[pl.ds(g * SC_GROUP, SC_GROUP)], buf)
                puts = [pltpu.async_copy(buf, o_hbm.at[i_vmem.at[0, pl.ds(k * SC_GROUP, SC_GROUP)]], sems[k])
                        for k in range(TOP_K)]
                for put in puts:
                    put.wait()

    return kernel(h2, dest_g)


def _sc_collect(y, dest_g):
    ng = dest_g.shape[0]

    @pl.kernel(out_type=jax.ShapeDtypeStruct((ng * LANES, D_MODEL), F32), mesh=_sc_mesh(),
               scratch_types=[pltpu.VMEM((1, LANES), jnp.int32)] + [pltpu.VMEM((SC_GROUP, D_MODEL), F32)] * 2
               + [pltpu.SemaphoreType.DMA] * 4)
    def kernel(y_hbm, d_hbm, o_hbm, i_vmem, buf0, buf1, g0, g1, w0, w1):
        wid = lax.axis_index("core") * SC_SUBCORES + lax.axis_index("subcore")
        bufs, gsem, wsem = (buf0, buf1), (g0, g1), (w0, w1)

        @pl.loop(0, pl.cdiv(ng, SC_WORKERS))
        def _(r):
            g = wid + SC_WORKERS * r

            @pl.when(g < ng)
            def _():
                pltpu.sync_copy(d_hbm.at[pl.ds(g, 1)], i_vmem)
                get = lambda k: pltpu.async_copy(
                    y_hbm.at[i_vmem.at[0, pl.ds(k * SC_GROUP, SC_GROUP)]], bufs[k % 2], gsem[k % 2])
                put = lambda k: pltpu.async_copy(
                    bufs[k % 2], o_hbm.at[pl.ds(g * LANES + k * SC_GROUP, SC_GROUP)], wsem[k % 2])
                gets = [get(0), get(1)]
                puts = []
                for k in range(TOP_K):
                    gets[k].wait()
                    puts.append(put(k))
                    if k + 2 < TOP_K:
                        puts[k].wait()
                        gets.append(get(k + 2))
                for k in range(TOP_K - 2, TOP_K):
                    puts[k].wait()

    return kernel(y, dest_g)


def _combine_kernel(xn_ref, g_ref, y_ref, oh_ref, ot_ref, *, n_head_tiles):
    def emit(o_ref):
        for gi in range(TOKEN_TILE // SC_GROUP):
            rows = slice(gi * SC_GROUP, (gi + 1) * SC_GROUP)
            acc = xn_ref[rows, :]
            for k in range(TOP_K):
                r0 = gi * LANES + k * SC_GROUP
                acc = acc + g_ref[rows, k:k + 1] * y_ref[r0:r0 + SC_GROUP, :]
            o_ref[rows, :] = acc

    pl.when(pl.program_id(0) < n_head_tiles)(lambda: emit(oh_ref))
    pl.when(pl.program_id(0) >= n_head_tiles)(lambda: emit(ot_ref))


def _combine(xn, gates, y4, n_head):
    n = xn.shape[0]
    tm = TOKEN_TILE
    row = lambda r, w: pl.BlockSpec((r, w), lambda i: (i, 0))
    return pl.pallas_call(
        functools.partial(_combine_kernel, n_head_tiles=n_head // tm),
        grid=(n // tm,),
        in_specs=[row(tm, D_MODEL), row(tm, ROUTER_PAD), row(tm * TOP_K, D_MODEL)],
        out_specs=_pair_specs(tm, D_MODEL, n_head),
        out_shape=[jax.ShapeDtypeStruct((n_head, D_MODEL), F32),
                   jax.ShapeDtypeStruct((n - n_head, D_MODEL), F32)],
        compiler_params=_cparams(("arbitrary",)),
        name="moe_combine",
    )(xn, gates, y4)


def _moe(xn, h2, idx, gates, rank, counts, lw, n_head):
    n_tok = xn.shape[0]
    dest, tile_expert, n_used, n_tiles = _route(idx[:, :TOP_K], rank[:, :TOP_K], counts)
    dest_g = dest.reshape(n_tok // SC_GROUP, SC_GROUP, TOP_K).transpose(0, 2, 1).reshape(-1, LANES)
    xs = _sc_dispatch(h2, dest_g, n_tiles * MOE_TILE)
    y = _moe_ffn(xs, tile_expert, n_used, lw)
    return _combine(xn, gates, _sc_collect(y, dest_g), n_head)


def _block_diag(w):
    g, a, b = w.shape
    out = jnp.zeros((g * a, g * b), w.dtype)
    for i in range(g):
        out = out.at[i * a:(i + 1) * a, i * b:(i + 1) * b].set(w[i])
    return out


def _layer_weights(l, p):
    wr = jnp.pad(p["w_router"][l], ((0, 0), (0, ROUTER_PAD - N_EXPERTS)))
    hd = jnp.arange(D_A) // HEAD_DIM
    return {
        "ln1_g": p["ln1_g"][l][None], "w_in": p["w_in"][l].astype(BF16),
        "g_q": jnp.tile(p["g_q"][l], N_HEADS_A)[None], "g_k": jnp.tile(p["g_k"][l], N_HEADS_A)[None],
        "g_vb": p["g_vb"][l][None],
        "ones_bd": (hd[:, None] == hd[None, :]).astype(BF16),
        "w_s": p["w_s"][l],
        "bias_s": jnp.repeat(p["b_s"][l].T, HEAD_DIM, axis=1),
        "conv_w": p["conv_w"][l], "conv_b": p["conv_b"][l][None],
        "w_a_bd": _block_diag(p["w_a"][l]).astype(BF16), "b_a": p["b_a"][l][None],
        "w_x_bd": _block_diag(p["w_x"][l]).astype(BF16), "b_x": p["b_x"][l][None],
        "lru_lambda": p["lru_lambda"][l][None],
        "g_out": p["g_out"][l][None], "w_out": p["w_out"][l].astype(BF16),
        "ln2_g": p["ln2_g"][l][None],
        "w_router": wr.astype(BF16),
        "b_router": jnp.pad(p["b_router"][l], (0, ROUTER_PAD - N_EXPERTS),
                            constant_values=NEG_BIG)[None],
        "layer": l, "w_gu": p["w_gu"], "w_down": p["w_down"],
        "b_glu": p["b_gu"][l][:, None, 0::2], "b_lin": p["b_gu"][l][:, None, 1::2],
        "b_down": p["b_down"][l][:, None, :],
    }


def kernel(x_prompt, x_sample, cache_win_k, cache_win_v, state_conv, state_lru, ln1_g, w_in, g_q, g_k,
           g_vb, w_s, b_s, conv_w, conv_b, w_a, b_a, w_x, b_x, lru_lambda, g_out, w_out, ln2_g, w_router,
           b_router, w_gu, b_gu, w_down, b_down):
    params = dict(ln1_g=ln1_g, w_in=w_in, g_q=g_q, g_k=g_k, g_vb=g_vb, w_s=w_s, b_s=b_s, conv_w=conv_w,
                  conv_b=conv_b, w_a=w_a, b_a=b_a, w_x=w_x, b_x=b_x, lru_lambda=lru_lambda, g_out=g_out,
                  w_out=w_out, ln2_g=ln2_g, w_router=w_router, b_router=b_router, w_gu=w_gu, b_gu=b_gu,
                  w_down=w_down, b_down=b_down)
    bp, sp, _ = x_prompt.shape
    bs, ss, _ = x_sample.shape
    depth = w_in.shape[0]
    n_p, n_s = bp * sp, bs * ss
    keep = min(DILATED_PATTERNS[-1][0], sp)
    w_buf = cache_win_k.shape[2]
    ckt = cache_win_k.transpose(0, 1, 3, 4, 2).reshape(depth, bs, D_A, w_buf)
    cvt = cache_win_v.transpose(0, 1, 3, 4, 2).reshape(depth, bs, D_A, w_buf)
    x = (x_prompt.reshape(n_p, D_MODEL), x_sample.reshape(n_s, D_MODEL))
    zero_conv = jnp.zeros((bp, SUBLANES, D_C), F32)
    zero_h = jnp.zeros((bp, 1, D_C), F32)
    outs = {name: [] for name in ("pconv", "plru", "sk", "sv", "sconv", "slru", "svb")}
    window = None
    for l in range(depth):
        lw = _layer_weights(l, params)
        q, k, v, bc = _in_proj(x, lw["ln1_g"], lw["w_in"], lw["g_q"], lw["g_k"], lw["g_vb"], lw["ones_bd"])
        oa_p = _attn_prompt(q, k, v, bp, sp)
        ob_p, oc_p, h_p = _mixer_bc(bc, zero_conv, zero_h, lw, bp, sp, MIX_TILE, MIX_TILE - 1)
        oa_s = _attn_sample(q, k, v, ckt, cvt, l, n_p, bs, ss)
        bc_s = bc[n_p:].reshape(bs, ss, -1)
        bc_s_pad = jnp.pad(bc_s, ((0, 0), (0, CHUNK - ss), (0, 0))).reshape(bs * CHUNK, -1)
        conv8 = jnp.pad(state_conv[l], ((0, 0), (SUBLANES - (CONV_W - 1), 0), (0, 0)))
        ob_s, oc_s, h_s = _mixer_bc(bc_s_pad, conv8, state_lru[l][:, None, :], lw, bs, CHUNK, CHUNK, ss - 1)
        ob_s = ob_s.reshape(bs, CHUNK, D_B)[:, :ss].reshape(n_s, D_B)
        oc_s = oc_s.reshape(bs, CHUNK, D_C)[:, :ss].reshape(n_s, D_C)
        xn, h2, idx, gates, rank, counts = _out_proj((oa_p, oa_s), (ob_p, ob_s), (oc_p, oc_s), x, lw)
        x = _moe(xn, h2, idx, gates, rank, counts[:, 0, :N_EXPERTS], lw, n_p)

        window = _export_window(k, v, window, l, depth, bp, sp, keep)
        xc_p = bc[:n_p, 2 * D_B:2 * D_B + D_C].reshape(bp, sp, D_C)
        outs["pconv"].append(xc_p[:, sp - (CONV_W - 1):])
        outs["plru"].append(h_p[:, 0])
        outs["sk"].append(k[n_p:].reshape(bs, ss, N_HEADS_A, HEAD_DIM))
        outs["sv"].append(v[n_p:].reshape(bs, ss, N_HEADS_A, HEAD_DIM))
        xpad_s = jnp.concatenate([state_conv[l], bc_s[:, :, 2 * D_B:2 * D_B + D_C]], axis=1)
        outs["sconv"].append(xpad_s[:, -(CONV_W - 1):])
        outs["slru"].append(h_s[:, 0])
        outs["svb"].append(bc_s[:, :, D_B:2 * D_B])
    y_p = x[0].reshape(bp, sp, D_MODEL)
    y_s = x[1].reshape(bs, ss, D_MODEL)
    st = lambda name: jnp.stack(outs[name])
    heads_last = lambda t: t.reshape(depth, bp, N_HEADS_A, HEAD_DIM, keep).transpose(0, 1, 4, 2, 3)
    return (y_p, y_s, heads_last(window[0]), heads_last(window[1]), st("pconv"), st("plru"), st("sk"), st("sv"), st("sconv"),
            st("slru"), st("svb"))
```

```python
import functools

import jax
import jax.numpy as jnp
from jax import lax
from jax.experimental import pallas as pl
from jax.experimental.pallas import tpu as pltpu
from jax.experimental.pallas import tpu_sc as plsc

F32 = jnp.float32
BF16 = jnp.bfloat16

D_MODEL = 1024
HEAD_DIM = 64
N_HEADS_A = 8
D_A = N_HEADS_A * HEAD_DIM
N_HEADS_B = 4
D_B = N_HEADS_B * HEAD_DIM
N_GROUPS_C = 4
D_C = N_GROUPS_C * HEAD_DIM
D_IN = 3 * D_A + 2 * D_B + 2 * D_C
DILATED_PATTERNS = ((128, 1), (512, 4), (2048, 16))
N_PATTERNS = len(DILATED_PATTERNS)
CHUNK = 128
CONV_W = 4
LRU_C = 8.0
N_EXPERTS = 32
TOP_K = 4
D_FF = 1024
SWIGLU_LIMIT = 7.0
SWIGLU_ALPHA = 1.702
EPS = 1e-6
ATTN_SCALE = HEAD_DIM ** -0.5
PAST_LEN = 16384

LANES = 128
SUBLANES = 8
VMEM_LIMIT_BYTES = 56 * 1024 * 1024

Q_BLOCK = 128
ATTN_SPAN = 2048
ATTN_GROUP = 4
RAW_KEY_STRIDE = 16
TOKEN_TILE = 256
MIX_TILE = 512
MOE_TILE = 256
ROUTER_PAD = LANES
SC_SUBCORES = 16
SC_WORKERS = 2 * SC_SUBCORES
SC_GROUP = LANES // TOP_K
NEG_BIG = -1e30


def _cparams(semantics):
    return pltpu.CompilerParams(dimension_semantics=semantics,
                                vmem_limit_bytes=VMEM_LIMIT_BYTES)


def _full(shape):
    return pl.BlockSpec(shape, lambda *_: (0,) * len(shape))


def _rms(t, g):
    ms = jnp.mean(t * t, axis=-1, keepdims=True)
    return t * lax.rsqrt(ms + EPS) * g


def _pair_specs(tm, w, n_head):
    nh = n_head // tm
    return [pl.BlockSpec((tm, w), lambda i: (jnp.minimum(i, nh - 1), 0)),
            pl.BlockSpec((tm, w), lambda i: (jnp.maximum(i - nh, 0), 0))]


def _pick(head_ref, tail_ref, n_head_tiles):
    return jnp.where(pl.program_id(0) >= n_head_tiles, tail_ref[...], head_ref[...])


def _split_bf16(t):
    hi = t.astype(BF16)
    lo = (t - hi.astype(F32)).astype(BF16)
    return hi, lo


def _in_proj_kernel(xh_ref, xt_ref, g1_ref, w_ref, gq_ref, gk_ref, gvb_ref, ones_ref,
                    q_ref, k_ref, v_ref, bc_ref, *, n_head_tiles):
    h = _rms(_pick(xh_ref, xt_ref, n_head_tiles), g1_ref[...]).astype(BF16)
    z = jnp.dot(h, w_ref[...], preferred_element_type=F32)

    def head_norm(t, g):
        hi, lo = _split_bf16(t * t)
        ss = (jnp.dot(hi, ones_ref[...], preferred_element_type=F32)
              + jnp.dot(lo, ones_ref[...], preferred_element_type=F32))
        return t * lax.rsqrt(ss * (1.0 / HEAD_DIM) + EPS) * g

    q_ref[...] = head_norm(z[:, 0:D_A], gq_ref[...]) * ATTN_SCALE
    k_ref[...] = head_norm(z[:, D_A:2 * D_A], gk_ref[...])
    v_ref[...] = z[:, 2 * D_A:3 * D_A]
    o = 3 * D_A
    bc_ref[...] = z[:, o:]
    bc_ref[:, D_B:2 * D_B] = _rms(z[:, o + D_B:o + 2 * D_B], gvb_ref[...])


def _in_proj(x_pair, g1, w_bf16, gq, gk, gvb, ones_bd):
    n_head = x_pair[0].shape[0]
    n = n_head + x_pair[1].shape[0]
    tm = TOKEN_TILE
    row = lambda w: pl.BlockSpec((tm, w), lambda i: (i, 0))
    return pl.pallas_call(
        functools.partial(_in_proj_kernel, n_head_tiles=n_head // tm),
        grid=(n // tm,),
        in_specs=_pair_specs(tm, D_MODEL, n_head) + [_full((1, D_MODEL)), _full((D_MODEL, D_IN)), _full((1, D_A)),
                  _full((1, D_A)), _full((1, D_B)), _full((D_A, D_A))],
        out_specs=[row(D_A), row(D_A), row(D_A), row(2 * D_B + 2 * D_C)],
        out_shape=[jax.ShapeDtypeStruct((n, D_A), F32)] * 3
        + [jax.ShapeDtypeStruct((n, 2 * D_B + 2 * D_C), F32)],
        compiler_params=_cparams(("parallel",)),
        name="in_proj",
    )(*x_pair, g1, w_bf16, gq, gk, gvb, ones_bd)


def _attn_prompt_kernel(q_ref, kp_ref, kc_ref, vp_ref, vc_ref, o_ref, kq0a, kq0b, kq1a, kq1b, vqh, vql,
                        m_s, l_s, a_s):
    span = pl.program_id(2)
    qb2 = 2 * Q_BLOCK
    lane = lax.broadcasted_iota(jnp.int32, (Q_BLOCK, LANES), 1)
    head0 = lane < HEAD_DIM
    lane2 = lax.broadcasted_iota(jnp.int32, (qb2, LANES), 1)
    head0_2 = lane2 < HEAD_DIM
    swap = lambda t: pltpu.roll(t, HEAD_DIM, axis=1)

    def pack(k_ref, v_ref, base):
        def body(c, carry):
            src = pl.ds(pl.multiple_of(c * qb2, qb2), qb2)
            dst = pl.ds(pl.multiple_of(base + c * qb2, qb2), qb2)
            k = k_ref[src, :]
            kh = k.astype(BF16).astype(F32)
            kl = k - kh
            kq0a[dst, :] = jnp.where(head0_2, kh, swap(kh))
            kq0b[dst, :] = jnp.where(head0_2, kl, 0.0)
            kq1a[dst, :] = jnp.where(head0_2, swap(kh), kh)
            kq1b[dst, :] = jnp.where(head0_2, swap(kl), 0.0)
            v = v_ref[src, :]
            vh = v.astype(BF16).astype(F32)
            vqh[dst, :] = vh
            vql[dst, :] = v - vh
            return carry
        lax.fori_loop(0, ATTN_SPAN // qb2, body, 0)

    pack(kp_ref, vp_ref, 0)
    pack(kc_ref, vc_ref, ATTN_SPAN)

    row = lax.broadcasted_iota(jnp.int32, (qb2, qb2), 0) & (Q_BLOCK - 1)
    col = lax.broadcasted_iota(jnp.int32, (qb2, qb2), 1)
    band = (col >= row) & (col <= row + Q_BLOCK)
    cur = col >= Q_BLOCK
    nt = (((1,), (1,)), ((), ()))

    def ds(start, size, d):
        return pl.ds(start, size) if d == 1 else pl.ds(start, size, stride=d)

    for p, (_, d) in enumerate(DILATED_PATTERNS):
        nblk = ATTN_SPAN // (Q_BLOCK * d)
        assert d < RAW_KEY_STRIDE or nblk == 1

        def scores(blk, d=d, nblk=nblk):
            r = blk // nblk
            ib = blk % nblk
            qstart = r + d * Q_BLOCK * ib
            if d == 1:
                qstart = pl.multiple_of(Q_BLOCK * blk, Q_BLOCK)
            kstart = ATTN_SPAN + qstart - d * Q_BLOCK
            q = q_ref[ds(qstart, Q_BLOCK, d), :]
            qh = q.astype(BF16).astype(F32)
            ql = q - qh
            lhs0 = jnp.concatenate([jnp.where(head0, qh, swap(ql)), jnp.where(head0, qh, 0.0)], axis=1)
            lhs1 = jnp.concatenate([jnp.where(head0, swap(qh), ql), jnp.where(head0, swap(qh), 0.0)], axis=1)
            if d < RAW_KEY_STRIDE:
                keys = ds(kstart, qb2, d)
                k0 = jnp.concatenate([kq0a[keys, :], kq0b[keys, :]], axis=1).astype(BF16)
                k1 = jnp.concatenate([kq1a[keys, :], kq1b[keys, :]], axis=1).astype(BF16)
                vh, vl = vqh[keys, :], vql[keys, :]
            else:
                both = lambda p_ref, c_ref: jnp.concatenate(
                    [p_ref[ds(r, Q_BLOCK, d), :], c_ref[ds(r, Q_BLOCK, d), :]], axis=0)
                k = both(kp_ref, kc_ref)
                kh = k.astype(BF16).astype(F32)
                kl = k - kh
                k0 = jnp.concatenate([jnp.where(head0_2, kh, swap(kh)), jnp.where(head0_2, kl, 0.0)],
                                     axis=1).astype(BF16)
                k1 = jnp.concatenate([jnp.where(head0_2, swap(kh), kh), jnp.where(head0_2, swap(kl), 0.0)],
                                     axis=1).astype(BF16)
                v = both(vp_ref, vc_ref)
                vh = v.astype(BF16).astype(F32)
                vl = v - vh
            s0 = lax.dot_general(lhs0.astype(BF16), k0, nt, preferred_element_type=F32)
            s1 = lax.dot_general(lhs1.astype(BF16), k1, nt, preferred_element_type=F32)
            prev_ok = jnp.logical_or(ib > 0, span > 0)
            s = jnp.concatenate([s0, s1], axis=0)
            return jnp.where(band & (cur | prev_ok), s, -jnp.inf), qstart, (vh, vl)

        def softmax(s):
            m = jnp.max(s, axis=-1, keepdims=True)
            e = jnp.exp(s - m)
            return m, e, jnp.sum(e, axis=-1, keepdims=True)

        def weighted(e, v_parts):
            eh, el = _split_bf16(e)
            vh, vl = v_parts
            rhs = jnp.concatenate([jnp.concatenate([vh, vl], axis=1),
                                   jnp.concatenate([vh, jnp.zeros_like(vh)], axis=1)], axis=0).astype(BF16)
            out = jnp.dot(jnp.concatenate([eh, el], axis=1), rhs, preferred_element_type=F32)
            return out[:, :LANES] + out[:, LANES:]

        def store(qstart, m, l, acc, p=p, d=d):
            dst = ds(qstart, Q_BLOCK, d)
            shape = (Q_BLOCK, LANES)
            m_s[p, dst, :] = jnp.where(head0, jnp.broadcast_to(m[:Q_BLOCK], shape),
                                       jnp.broadcast_to(m[Q_BLOCK:], shape))
            l_s[p, dst, :] = jnp.where(head0, jnp.broadcast_to(l[:Q_BLOCK], shape),
                                       jnp.broadcast_to(l[Q_BLOCK:], shape))
            a_s[p, dst, :] = jnp.where(head0, acc[:Q_BLOCK], acc[Q_BLOCK:])

        def body(it, carry):
            sc = [scores(it * ATTN_GROUP + g) for g in range(ATTN_GROUP)]
            sm = [softmax(s) for s, _, _ in sc]
            ac = [weighted(e, v_parts) for (_, e, _), (_, _, v_parts) in zip(sm, sc)]
            for (_, qstart, _), (m, _, l), acc in zip(sc, sm, ac):
                store(qstart, m, l, acc)
            return carry

        lax.fori_loop(0, ATTN_SPAN // (Q_BLOCK * ATTN_GROUP), body, 0)

    def merge(c, carry):
        rows = pl.ds(pl.multiple_of(c * Q_BLOCK, Q_BLOCK), Q_BLOCK)
        ms = [m_s[p, rows, :] for p in range(N_PATTERNS)]
        m_all = functools.reduce(jnp.maximum, ms)
        ws = [jnp.exp(m - m_all) for m in ms]
        num = sum(w * a_s[p, rows, :] for p, w in enumerate(ws))
        den = sum(w * l_s[p, rows, :] for p, w in enumerate(ws))
        o_ref[rows, :] = num / den
        return carry

    lax.fori_loop(0, ATTN_SPAN // Q_BLOCK, merge, 0)


def _attn_prompt(q, k, v, batch, seq):
    nspan = seq // ATTN_SPAN
    blk = (ATTN_SPAN, LANES)
    cur = pl.BlockSpec(blk, lambda b, hp, s: (b * nspan + s, hp))
    prev = pl.BlockSpec(blk, lambda b, hp, s: (b * nspan + jnp.maximum(s - 1, 0), hp))
    acc = pltpu.VMEM((N_PATTERNS, ATTN_SPAN, LANES), F32)
    packed = pltpu.VMEM((2 * ATTN_SPAN, LANES), F32)
    return pl.pallas_call(
        _attn_prompt_kernel,
        grid=(batch, D_A // LANES, nspan),
        in_specs=[cur, prev, cur, prev, cur],
        out_specs=cur,
        out_shape=jax.ShapeDtypeStruct((batch * seq, D_A), F32),
        scratch_shapes=[packed] * 6 + [acc, acc, acc],
        compiler_params=_cparams(("parallel", "parallel", "arbitrary")),
        name="attn_prompt",
    )(q, k, k, v, v)


def _attn_sample_kernel(q_ref, kn_ref, vn_ref, ckt_ref, cvt_ref, o_ref, *, w_buf, t_new):
    pad = LANES - t_new
    zeros = jnp.zeros((pad, D_A), F32)
    kn = jnp.concatenate([kn_ref[...], zeros], axis=0).astype(BF16)
    vn = jnp.concatenate([vn_ref[...], zeros], axis=0).astype(BF16)
    n_rows = N_HEADS_A * t_new
    hrow = lax.broadcasted_iota(jnp.int32, (n_rows, D_A), 0) // t_new
    hlane = lax.broadcasted_iota(jnp.int32, (n_rows, D_A), 1) // HEAD_DIM
    own = hrow == hlane
    q_rep = jnp.concatenate([q_ref[...]] * N_HEADS_A, axis=0)
    q64 = jnp.where(own, q_rep, 0.0).astype(BF16)
    nt = (((1,), (1,)), ((), ()))
    s_w = jnp.dot(q64, ckt_ref[...].astype(BF16), preferred_element_type=F32)
    s_n = lax.dot_general(q64, kn, nt, preferred_element_type=F32)
    vt = cvt_ref[...].astype(BF16)

    def dist(n_cols, first):
        t = lax.broadcasted_iota(jnp.int32, (n_rows, n_cols), 0) % t_new
        return t - lax.broadcasted_iota(jnp.int32, (n_rows, n_cols), 1) - first

    dist_w, dist_n = dist(w_buf, -w_buf), dist(LANES, 0)
    ms, ls, accs = [], [], []
    for w, d in DILATED_PATTERNS:
        ok = lambda ds: (ds >= 0) & (ds <= w) & ((ds & (d - 1)) == 0)
        sw = jnp.where(ok(dist_w), s_w, -jnp.inf)
        sn = jnp.where(ok(dist_n), s_n, -jnp.inf)
        m = jnp.maximum(jnp.max(sw, axis=-1, keepdims=True), jnp.max(sn, axis=-1, keepdims=True))
        ew = jnp.exp(sw - m)
        en = jnp.exp(sn - m)
        ms.append(m)
        ls.append(jnp.sum(ew, axis=-1, keepdims=True) + jnp.sum(en, axis=-1, keepdims=True))
        accs.append(lax.dot_general(ew.astype(BF16), vt, nt, preferred_element_type=F32)
                    + jnp.dot(en.astype(BF16), vn, preferred_element_type=F32))
    m_all = functools.reduce(jnp.maximum, ms)
    ws = [jnp.exp(m - m_all) for m in ms]
    num = sum(w * a for w, a in zip(ws, accs))
    den = sum(w * l for w, l in zip(ws, ls))
    o = jnp.where(own, num / den, 0.0)
    out = o[0:t_new]
    for h in range(1, N_HEADS_A):
        out = out + o[h * t_new:(h + 1) * t_new]
    o_ref[...] = out


def _attn_sample(q, k, v, cache_kt, cache_vt, layer, row0, batch, t_new):
    w_buf = cache_kt.shape[3]
    assert row0 % t_new == 0 and t_new == SUBLANES
    new = pl.BlockSpec((t_new, D_A), lambda b: (row0 // t_new + b, 0))
    cache = pl.BlockSpec((None, None, D_A, w_buf), lambda b: (layer, b, 0, 0))
    return pl.pallas_call(
        functools.partial(_attn_sample_kernel, w_buf=w_buf, t_new=t_new),
        grid=(batch,),
        in_specs=[new, new, new, cache, cache],
        out_specs=pl.BlockSpec((t_new, D_A), lambda b: (b, 0)),
        out_shape=jax.ShapeDtypeStruct((batch * t_new, D_A), F32),
        compiler_params=_cparams(("parallel",)),
        name="attn_sample",
    )(q, k, v, cache_kt, cache_vt)


def _export_kernel(*refs):
    k_ref, v_ref = refs[0], refs[1]
    pk_ref, pv_ref = refs[-2], refs[-1]
    pk_ref[...] = k_ref[...].T
    pv_ref[...] = v_ref[...].T


def _export_window(k, v, prev, layer, depth, batch, seq, keep):
    tm = TOKEN_TILE
    first = (seq - keep) // tm
    src = pl.BlockSpec((tm, D_A), lambda b, j: (b * (seq // tm) + first + j, 0))
    dst = pl.BlockSpec((None, None, D_A, tm), lambda b, j: (layer, b, 0, j))
    shape = jax.ShapeDtypeStruct((depth, batch, D_A, keep), F32)
    carried = [] if prev is None else list(prev)
    return pl.pallas_call(
        _export_kernel,
        grid=(batch, keep // tm),
        in_specs=[src, src] + [pl.BlockSpec(memory_space=pl.ANY)] * len(carried),
        out_specs=[dst, dst],
        out_shape=[shape, shape],
        input_output_aliases={2 + i: i for i in range(len(carried))},
        compiler_params=_cparams(("parallel", "parallel")),
        name="export_window",
    )(k, v, *carried)


def _gelu_tanh(x):
    return 0.5 * x * (1.0 + jnp.tanh(0.7978845608028654 * (x + 0.044715 * x * x * x)))


def _mixer_bc_kernel(bc_ref, cb_ref, h0_ref, ws_ref, bs_ref, cw_ref, cbias_ref, wa_ref, ba_ref,
                     wx_ref, bx_ref, lam_ref, ob_ref, oc_ref, hl_ref, xp_s, h_s, *, tt, last_row):
    j = pl.program_id(1)

    @pl.when(j == 0)
    def _():
        xp_s[0:SUBLANES, :] = cb_ref[...]
        h_s[...] = h0_ref[...]

    nch = tt // CHUNK
    vcat = jnp.concatenate([bc_ref[c * CHUNK:(c + 1) * CHUNK, D_B:2 * D_B] for c in range(nch)],
                           axis=1).astype(BF16)
    ri = lax.broadcasted_iota(jnp.int32, (CHUNK, CHUNK), 0)
    ci = lax.broadcasted_iota(jnp.int32, (CHUNK, CHUNK), 1)
    hl = (lax.broadcasted_iota(jnp.int32, (CHUNK, nch * D_B), 1) % D_B) // HEAD_DIM
    mixed = jnp.zeros((CHUNK, nch * D_B), F32)
    for h in range(N_HEADS_B):
        wh = jnp.where(ri >= ci, ws_ref[h], 0.0).astype(BF16)
        mh = jnp.dot(wh, vcat, preferred_element_type=F32)
        mixed = mixed + jnp.where(hl == h, mh, 0.0)
    for c in range(nch):
        rows = slice(c * CHUNK, (c + 1) * CHUNK)
        ob_ref[rows, :] = bc_ref[rows, 0:D_B] * (mixed[:, c * D_B:(c + 1) * D_B] + bs_ref[...])

    xc = bc_ref[:, 2 * D_B:2 * D_B + D_C]
    xp_s[SUBLANES:SUBLANES + tt, :] = xc
    xconv = cbias_ref[...] + cw_ref[CONV_W - 1:CONV_W, :] * xc
    for kk in range(CONV_W - 1):
        off = SUBLANES - (CONV_W - 1) + kk
        xconv = xconv + cw_ref[kk:kk + 1, :] * xp_s[off:off + tt, :]
    xp_s[0:SUBLANES, :] = xp_s[tt:tt + SUBLANES, :]
    xb = xconv.astype(BF16)
    r = jax.nn.sigmoid(jnp.dot(xb, wa_ref[...], preferred_element_type=F32) + ba_ref[...])
    i = jax.nn.sigmoid(jnp.dot(xb, wx_ref[...], preferred_element_type=F32) + bx_ref[...])
    nl = -lam_ref[...]
    softplus = jnp.maximum(nl, 0.0) + jnp.log1p(jnp.exp(-jnp.abs(nl)))
    a = jnp.exp(-LRU_C * r * softplus)
    b = jnp.sqrt(1.0 - a * a) * (i * xconv)
    rowi = lax.broadcasted_iota(jnp.int32, (tt, D_C), 0)
    step = 1
    while step < tt:
        a_sh = pltpu.roll(a, step, axis=0)
        b_sh = pltpu.roll(b, step, axis=0)
        live = rowi >= step
        b = jnp.where(live, a * b_sh + b, b)
        a = jnp.where(live, a * a_sh, a)
        step *= 2
    h = a * h_s[...] + b
    h_s[...] = h[tt - 1:tt, :]
    oc_ref[...] = h * _gelu_tanh(bc_ref[:, 2 * D_B + D_C:])

    @pl.when(j == pl.num_programs(1) - 1)
    def _():
        hl_ref[...] = h[last_row:last_row + 1, :]


def _mixer_bc(bc, conv_buf8, h0, lw, batch, t_len, tt, last_row):
    nt = t_len // tt
    rows = lambda w: pl.BlockSpec((tt, w), lambda b, j: (b * nt + j, 0))
    per_b = lambda s: pl.BlockSpec((None,) + s, lambda b, j: (b,) + (0,) * len(s))
    return pl.pallas_call(
        functools.partial(_mixer_bc_kernel, tt=tt, last_row=last_row),
        grid=(batch, nt),
        in_specs=[rows(2 * D_B + 2 * D_C), per_b((SUBLANES, D_C)), per_b((1, D_C)),
                  _full((N_HEADS_B, CHUNK, CHUNK)), _full((CHUNK, D_B)), _full((CONV_W, D_C)),
                  _full((1, D_C)), _full((D_C, D_C)), _full((1, D_C)), _full((D_C, D_C)),
                  _full((1, D_C)), _full((1, D_C))],
        out_specs=[rows(D_B), rows(D_C), per_b((1, D_C))],
        out_shape=[jax.ShapeDtypeStruct((batch * t_len, D_B), F32),
                   jax.ShapeDtypeStruct((batch * t_len, D_C), F32),
                   jax.ShapeDtypeStruct((batch, 1, D_C), F32)],
        scratch_shapes=[pltpu.VMEM((tt + SUBLANES, D_C), F32), pltpu.VMEM((1, D_C), F32)],
        compiler_params=_cparams(("parallel", "arbitrary")),
        name="mixer_bc",
    )(bc, conv_buf8, h0, lw["w_s"], lw["bias_s"], lw["conv_w"], lw["conv_b"], lw["w_a_bd"],
      lw["b_a"], lw["w_x_bd"], lw["b_x"], lw["lru_lambda"])


def _out_proj_kernel(oah_ref, oat_ref, obh_ref, obt_ref, och_ref, oct_ref, xh_ref, xt_ref, go_ref, wo_ref,
                     g2_ref, wr_ref, br_ref, xn_ref, h2_ref, idx_ref, gate_ref, rank_ref, cnt_ref, *,
                     n_head_tiles):
    pick = functools.partial(_pick, n_head_tiles=n_head_tiles)
    oa = _rms(pick(oah_ref, oat_ref), go_ref[:, 0:D_A]).astype(BF16)
    ob = _rms(pick(obh_ref, obt_ref), go_ref[:, D_A:D_A + D_B]).astype(BF16)
    oc = _rms(pick(och_ref, oct_ref), go_ref[:, D_A + D_B:]).astype(BF16)
    y = (jnp.dot(oa, wo_ref[0:D_A, :], preferred_element_type=F32)
         + jnp.dot(ob, wo_ref[D_A:D_A + D_B, :], preferred_element_type=F32)
         + jnp.dot(oc, wo_ref[D_A + D_B:, :], preferred_element_type=F32))
    xn = pick(xh_ref, xt_ref) + y
    xn_ref[...] = xn
    hb = _rms(xn, g2_ref[...]).astype(BF16)
    bits = pltpu.bitcast(hb.astype(F32), jnp.uint32)
    half = D_MODEL // 2
    h2_ref[...] = (bits[:, :half] >> 16) | (bits[:, half:] & jnp.uint32(0xFFFF0000))
    logits = jnp.dot(hb, wr_ref[...], preferred_element_type=F32) + br_ref[...]
    lane = lax.broadcasted_iota(jnp.int32, logits.shape, 1).astype(F32)
    cur = logits
    tops, idxs = [], []
    for _ in range(TOP_K):
        m = jnp.max(cur, axis=-1, keepdims=True)
        ix = jnp.min(jnp.where(cur == m, lane, float(ROUTER_PAD)), axis=-1, keepdims=True)
        tops.append(m)
        idxs.append(ix)
        cur = jnp.where(lane == ix, -jnp.inf, cur)
    es = [jnp.exp(t - tops[0]) for t in tops]
    den = sum(es)
    idx_out = jnp.zeros(logits.shape, F32)
    gate_out = jnp.zeros(logits.shape, F32)
    for kk in range(TOP_K):
        idx_out = jnp.where(lane == kk, idxs[kk], idx_out)
        gate_out = jnp.where(lane == kk, es[kk] / den, gate_out)
    idx_ref[...] = idx_out.astype(jnp.int32)
    gate_ref[...] = gate_out
    tm = logits.shape[0]
    chosen = jnp.zeros(logits.shape, F32)
    for kk in range(TOP_K):
        chosen = jnp.where(lane == idxs[kk], 1.0, chosen)
    below = (lax.broadcasted_iota(jnp.int32, (tm, tm), 0)
             > lax.broadcasted_iota(jnp.int32, (tm, tm), 1)).astype(BF16)
    earlier = jnp.dot(below, chosen.astype(BF16), preferred_element_type=F32)
    rank_out = jnp.zeros(logits.shape, F32)
    for kk in range(TOP_K):
        rk = jnp.sum(jnp.where(lane == idxs[kk], earlier, 0.0), axis=-1, keepdims=True)
        rank_out = jnp.where(lane == kk, rk, rank_out)
    rank_ref[...] = rank_out.astype(jnp.int32)
    cnt_ref[...] = jnp.broadcast_to(jnp.sum(chosen, axis=0, keepdims=True),
                                    cnt_ref.shape).astype(jnp.int32)


def _out_proj(oa_pair, ob_pair, oc_pair, x_pair, lw):
    n_head = x_pair[0].shape[0]
    n = n_head + x_pair[1].shape[0]
    tm = TOKEN_TILE
    row = lambda w: pl.BlockSpec((tm, w), lambda i: (i, 0))
    pair = lambda w: _pair_specs(tm, w, n_head)
    return pl.pallas_call(
        functools.partial(_out_proj_kernel, n_head_tiles=n_head // tm),
        grid=(n // tm,),
        in_specs=pair(D_A) + pair(D_B) + pair(D_C) + pair(D_MODEL) + [_full((1, D_MODEL)),
                  _full((D_MODEL, D_MODEL)), _full((1, D_MODEL)), _full((D_MODEL, ROUTER_PAD)),
                  _full((1, ROUTER_PAD))],
        out_specs=[row(D_MODEL), row(D_MODEL // 2), row(ROUTER_PAD), row(ROUTER_PAD), row(ROUTER_PAD),
                   pl.BlockSpec((None, SUBLANES, ROUTER_PAD), lambda i: (i, 0, 0))],
        out_shape=[jax.ShapeDtypeStruct((n, D_MODEL), F32), jax.ShapeDtypeStruct((n, D_MODEL // 2), jnp.uint32),
                   jax.ShapeDtypeStruct((n, ROUTER_PAD), jnp.int32),
                   jax.ShapeDtypeStruct((n, ROUTER_PAD), F32),
                   jax.ShapeDtypeStruct((n, ROUTER_PAD), jnp.int32),
                   jax.ShapeDtypeStruct((n // tm, SUBLANES, ROUTER_PAD), jnp.int32)],
        compiler_params=_cparams(("parallel",)),
        name="out_proj",
    )(*oa_pair, *ob_pair, *oc_pair, *x_pair, lw["g_out"], lw["w_out"], lw["ln2_g"], lw["w_router"],
      lw["b_router"])


def _moe_kernel(te_ref, nu_ref, x_ref, wgu_ref, bg_ref, bl_ref, wd_ref, bd_ref, y_ref, wg_s, wl_s, wd_s):
    i = pl.program_id(0)
    used = i < nu_ref[0]
    new_expert = jnp.logical_or(i == 0, te_ref[i] != te_ref[jnp.maximum(i - 1, 0)])

    @pl.when(jnp.logical_and(used, new_expert))
    def _():
        w2 = 2 * LANES
        src = lax.broadcasted_iota(jnp.int32, (w2, w2), 0)
        col = lax.broadcasted_iota(jnp.int32, (w2, w2), 1)
        pick = (src == jnp.where(col < LANES, 2 * col, 2 * (col - LANES) + 1)).astype(BF16)
        for c in range(2 * D_FF // w2):
            out = jnp.dot(wgu_ref[:, c * w2:(c + 1) * w2].astype(BF16), pick, preferred_element_type=F32)
            wg_s[:, c * LANES:(c + 1) * LANES] = out[:, :LANES].astype(BF16)
            wl_s[:, c * LANES:(c + 1) * LANES] = out[:, LANES:].astype(BF16)
        wd_s[...] = wd_ref[...].astype(BF16)

    @pl.when(used)
    def _():
        u = x_ref[...]
        x = jnp.concatenate([pltpu.bitcast(u << 16, F32).astype(BF16),
                             pltpu.bitcast(u & jnp.uint32(0xFFFF0000), F32).astype(BF16)], axis=1)
        zg = jnp.dot(x, wg_s[...], preferred_element_type=F32) + bg_ref[...]
        zl = jnp.dot(x, wl_s[...], preferred_element_type=F32) + bl_ref[...]
        glu = jnp.minimum(zg, SWIGLU_LIMIT)
        lin = jnp.clip(zl, -SWIGLU_LIMIT, SWIGLU_LIMIT)
        act = glu * jax.nn.sigmoid(SWIGLU_ALPHA * glu) * (lin + 1.0)
        y_ref[...] = jnp.dot(act.astype(BF16), wd_s[...], preferred_element_type=F32) + bd_ref[...]

    @pl.when(i >= nu_ref[0])
    def _():
        y_ref[...] = jnp.zeros(y_ref.shape, F32)


def _moe_ffn(xs, tile_expert, n_used, lw):
    n_slots = xs.shape[0]
    tm = MOE_TILE
    layer = lw["layer"]
    wspec = lambda r, c: pl.BlockSpec((None, r, c), lambda i, te, nu: (te[i], 0, 0))
    wfull = lambda r, c: pl.BlockSpec((None, None, r, c), lambda i, te, nu: (layer, te[i], 0, 0))
    return pl.pallas_call(
        _moe_kernel,
        grid_spec=pltpu.PrefetchScalarGridSpec(
            num_scalar_prefetch=2,
            grid=(n_slots // tm,),
            in_specs=[pl.BlockSpec((tm, D_MODEL // 2), lambda i, te, nu: (i, 0)),
                      wfull(D_MODEL, 2 * D_FF), wspec(1, D_FF), wspec(1, D_FF),
                      wfull(D_FF, D_MODEL), wspec(1, D_MODEL)],
            out_specs=pl.BlockSpec((tm, D_MODEL), lambda i, te, nu: (i, 0)),
            scratch_shapes=[pltpu.VMEM((D_MODEL, D_FF), BF16), pltpu.VMEM((D_MODEL, D_FF), BF16),
                            pltpu.VMEM((D_FF, D_MODEL), BF16)],
        ),
        out_shape=jax.ShapeDtypeStruct((n_slots, D_MODEL), F32),
        compiler_params=_cparams(("arbitrary",)),
        name="moe_ffn",
    )(tile_expert, n_used, xs, lw["w_gu"], lw["b_glu"], lw["b_lin"], lw["w_down"], lw["b_down"])


def _route(idx, rank, counts):
    n_tok = idx.shape[0]
    experts = jnp.arange(N_EXPERTS, dtype=jnp.int32)
    totals = jnp.sum(counts, axis=0)
    tiles = (totals + MOE_TILE - 1) // MOE_TILE
    tile_end = jnp.sum(jnp.where(experts[None, :] <= experts[:, None], tiles[None, :], 0), axis=1)
    pad_start = (tile_end - tiles) * MOE_TILE
    tt = jnp.arange(counts.shape[0], dtype=jnp.int32)
    before = jnp.sum(jnp.where((tt[None, :] < tt[:, None])[:, :, None], counts[None, :, :], 0), axis=1)
    base = pad_start[None, :] + before
    base_tok = jnp.repeat(base, TOKEN_TILE, axis=0)
    picked = idx[:, :, None] == experts[None, None, :]
    dest = jnp.sum(jnp.where(picked, base_tok[:, None, :], 0), axis=-1) + rank
    n_tiles = -(-(n_tok * TOP_K) // MOE_TILE) + N_EXPERTS
    tile_ids = jnp.arange(n_tiles, dtype=jnp.int32)
    tile_expert = jnp.minimum(jnp.sum((tile_end[None, :] <= tile_ids[:, None]).astype(jnp.int32), axis=1),
                              N_EXPERTS - 1)
    n_used = tile_end[-1:].astype(jnp.int32)
    return dest.astype(jnp.int32), tile_expert, n_used, n_tiles


def _sc_mesh():
    return plsc.VectorSubcoreMesh(core_axis_name="core", subcore_axis_name="subcore")


def _load_index_rows(d_hbm, i_vmem, sem, wid, ng, nr):
    def row(r):
        return pltpu.make_async_copy(d_hbm.at[pl.ds(wid + SC_WORKERS * r, 1)], i_vmem.at[pl.ds(r, 1)], sem)
    for r in range(nr):
        pl.when(wid + SC_WORKERS * r < ng)(lambda r=r: row(r).start())
    for r in range(nr):
        pl.when(wid + SC_WORKERS * r < ng)(lambda r=r: row(r).wait())


def _sc_dispatch(h2, dest_g, n_slots):
    ng = dest_g.shape[0]
    width = h2.shape[1]

    @pl.kernel(out_type=jax.ShapeDtypeStruct((n_slots, width), h2.dtype), mesh=_sc_mesh(),
               scratch_types=[pltpu.VMEM((pl.cdiv(ng, SC_WORKERS), LANES), jnp.int32), pltpu.VMEM((SC_GROUP, width), h2.dtype)]
               + [pltpu.SemaphoreType.DMA] * (1 + TOP_K))
    def kernel(h_hbm, d_hbm, o_hbm, i_vmem, buf, isem, *sems):
        wid = lax.axis_index("core") * SC_SUBCORES + lax.axis_index("subcore")
        nr = pl.cdiv(ng, SC_WORKERS)
        _load_index_rows(d_hbm, i_vmem, isem, wid, ng, nr)

        @pl.loop(0, nr)
        def _(r):
            g = wid + SC_WORKERS * r

            @pl.when(g < ng)
            def _():
                pltpu.sync_copy(h_hbm.at[pl.ds(g * SC_GROUP, SC_GROUP)], buf)
                puts = [pltpu.async_copy(buf, o_hbm.at[i_vmem.at[r, pl.ds(k * SC_GROUP, SC_GROUP)]], sems[k])
                        for k in range(TOP_K)]
                for put in puts:
                    put.wait()

    return kernel(h2, dest_g)


def _sc_collect(y, dest_g):
    ng = dest_g.shape[0]

    @pl.kernel(out_type=jax.ShapeDtypeStruct((ng * LANES, D_MODEL), F32), mesh=_sc_mesh(),
               scratch_types=[pltpu.VMEM((pl.cdiv(ng, SC_WORKERS), LANES), jnp.int32)]
               + [pltpu.VMEM((SC_GROUP, D_MODEL), F32)] * 2 + [pltpu.SemaphoreType.DMA] * 5)
    def kernel(y_hbm, d_hbm, o_hbm, i_vmem, buf0, buf1, isem, g0, g1, w0, w1):
        wid = lax.axis_index("core") * SC_SUBCORES + lax.axis_index("subcore")
        bufs, gsem, wsem = (buf0, buf1), (g0, g1), (w0, w1)
        nr = pl.cdiv(ng, SC_WORKERS)
        _load_index_rows(d_hbm, i_vmem, isem, wid, ng, nr)

        @pl.loop(0, nr)
        def _(r):
            g = wid + SC_WORKERS * r

            @pl.when(g < ng)
            def _():
                get = lambda k: pltpu.async_copy(
                    y_hbm.at[i_vmem.at[r, pl.ds(k * SC_GROUP, SC_GROUP)]], bufs[k % 2], gsem[k % 2])
                put = lambda k: pltpu.async_copy(
                    bufs[k % 2], o_hbm.at[pl.ds(g * LANES + k * SC_GROUP, SC_GROUP)], wsem[k % 2])
                gets = [get(0), get(1)]
                puts = []
                for k in range(TOP_K):
                    gets[k].wait()
                    puts.append(put(k))
                    if k + 2 < TOP_K:
                        puts[k].wait()
                        gets.append(get(k + 2))
                for k in range(TOP_K - 2, TOP_K):
                    puts[k].wait()

    return kernel(y, dest_g)


def _combine_kernel(xn_ref, g_ref, y_ref, oh_ref, ot_ref, *, n_head_tiles):
    def emit(o_ref):
        for gi in range(TOKEN_TILE // SC_GROUP):
            rows = slice(gi * SC_GROUP, (gi + 1) * SC_GROUP)
            acc = xn_ref[rows, :]
            for k in range(TOP_K):
                r0 = gi * LANES + k * SC_GROUP
                acc = acc + g_ref[rows, k:k + 1] * y_ref[r0:r0 + SC_GROUP, :]
            o_ref[rows, :] = acc

    pl.when(pl.program_id(0) < n_head_tiles)(lambda: emit(oh_ref))
    pl.when(pl.program_id(0) >= n_head_tiles)(lambda: emit(ot_ref))


def _combine(xn, gates, y4, n_head):
    n = xn.shape[0]
    tm = TOKEN_TILE
    row = lambda r, w: pl.BlockSpec((r, w), lambda i: (i, 0))
    return pl.pallas_call(
        functools.partial(_combine_kernel, n_head_tiles=n_head // tm),
        grid=(n // tm,),
        in_specs=[row(tm, D_MODEL), row(tm, ROUTER_PAD), row(tm * TOP_K, D_MODEL)],
        out_specs=_pair_specs(tm, D_MODEL, n_head),
        out_shape=[jax.ShapeDtypeStruct((n_head, D_MODEL), F32),
                   jax.ShapeDtypeStruct((n - n_head, D_MODEL), F32)],
        compiler_params=_cparams(("arbitrary",)),
        name="moe_combine",
    )(xn, gates, y4)


def _moe(xn, h2, idx, gates, rank, counts, lw, n_head):
    n_tok = xn.shape[0]
    dest, tile_expert, n_used, n_tiles = _route(idx[:, :TOP_K], rank[:, :TOP_K], counts)
    dest_g = dest.reshape(n_tok // SC_GROUP, SC_GROUP, TOP_K).transpose(0, 2, 1).reshape(-1, LANES)
    xs = _sc_dispatch(h2, dest_g, n_tiles * MOE_TILE)
    y = _moe_ffn(xs, tile_expert, n_used, lw)
    return _combine(xn, gates, _sc_collect(y, dest_g), n_head)


def _block_diag(w):
    g, a, b = w.shape
    out = jnp.zeros((g * a, g * b), w.dtype)
    for i in range(g):
        out = out.at[i * a:(i + 1) * a, i * b:(i + 1) * b].set(w[i])
    return out


def _layer_weights(l, p):
    wr = jnp.pad(p["w_router"][l], ((0, 0), (0, ROUTER_PAD - N_EXPERTS)))
    hd = jnp.arange(D_A) // HEAD_DIM
    return {
        "ln1_g": p["ln1_g"][l][None], "w_in": p["w_in"][l].astype(BF16),
        "g_q": jnp.tile(p["g_q"][l], N_HEADS_A)[None], "g_k": jnp.tile(p["g_k"][l], N_HEADS_A)[None],
        "g_vb": p["g_vb"][l][None],
        "ones_bd": (hd[:, None] == hd[None, :]).astype(BF16),
        "w_s": p["w_s"][l],
        "bias_s": jnp.repeat(p["b_s"][l].T, HEAD_DIM, axis=1),
        "conv_w": p["conv_w"][l], "conv_b": p["conv_b"][l][None],
        "w_a_bd": _block_diag(p["w_a"][l]).astype(BF16), "b_a": p["b_a"][l][None],
        "w_x_bd": _block_diag(p["w_x"][l]).astype(BF16), "b_x": p["b_x"][l][None],
        "lru_lambda": p["lru_lambda"][l][None],
        "g_out": p["g_out"][l][None], "w_out": p["w_out"][l].astype(BF16),
        "ln2_g": p["ln2_g"][l][None],
        "w_router": wr.astype(BF16),
        "b_router": jnp.pad(p["b_router"][l], (0, ROUTER_PAD - N_EXPERTS),
                            constant_values=NEG_BIG)[None],
        "layer": l, "w_gu": p["w_gu"], "w_down": p["w_down"],
        "b_glu": p["b_gu"][l][:, None, 0::2], "b_lin": p["b_gu"][l][:, None, 1::2],
        "b_down": p["b_down"][l][:, None, :],
    }


def kernel(x_prompt, x_sample, cache_win_k, cache_win_v, state_conv, state_lru, ln1_g, w_in, g_q, g_k,
           g_vb, w_s, b_s, conv_w, conv_b, w_a, b_a, w_x, b_x, lru_lambda, g_out, w_out, ln2_g, w_router,
           b_router, w_gu, b_gu, w_down, b_down):
    params = dict(ln1_g=ln1_g, w_in=w_in, g_q=g_q, g_k=g_k, g_vb=g_vb, w_s=w_s, b_s=b_s, conv_w=conv_w,
                  conv_b=conv_b, w_a=w_a, b_a=b_a, w_x=w_x, b_x=b_x, lru_lambda=lru_lambda, g_out=g_out,
                  w_out=w_out, ln2_g=ln2_g, w_router=w_router, b_router=b_router, w_gu=w_gu, b_gu=b_gu,
                  w_down=w_down, b_down=b_down)
    bp, sp, _ = x_prompt.shape
    bs, ss, _ = x_sample.shape
    depth = w_in.shape[0]
    n_p, n_s = bp * sp, bs * ss
    keep = min(DILATED_PATTERNS[-1][0], sp)
    w_buf = cache_win_k.shape[2]
    ckt = cache_win_k.transpose(0, 1, 3, 4, 2).reshape(depth, bs, D_A, w_buf)
    cvt = cache_win_v.transpose(0, 1, 3, 4, 2).reshape(depth, bs, D_A, w_buf)
    x = (x_prompt.reshape(n_p, D_MODEL), x_sample.reshape(n_s, D_MODEL))
    zero_conv = jnp.zeros((bp, SUBLANES, D_C), F32)
    zero_h = jnp.zeros((bp, 1, D_C), F32)
    outs = {name: [] for name in ("pconv", "plru", "sk", "sv", "sconv", "slru", "svb")}
    window = None
    for l in range(depth):
        lw = _layer_weights(l, params)
        q, k, v, bc = _in_proj(x, lw["ln1_g"], lw["w_in"], lw["g_q"], lw["g_k"], lw["g_vb"], lw["ones_bd"])
        oa_p = _attn_prompt(q, k, v, bp, sp)
        ob_p, oc_p, h_p = _mixer_bc(bc, zero_conv, zero_h, lw, bp, sp, MIX_TILE, MIX_TILE - 1)
        oa_s = _attn_sample(q, k, v, ckt, cvt, l, n_p, bs, ss)
        bc_s = bc[n_p:].reshape(bs, ss, -1)
        bc_s_pad = jnp.pad(bc_s, ((0, 0), (0, CHUNK - ss), (0, 0))).reshape(bs * CHUNK, -1)
        conv8 = jnp.pad(state_conv[l], ((0, 0), (SUBLANES - (CONV_W - 1), 0), (0, 0)))
        ob_s, oc_s, h_s = _mixer_bc(bc_s_pad, conv8, state_lru[l][:, None, :], lw, bs, CHUNK, CHUNK, ss - 1)
        ob_s = ob_s.reshape(bs, CHUNK, D_B)[:, :ss].reshape(n_s, D_B)
        oc_s = oc_s.reshape(bs, CHUNK, D_C)[:, :ss].reshape(n_s, D_C)
        xn, h2, idx, gates, rank, counts = _out_proj((oa_p, oa_s), (ob_p, ob_s), (oc_p, oc_s), x, lw)
        x = _moe(xn, h2, idx, gates, rank, counts[:, 0, :N_EXPERTS], lw, n_p)

        window = _export_window(k, v, window, l, depth, bp, sp, keep)
        xc_p = bc[:n_p, 2 * D_B:2 * D_B + D_C].reshape(bp, sp, D_C)
        outs["pconv"].append(xc_p[:, sp - (CONV_W - 1):])
        outs["plru"].append(h_p[:, 0])
        outs["sk"].append(k[n_p:].reshape(bs, ss, N_HEADS_A, HEAD_DIM))
        outs["sv"].append(v[n_p:].reshape(bs, ss, N_HEADS_A, HEAD_DIM))
        xpad_s = jnp.concatenate([state_conv[l], bc_s[:, :, 2 * D_B:2 * D_B + D_C]], axis=1)
        outs["sconv"].append(xpad_s[:, -(CONV_W - 1):])
        outs["slru"].append(h_s[:, 0])
        outs["svb"].append(bc_s[:, :, D_B:2 * D_B])
    y_p = x[0].reshape(bp, sp, D_MODEL)
    y_s = x[1].reshape(bs, ss, D_MODEL)
    st = lambda name: jnp.stack(outs[name])
    heads_last = lambda t: t.reshape(depth, bp, N_HEADS_A, HEAD_DIM, keep).transpose(0, 1, 4, 2, 3)
    return (y_p, y_s, heads_last(window[0]), heads_last(window[1]), st("pconv"), st("plru"), st("sk"), st("sv"), st("sconv"),
            st("slru"), st("svb"))
```

```python
import functools

import jax
import jax.numpy as jnp
from jax import lax
from jax.experimental import pallas as pl
from jax.experimental.pallas import tpu as pltpu
from jax.experimental.pallas import tpu_sc as plsc

F32 = jnp.float32
BF16 = jnp.bfloat16

D_MODEL = 1024
HEAD_DIM = 64
N_HEADS_A = 8
D_A = N_HEADS_A * HEAD_DIM
N_HEADS_B = 4
D_B = N_HEADS_B * HEAD_DIM
N_GROUPS_C = 4
D_C = N_GROUPS_C * HEAD_DIM
D_IN = 3 * D_A + 2 * D_B + 2 * D_C
DILATED_PATTERNS = ((128, 1), (512, 4), (2048, 16))
N_PATTERNS = len(DILATED_PATTERNS)
CHUNK = 128
CONV_W = 4
LRU_C = 8.0
N_EXPERTS = 32
TOP_K = 4
D_FF = 1024
SWIGLU_LIMIT = 7.0
SWIGLU_ALPHA = 1.702
EPS = 1e-6
ATTN_SCALE = HEAD_DIM ** -0.5
PAST_LEN = 16384

LANES = 128
SUBLANES = 8
VMEM_LIMIT_BYTES = 56 * 1024 * 1024

Q_BLOCK = 128
ATTN_SPAN = 2048
ATTN_GROUP = 4
RAW_KEY_STRIDE = 16
TOKEN_TILE = 256
EXPORT_TILE = 1024
MIX_TILE = 512
MOE_TILE = 256
ROUTER_PAD = LANES
SC_SUBCORES = 16
SC_WORKERS = 2 * SC_SUBCORES
SC_GROUP = LANES // TOP_K
NEG_BIG = -1e30


def _cparams(semantics):
    return pltpu.CompilerParams(dimension_semantics=semantics,
                                vmem_limit_bytes=VMEM_LIMIT_BYTES)


def _full(shape):
    return pl.BlockSpec(shape, lambda *_: (0,) * len(shape))


def _rms(t, g):
    ms = jnp.mean(t * t, axis=-1, keepdims=True)
    return t * lax.rsqrt(ms + EPS) * g


def _pair_specs(tm, w, n_head):
    nh = n_head // tm
    return [pl.BlockSpec((tm, w), lambda i: (jnp.minimum(i, nh - 1), 0)),
            pl.BlockSpec((tm, w), lambda i: (jnp.maximum(i - nh, 0), 0))]


def _pick(head_ref, tail_ref, n_head_tiles):
    return jnp.where(pl.program_id(0) >= n_head_tiles, tail_ref[...], head_ref[...])


def _split_bf16(t):
    hi = t.astype(BF16)
    lo = (t - hi.astype(F32)).astype(BF16)
    return hi, lo


def _in_proj_kernel(xh_ref, xt_ref, g1_ref, w_ref, gq_ref, gk_ref, gvb_ref, ones_ref,
                    q_ref, k_ref, v_ref, bc_ref, *, n_head_tiles):
    h = _rms(_pick(xh_ref, xt_ref, n_head_tiles), g1_ref[...]).astype(BF16)
    z = jnp.dot(h, w_ref[...], preferred_element_type=F32)

    def head_norm(t, g):
        hi, lo = _split_bf16(t * t)
        ss = (jnp.dot(hi, ones_ref[...], preferred_element_type=F32)
              + jnp.dot(lo, ones_ref[...], preferred_element_type=F32))
        return t * lax.rsqrt(ss * (1.0 / HEAD_DIM) + EPS) * g

    q_ref[...] = head_norm(z[:, 0:D_A], gq_ref[...]) * ATTN_SCALE
    k_ref[...] = head_norm(z[:, D_A:2 * D_A], gk_ref[...])
    v_ref[...] = z[:, 2 * D_A:3 * D_A]
    o = 3 * D_A
    bc_ref[...] = z[:, o:]
    bc_ref[:, D_B:2 * D_B] = _rms(z[:, o + D_B:o + 2 * D_B], gvb_ref[...])


def _in_proj(x_pair, g1, w_bf16, gq, gk, gvb, ones_bd):
    n_head = x_pair[0].shape[0]
    n = n_head + x_pair[1].shape[0]
    tm = TOKEN_TILE
    row = lambda w: pl.BlockSpec((tm, w), lambda i: (i, 0))
    return pl.pallas_call(
        functools.partial(_in_proj_kernel, n_head_tiles=n_head // tm),
        grid=(n // tm,),
        in_specs=_pair_specs(tm, D_MODEL, n_head) + [_full((1, D_MODEL)), _full((D_MODEL, D_IN)), _full((1, D_A)),
                  _full((1, D_A)), _full((1, D_B)), _full((D_A, D_A))],
        out_specs=[row(D_A), row(D_A), row(D_A), row(2 * D_B + 2 * D_C)],
        out_shape=[jax.ShapeDtypeStruct((n, D_A), F32)] * 3
        + [jax.ShapeDtypeStruct((n, 2 * D_B + 2 * D_C), F32)],
        compiler_params=_cparams(("parallel",)),
        name="in_proj",
    )(*x_pair, g1, w_bf16, gq, gk, gvb, ones_bd)


def _attn_prompt_kernel(q_ref, kp_ref, kc_ref, vp_ref, vc_ref, o_ref, kq0a, kq0b, kq1a, kq1b, vqh, vql,
                        m_s, l_s, a_s):
    span = pl.program_id(2)
    qb2 = 2 * Q_BLOCK
    lane = lax.broadcasted_iota(jnp.int32, (Q_BLOCK, LANES), 1)
    head0 = lane < HEAD_DIM
    lane2 = lax.broadcasted_iota(jnp.int32, (qb2, LANES), 1)
    head0_2 = lane2 < HEAD_DIM
    swap = lambda t: pltpu.roll(t, HEAD_DIM, axis=1)

    def pack(k_ref, v_ref, base):
        def body(c, carry):
            src = pl.ds(pl.multiple_of(c * qb2, qb2), qb2)
            dst = pl.ds(pl.multiple_of(base + c * qb2, qb2), qb2)
            k = k_ref[src, :]
            kh = k.astype(BF16).astype(F32)
            kl = k - kh
            kq0a[dst, :] = jnp.where(head0_2, kh, swap(kh))
            kq0b[dst, :] = jnp.where(head0_2, kl, 0.0)
            kq1a[dst, :] = jnp.where(head0_2, swap(kh), kh)
            kq1b[dst, :] = jnp.where(head0_2, swap(kl), 0.0)
            v = v_ref[src, :]
            vh = v.astype(BF16).astype(F32)
            vqh[dst, :] = vh
            vql[dst, :] = v - vh
            return carry
        lax.fori_loop(0, ATTN_SPAN // qb2, body, 0)

    pack(kp_ref, vp_ref, 0)
    pack(kc_ref, vc_ref, ATTN_SPAN)

    row = lax.broadcasted_iota(jnp.int32, (qb2, qb2), 0) & (Q_BLOCK - 1)
    col = lax.broadcasted_iota(jnp.int32, (qb2, qb2), 1)
    band = (col >= row) & (col <= row + Q_BLOCK)
    cur = col >= Q_BLOCK
    nt = (((1,), (1,)), ((), ()))

    def ds(start, size, d):
        return pl.ds(start, size) if d == 1 else pl.ds(start, size, stride=d)

    for p, (_, d) in enumerate(DILATED_PATTERNS):
        nblk = ATTN_SPAN // (Q_BLOCK * d)
        assert d < RAW_KEY_STRIDE or nblk == 1

        def scores(blk, d=d, nblk=nblk):
            r = blk // nblk
            ib = blk % nblk
            qstart = r + d * Q_BLOCK * ib
            if d == 1:
                qstart = pl.multiple_of(Q_BLOCK * blk, Q_BLOCK)
            kstart = ATTN_SPAN + qstart - d * Q_BLOCK
            q = q_ref[ds(qstart, Q_BLOCK, d), :]
            qh = q.astype(BF16).astype(F32)
            ql = q - qh
            lhs0 = jnp.concatenate([jnp.where(head0, qh, swap(ql)), jnp.where(head0, qh, 0.0)], axis=1)
            lhs1 = jnp.concatenate([jnp.where(head0, swap(qh), ql), jnp.where(head0, swap(qh), 0.0)], axis=1)
            if d < RAW_KEY_STRIDE:
                keys = ds(kstart, qb2, d)
                k0 = jnp.concatenate([kq0a[keys, :], kq0b[keys, :]], axis=1).astype(BF16)
                k1 = jnp.concatenate([kq1a[keys, :], kq1b[keys, :]], axis=1).astype(BF16)
                vh, vl = vqh[keys, :], vql[keys, :]
            else:
                both = lambda p_ref, c_ref: jnp.concatenate(
                    [p_ref[ds(r, Q_BLOCK, d), :], c_ref[ds(r, Q_BLOCK, d), :]], axis=0)
                k = both(kp_ref, kc_ref)
                kh = k.astype(BF16).astype(F32)
                kl = k - kh
                k0 = jnp.concatenate([jnp.where(head0_2, kh, swap(kh)), jnp.where(head0_2, kl, 0.0)],
                                     axis=1).astype(BF16)
                k1 = jnp.concatenate([jnp.where(head0_2, swap(kh), kh), jnp.where(head0_2, swap(kl), 0.0)],
                                     axis=1).astype(BF16)
                v = both(vp_ref, vc_ref)
                vh = v.astype(BF16).astype(F32)
                vl = v - vh
            s0 = lax.dot_general(lhs0.astype(BF16), k0, nt, preferred_element_type=F32)
            s1 = lax.dot_general(lhs1.astype(BF16), k1, nt, preferred_element_type=F32)
            prev_ok = jnp.logical_or(ib > 0, span > 0)
            s = jnp.concatenate([s0, s1], axis=0)
            return jnp.where(band & (cur | prev_ok), s, -jnp.inf), qstart, (vh, vl)

        def softmax(s):
            m = jnp.max(s, axis=-1, keepdims=True)
            e = jnp.exp(s - m)
            return m, e, jnp.sum(e, axis=-1, keepdims=True)

        def weighted(e, v_parts):
            eh, el = _split_bf16(e)
            vh, vl = v_parts
            rhs = jnp.concatenate([jnp.concatenate([vh, vl], axis=1),
                                   jnp.concatenate([vh, jnp.zeros_like(vh)], axis=1)], axis=0).astype(BF16)
            out = jnp.dot(jnp.concatenate([eh, el], axis=1), rhs, preferred_element_type=F32)
            return out[:, :LANES] + out[:, LANES:]

        def store(qstart, m, l, acc, p=p, d=d):
            dst = ds(qstart, Q_BLOCK, d)
            shape = (Q_BLOCK, LANES)
            m_s[p, dst, :] = jnp.where(head0, jnp.broadcast_to(m[:Q_BLOCK], shape),
                                       jnp.broadcast_to(m[Q_BLOCK:], shape))
            l_s[p, dst, :] = jnp.where(head0, jnp.broadcast_to(l[:Q_BLOCK], shape),
                                       jnp.broadcast_to(l[Q_BLOCK:], shape))
            a_s[p, dst, :] = jnp.where(head0, acc[:Q_BLOCK], acc[Q_BLOCK:])

        def body(it, carry):
            sc = [scores(it * ATTN_GROUP + g) for g in range(ATTN_GROUP)]
            sm = [softmax(s) for s, _, _ in sc]
            ac = [weighted(e, v_parts) for (_, e, _), (_, _, v_parts) in zip(sm, sc)]
            for (_, qstart, _), (m, _, l), acc in zip(sc, sm, ac):
                store(qstart, m, l, acc)
            return carry

        lax.fori_loop(0, ATTN_SPAN // (Q_BLOCK * ATTN_GROUP), body, 0)

    def merge(c, carry):
        rows = pl.ds(pl.multiple_of(c * Q_BLOCK, Q_BLOCK), Q_BLOCK)
        ms = [m_s[p, rows, :] for p in range(N_PATTERNS)]
        m_all = functools.reduce(jnp.maximum, ms)
        ws = [jnp.exp(m - m_all) for m in ms]
        num = sum(w * a_s[p, rows, :] for p, w in enumerate(ws))
        den = sum(w * l_s[p, rows, :] for p, w in enumerate(ws))
        o_ref[rows, :] = num / den
        return carry

    lax.fori_loop(0, ATTN_SPAN // Q_BLOCK, merge, 0)


def _attn_prompt(q, k, v, batch, seq):
    nspan = seq // ATTN_SPAN
    blk = (ATTN_SPAN, LANES)
    cur = pl.BlockSpec(blk, lambda b, hp, s: (b * nspan + s, hp))
    prev = pl.BlockSpec(blk, lambda b, hp, s: (b * nspan + jnp.maximum(s - 1, 0), hp))
    acc = pltpu.VMEM((N_PATTERNS, ATTN_SPAN, LANES), F32)
    packed = pltpu.VMEM((2 * ATTN_SPAN, LANES), F32)
    return pl.pallas_call(
        _attn_prompt_kernel,
        grid=(batch, D_A // LANES, nspan),
        in_specs=[cur, prev, cur, prev, cur],
        out_specs=cur,
        out_shape=jax.ShapeDtypeStruct((batch * seq, D_A), F32),
        scratch_shapes=[packed] * 6 + [acc, acc, acc],
        compiler_params=_cparams(("parallel", "parallel", "arbitrary")),
        name="attn_prompt",
    )(q, k, k, v, v)


def _attn_sample_kernel(q_ref, kn_ref, vn_ref, ckt_ref, cvt_ref, o_ref, *, w_buf, t_new):
    pad = LANES - t_new
    zeros = jnp.zeros((pad, D_A), F32)
    kn = jnp.concatenate([kn_ref[...], zeros], axis=0).astype(BF16)
    vn = jnp.concatenate([vn_ref[...], zeros], axis=0).astype(BF16)
    n_rows = N_HEADS_A * t_new
    hrow = lax.broadcasted_iota(jnp.int32, (n_rows, D_A), 0) // t_new
    hlane = lax.broadcasted_iota(jnp.int32, (n_rows, D_A), 1) // HEAD_DIM
    own = hrow == hlane
    q_rep = jnp.concatenate([q_ref[...]] * N_HEADS_A, axis=0)
    q64 = jnp.where(own, q_rep, 0.0).astype(BF16)
    nt = (((1,), (1,)), ((), ()))
    s_w = jnp.dot(q64, ckt_ref[...].astype(BF16), preferred_element_type=F32)
    s_n = lax.dot_general(q64, kn, nt, preferred_element_type=F32)
    vt = cvt_ref[...].astype(BF16)

    def dist(n_cols, first):
        t = lax.broadcasted_iota(jnp.int32, (n_rows, n_cols), 0) % t_new
        return t - lax.broadcasted_iota(jnp.int32, (n_rows, n_cols), 1) - first

    dist_w, dist_n = dist(w_buf, -w_buf), dist(LANES, 0)
    ms, ls, accs = [], [], []
    for w, d in DILATED_PATTERNS:
        ok = lambda ds: (ds >= 0) & (ds <= w) & ((ds & (d - 1)) == 0)
        sw = jnp.where(ok(dist_w), s_w, -jnp.inf)
        sn = jnp.where(ok(dist_n), s_n, -jnp.inf)
        m = jnp.maximum(jnp.max(sw, axis=-1, keepdims=True), jnp.max(sn, axis=-1, keepdims=True))
        ew = jnp.exp(sw - m)
        en = jnp.exp(sn - m)
        ms.append(m)
        ls.append(jnp.sum(ew, axis=-1, keepdims=True) + jnp.sum(en, axis=-1, keepdims=True))
        accs.append(lax.dot_general(ew.astype(BF16), vt, nt, preferred_element_type=F32)
                    + jnp.dot(en.astype(BF16), vn, preferred_element_type=F32))
    m_all = functools.reduce(jnp.maximum, ms)
    ws = [jnp.exp(m - m_all) for m in ms]
    num = sum(w * a for w, a in zip(ws, accs))
    den = sum(w * l for w, l in zip(ws, ls))
    o = jnp.where(own, num / den, 0.0)
    out = o[0:t_new]
    for h in range(1, N_HEADS_A):
        out = out + o[h * t_new:(h + 1) * t_new]
    o_ref[...] = out


def _attn_sample(q, k, v, cache_kt, cache_vt, layer, row0, batch, t_new):
    w_buf = cache_kt.shape[3]
    assert row0 % t_new == 0 and t_new == SUBLANES
    new = pl.BlockSpec((t_new, D_A), lambda b: (row0 // t_new + b, 0))
    cache = pl.BlockSpec((None, None, D_A, w_buf), lambda b: (layer, b, 0, 0))
    return pl.pallas_call(
        functools.partial(_attn_sample_kernel, w_buf=w_buf, t_new=t_new),
        grid=(batch,),
        in_specs=[new, new, new, cache, cache],
        out_specs=pl.BlockSpec((t_new, D_A), lambda b: (b, 0)),
        out_shape=jax.ShapeDtypeStruct((batch * t_new, D_A), F32),
        compiler_params=_cparams(("parallel",)),
        name="attn_sample",
    )(q, k, v, cache_kt, cache_vt)


def _export_kernel(*refs):
    k_ref, v_ref = refs[0], refs[1]
    pk_ref, pv_ref = refs[-2], refs[-1]
    pk_ref[...] = k_ref[...].T
    pv_ref[...] = v_ref[...].T


def _export_window(k, v, prev, layer, depth, batch, seq, keep):
    tm = EXPORT_TILE
    first = (seq - keep) // tm
    src = pl.BlockSpec((tm, D_A), lambda b, j: (b * (seq // tm) + first + j, 0))
    dst = pl.BlockSpec((None, None, D_A, tm), lambda b, j: (layer, b, 0, j))
    shape = jax.ShapeDtypeStruct((depth, batch, D_A, keep), F32)
    carried = [] if prev is None else list(prev)
    return pl.pallas_call(
        _export_kernel,
        grid=(batch, keep // tm),
        in_specs=[src, src] + [pl.BlockSpec(memory_space=pl.ANY)] * len(carried),
        out_specs=[dst, dst],
        out_shape=[shape, shape],
        input_output_aliases={2 + i: i for i in range(len(carried))},
        compiler_params=_cparams(("parallel", "parallel")),
        name="export_window",
    )(k, v, *carried)


def _gelu_tanh(x):
    return 0.5 * x * (1.0 + jnp.tanh(0.7978845608028654 * (x + 0.044715 * x * x * x)))


def _mixer_bc_kernel(bc_ref, cb_ref, h0_ref, ws_ref, bs_ref, cw_ref, cbias_ref, wa_ref, ba_ref,
                     wx_ref, bx_ref, lam_ref, ob_ref, oc_ref, hl_ref, xp_s, h_s, *, tt, last_row):
    j = pl.program_id(1)

    @pl.when(j == 0)
    def _():
        xp_s[0:SUBLANES, :] = cb_ref[...]
        h_s[...] = h0_ref[...]

    nch = tt // CHUNK
    vcat = jnp.concatenate([bc_ref[c * CHUNK:(c + 1) * CHUNK, D_B:2 * D_B] for c in range(nch)],
                           axis=1).astype(BF16)
    ri = lax.broadcasted_iota(jnp.int32, (CHUNK, CHUNK), 0)
    ci = lax.broadcasted_iota(jnp.int32, (CHUNK, CHUNK), 1)
    hl = (lax.broadcasted_iota(jnp.int32, (CHUNK, nch * D_B), 1) % D_B) // HEAD_DIM
    mixed = jnp.zeros((CHUNK, nch * D_B), F32)
    for h in range(N_HEADS_B):
        wh = jnp.where(ri >= ci, ws_ref[h], 0.0).astype(BF16)
        mh = jnp.dot(wh, vcat, preferred_element_type=F32)
        mixed = mixed + jnp.where(hl == h, mh, 0.0)
    for c in range(nch):
        rows = slice(c * CHUNK, (c + 1) * CHUNK)
        ob_ref[rows, :] = bc_ref[rows, 0:D_B] * (mixed[:, c * D_B:(c + 1) * D_B] + bs_ref[...])

    xc = bc_ref[:, 2 * D_B:2 * D_B + D_C]
    xp_s[SUBLANES:SUBLANES + tt, :] = xc
    xconv = cbias_ref[...] + cw_ref[CONV_W - 1:CONV_W, :] * xc
    for kk in range(CONV_W - 1):
        off = SUBLANES - (CONV_W - 1) + kk
        xconv = xconv + cw_ref[kk:kk + 1, :] * xp_s[off:off + tt, :]
    xp_s[0:SUBLANES, :] = xp_s[tt:tt + SUBLANES, :]
    xb = xconv.astype(BF16)
    r = jax.nn.sigmoid(jnp.dot(xb, wa_ref[...], preferred_element_type=F32) + ba_ref[...])
    i = jax.nn.sigmoid(jnp.dot(xb, wx_ref[...], preferred_element_type=F32) + bx_ref[...])
    nl = -lam_ref[...]
    softplus = jnp.maximum(nl, 0.0) + jnp.log1p(jnp.exp(-jnp.abs(nl)))
    a = jnp.exp(-LRU_C * r * softplus)
    b = jnp.sqrt(1.0 - a * a) * (i * xconv)
    rowi = lax.broadcasted_iota(jnp.int32, (tt, D_C), 0)
    step = 1
    while step < tt:
        a_sh = pltpu.roll(a, step, axis=0)
        b_sh = pltpu.roll(b, step, axis=0)
        live = rowi >= step
        b = jnp.where(live, a * b_sh + b, b)
        a = jnp.where(live, a * a_sh, a)
        step *= 2
    h = a * h_s[...] + b
    h_s[...] = h[tt - 1:tt, :]
    oc_ref[...] = h * _gelu_tanh(bc_ref[:, 2 * D_B + D_C:])

    @pl.when(j == pl.num_programs(1) - 1)
    def _():
        hl_ref[...] = h[last_row:last_row + 1, :]


def _mixer_bc(bc, conv_buf8, h0, lw, batch, t_len, tt, last_row):
    nt = t_len // tt
    rows = lambda w: pl.BlockSpec((tt, w), lambda b, j: (b * nt + j, 0))
    per_b = lambda s: pl.BlockSpec((None,) + s, lambda b, j: (b,) + (0,) * len(s))
    return pl.pallas_call(
        functools.partial(_mixer_bc_kernel, tt=tt, last_row=last_row),
        grid=(batch, nt),
        in_specs=[rows(2 * D_B + 2 * D_C), per_b((SUBLANES, D_C)), per_b((1, D_C)),
                  _full((N_HEADS_B, CHUNK, CHUNK)), _full((CHUNK, D_B)), _full((CONV_W, D_C)),
                  _full((1, D_C)), _full((D_C, D_C)), _full((1, D_C)), _full((D_C, D_C)),
                  _full((1, D_C)), _full((1, D_C))],
        out_specs=[rows(D_B), rows(D_C), per_b((1, D_C))],
        out_shape=[jax.ShapeDtypeStruct((batch * t_len, D_B), F32),
                   jax.ShapeDtypeStruct((batch * t_len, D_C), F32),
                   jax.ShapeDtypeStruct((batch, 1, D_C), F32)],
        scratch_shapes=[pltpu.VMEM((tt + SUBLANES, D_C), F32), pltpu.VMEM((1, D_C), F32)],
        compiler_params=_cparams(("parallel", "arbitrary")),
        name="mixer_bc",
    )(bc, conv_buf8, h0, lw["w_s"], lw["bias_s"], lw["conv_w"], lw["conv_b"], lw["w_a_bd"],
      lw["b_a"], lw["w_x_bd"], lw["b_x"], lw["lru_lambda"])


def _out_proj_kernel(oah_ref, oat_ref, obh_ref, obt_ref, och_ref, oct_ref, xh_ref, xt_ref, go_ref, wo_ref,
                     g2_ref, wr_ref, br_ref, xn_ref, h2_ref, idx_ref, gate_ref, rank_ref, cnt_ref, *,
                     n_head_tiles):
    pick = functools.partial(_pick, n_head_tiles=n_head_tiles)
    oa = _rms(pick(oah_ref, oat_ref), go_ref[:, 0:D_A]).astype(BF16)
    ob = _rms(pick(obh_ref, obt_ref), go_ref[:, D_A:D_A + D_B]).astype(BF16)
    oc = _rms(pick(och_ref, oct_ref), go_ref[:, D_A + D_B:]).astype(BF16)
    y = (jnp.dot(oa, wo_ref[0:D_A, :], preferred_element_type=F32)
         + jnp.dot(ob, wo_ref[D_A:D_A + D_B, :], preferred_element_type=F32)
         + jnp.dot(oc, wo_ref[D_A + D_B:, :], preferred_element_type=F32))
    xn = pick(xh_ref, xt_ref) + y
    xn_ref[...] = xn
    hb = _rms(xn, g2_ref[...]).astype(BF16)
    bits = pltpu.bitcast(hb.astype(F32), jnp.uint32)
    half = D_MODEL // 2
    h2_ref[...] = (bits[:, :half] >> 16) | (bits[:, half:] & jnp.uint32(0xFFFF0000))
    logits = jnp.dot(hb, wr_ref[...], preferred_element_type=F32) + br_ref[...]
    lane = lax.broadcasted_iota(jnp.int32, logits.shape, 1).astype(F32)
    cur = logits
    tops, idxs = [], []
    for _ in range(TOP_K):
        m = jnp.max(cur, axis=-1, keepdims=True)
        ix = jnp.min(jnp.where(cur == m, lane, float(ROUTER_PAD)), axis=-1, keepdims=True)
        tops.append(m)
        idxs.append(ix)
        cur = jnp.where(lane == ix, -jnp.inf, cur)
    es = [jnp.exp(t - tops[0]) for t in tops]
    den = sum(es)
    idx_out = jnp.zeros(logits.shape, F32)
    gate_out = jnp.zeros(logits.shape, F32)
    for kk in range(TOP_K):
        idx_out = jnp.where(lane == kk, idxs[kk], idx_out)
        gate_out = jnp.where(lane == kk, es[kk] / den, gate_out)
    idx_ref[...] = idx_out.astype(jnp.int32)
    gate_ref[...] = gate_out
    tm = logits.shape[0]
    chosen = jnp.zeros(logits.shape, F32)
    for kk in range(TOP_K):
        chosen = jnp.where(lane == idxs[kk], 1.0, chosen)
    below = (lax.broadcasted_iota(jnp.int32, (tm, tm), 0)
             > lax.broadcasted_iota(jnp.int32, (tm, tm), 1)).astype(BF16)
    earlier = jnp.dot(below, chosen.astype(BF16), preferred_element_type=F32)
    rank_out = jnp.zeros(logits.shape, F32)
    for kk in range(TOP_K):
        rk = jnp.sum(jnp.where(lane == idxs[kk], earlier, 0.0), axis=-1, keepdims=True)
        rank_out = jnp.where(lane == kk, rk, rank_out)
    rank_ref[...] = rank_out.astype(jnp.int32)
    cnt_ref[...] = jnp.broadcast_to(jnp.sum(chosen, axis=0, keepdims=True),
                                    cnt_ref.shape).astype(jnp.int32)


def _out_proj(oa_pair, ob_pair, oc_pair, x_pair, lw):
    n_head = x_pair[0].shape[0]
    n = n_head + x_pair[1].shape[0]
    tm = TOKEN_TILE
    row = lambda w: pl.BlockSpec((tm, w), lambda i: (i, 0))
    pair = lambda w: _pair_specs(tm, w, n_head)
    return pl.pallas_call(
        functools.partial(_out_proj_kernel, n_head_tiles=n_head // tm),
        grid=(n // tm,),
        in_specs=pair(D_A) + pair(D_B) + pair(D_C) + pair(D_MODEL) + [_full((1, D_MODEL)),
                  _full((D_MODEL, D_MODEL)), _full((1, D_MODEL)), _full((D_MODEL, ROUTER_PAD)),
                  _full((1, ROUTER_PAD))],
        out_specs=[row(D_MODEL), row(D_MODEL // 2), row(ROUTER_PAD), row(ROUTER_PAD), row(ROUTER_PAD),
                   pl.BlockSpec((None, SUBLANES, ROUTER_PAD), lambda i: (i, 0, 0))],
        out_shape=[jax.ShapeDtypeStruct((n, D_MODEL), F32), jax.ShapeDtypeStruct((n, D_MODEL // 2), jnp.uint32),
                   jax.ShapeDtypeStruct((n, ROUTER_PAD), jnp.int32),
                   jax.ShapeDtypeStruct((n, ROUTER_PAD), F32),
                   jax.ShapeDtypeStruct((n, ROUTER_PAD), jnp.int32),
                   jax.ShapeDtypeStruct((n // tm, SUBLANES, ROUTER_PAD), jnp.int32)],
        compiler_params=_cparams(("parallel",)),
        name="out_proj",
    )(*oa_pair, *ob_pair, *oc_pair, *x_pair, lw["g_out"], lw["w_out"], lw["ln2_g"], lw["w_router"],
      lw["b_router"])


def _moe_kernel(te_ref, nu_ref, x_ref, wgu_ref, bg_ref, bl_ref, wd_ref, bd_ref, y_ref, wg_s, wl_s, wd_s):
    i = pl.program_id(0)
    used = i < nu_ref[0]
    new_expert = jnp.logical_or(i == 0, te_ref[i] != te_ref[jnp.maximum(i - 1, 0)])

    @pl.when(jnp.logical_and(used, new_expert))
    def _():
        w2 = 2 * LANES
        src = lax.broadcasted_iota(jnp.int32, (w2, w2), 0)
        col = lax.broadcasted_iota(jnp.int32, (w2, w2), 1)
        pick = (src == jnp.where(col < LANES, 2 * col, 2 * (col - LANES) + 1)).astype(BF16)
        for c in range(2 * D_FF // w2):
            out = jnp.dot(wgu_ref[:, c * w2:(c + 1) * w2].astype(BF16), pick, preferred_element_type=F32)
            wg_s[:, c * LANES:(c + 1) * LANES] = out[:, :LANES].astype(BF16)
            wl_s[:, c * LANES:(c + 1) * LANES] = out[:, LANES:].astype(BF16)
        wd_s[...] = wd_ref[...].astype(BF16)

    @pl.when(used)
    def _():
        u = x_ref[...]
        x = jnp.concatenate([pltpu.bitcast(u << 16, F32).astype(BF16),
                             pltpu.bitcast(u & jnp.uint32(0xFFFF0000), F32).astype(BF16)], axis=1)
        zg = jnp.dot(x, wg_s[...], preferred_element_type=F32) + bg_ref[...]
        zl = jnp.dot(x, wl_s[...], preferred_element_type=F32) + bl_ref[...]
        glu = jnp.minimum(zg, SWIGLU_LIMIT)
        lin = jnp.clip(zl, -SWIGLU_LIMIT, SWIGLU_LIMIT)
        act = glu * jax.nn.sigmoid(SWIGLU_ALPHA * glu) * (lin + 1.0)
        y_ref[...] = jnp.dot(act.astype(BF16), wd_s[...], preferred_element_type=F32) + bd_ref[...]

    @pl.when(i >= nu_ref[0])
    def _():
        y_ref[...] = jnp.zeros(y_ref.shape, F32)


def _moe_ffn(xs, tile_expert, n_used, lw):
    n_slots = xs.shape[0]
    tm = MOE_TILE
    layer = lw["layer"]
    wspec = lambda r, c: pl.BlockSpec((None, r, c), lambda i, te, nu: (te[i], 0, 0))
    wfull = lambda r, c: pl.BlockSpec((None, None, r, c), lambda i, te, nu: (layer, te[i], 0, 0))
    return pl.pallas_call(
        _moe_kernel,
        grid_spec=pltpu.PrefetchScalarGridSpec(
            num_scalar_prefetch=2,
            grid=(n_slots // tm,),
            in_specs=[pl.BlockSpec((tm, D_MODEL // 2), lambda i, te, nu: (i, 0)),
                      wfull(D_MODEL, 2 * D_FF), wspec(1, D_FF), wspec(1, D_FF),
                      wfull(D_FF, D_MODEL), wspec(1, D_MODEL)],
            out_specs=pl.BlockSpec((tm, D_MODEL), lambda i, te, nu: (i, 0)),
            scratch_shapes=[pltpu.VMEM((D_MODEL, D_FF), BF16), pltpu.VMEM((D_MODEL, D_FF), BF16),
                            pltpu.VMEM((D_FF, D_MODEL), BF16)],
        ),
        out_shape=jax.ShapeDtypeStruct((n_slots, D_MODEL), F32),
        compiler_params=_cparams(("arbitrary",)),
        name="moe_ffn",
    )(tile_expert, n_used, xs, lw["w_gu"], lw["b_glu"], lw["b_lin"], lw["w_down"], lw["b_down"])


def _route(idx, rank, counts):
    n_tok = idx.shape[0]
    experts = jnp.arange(N_EXPERTS, dtype=jnp.int32)
    totals = jnp.sum(counts, axis=0)
    tiles = (totals + MOE_TILE - 1) // MOE_TILE
    tile_end = jnp.sum(jnp.where(experts[None, :] <= experts[:, None], tiles[None, :], 0), axis=1)
    pad_start = (tile_end - tiles) * MOE_TILE
    tt = jnp.arange(counts.shape[0], dtype=jnp.int32)
    before = jnp.sum(jnp.where((tt[None, :] < tt[:, None])[:, :, None], counts[None, :, :], 0), axis=1)
    base = pad_start[None, :] + before
    base_tok = jnp.repeat(base, TOKEN_TILE, axis=0)
    picked = idx[:, :, None] == experts[None, None, :]
    dest = jnp.sum(jnp.where(picked, base_tok[:, None, :], 0), axis=-1) + rank
    n_tiles = -(-(n_tok * TOP_K) // MOE_TILE) + N_EXPERTS
    tile_ids = jnp.arange(n_tiles, dtype=jnp.int32)
    tile_expert = jnp.minimum(jnp.sum((tile_end[None, :] <= tile_ids[:, None]).astype(jnp.int32), axis=1),
                              N_EXPERTS - 1)
    n_used = tile_end[-1:].astype(jnp.int32)
    return dest.astype(jnp.int32), tile_expert, n_used, n_tiles


def _sc_mesh():
    return plsc.VectorSubcoreMesh(core_axis_name="core", subcore_axis_name="subcore")


def _load_index_rows(d_hbm, i_vmem, sem, wid, ng, nr):
    def row(r):
        return pltpu.make_async_copy(d_hbm.at[pl.ds(wid + SC_WORKERS * r, 1)], i_vmem.at[pl.ds(r, 1)], sem)
    for r in range(nr):
        pl.when(wid + SC_WORKERS * r < ng)(lambda r=r: row(r).start())
    for r in range(nr):
        pl.when(wid + SC_WORKERS * r < ng)(lambda r=r: row(r).wait())


def _sc_dispatch(h2, dest_g, n_slots):
    ng = dest_g.shape[0]
    width = h2.shape[1]

    @pl.kernel(out_type=jax.ShapeDtypeStruct((n_slots, width), h2.dtype), mesh=_sc_mesh(),
               scratch_types=[pltpu.VMEM((pl.cdiv(ng, SC_WORKERS), LANES), jnp.int32), pltpu.VMEM((SC_GROUP, width), h2.dtype)]
               + [pltpu.SemaphoreType.DMA] * (1 + TOP_K))
    def kernel(h_hbm, d_hbm, o_hbm, i_vmem, buf, isem, *sems):
        wid = lax.axis_index("core") * SC_SUBCORES + lax.axis_index("subcore")
        nr = pl.cdiv(ng, SC_WORKERS)
        _load_index_rows(d_hbm, i_vmem, isem, wid, ng, nr)

        @pl.loop(0, nr)
        def _(r):
            g = wid + SC_WORKERS * r

            @pl.when(g < ng)
            def _():
                pltpu.sync_copy(h_hbm.at[pl.ds(g * SC_GROUP, SC_GROUP)], buf)
                puts = [pltpu.async_copy(buf, o_hbm.at[i_vmem.at[r, pl.ds(k * SC_GROUP, SC_GROUP)]], sems[k])
                        for k in range(TOP_K)]
                for put in puts:
                    put.wait()

    return kernel(h2, dest_g)


def _sc_collect(y, dest_g):
    ng = dest_g.shape[0]

    @pl.kernel(out_type=jax.ShapeDtypeStruct((ng * LANES, D_MODEL), F32), mesh=_sc_mesh(),
               scratch_types=[pltpu.VMEM((pl.cdiv(ng, SC_WORKERS), LANES), jnp.int32)]
               + [pltpu.VMEM((SC_GROUP, D_MODEL), F32)] * 2 + [pltpu.SemaphoreType.DMA] * 5)
    def kernel(y_hbm, d_hbm, o_hbm, i_vmem, buf0, buf1, isem, g0, g1, w0, w1):
        wid = lax.axis_index("core") * SC_SUBCORES + lax.axis_index("subcore")
        bufs, gsem, wsem = (buf0, buf1), (g0, g1), (w0, w1)
        nr = pl.cdiv(ng, SC_WORKERS)
        _load_index_rows(d_hbm, i_vmem, isem, wid, ng, nr)

        @pl.loop(0, nr)
        def _(r):
            g = wid + SC_WORKERS * r

            @pl.when(g < ng)
            def _():
                get = lambda k: pltpu.async_copy(
                    y_hbm.at[i_vmem.at[r, pl.ds(k * SC_GROUP, SC_GROUP)]], bufs[k % 2], gsem[k % 2])
                put = lambda k: pltpu.async_copy(
                    bufs[k % 2], o_hbm.at[pl.ds(g * LANES + k * SC_GROUP, SC_GROUP)], wsem[k % 2])
                gets = [get(0), get(1)]
                puts = []
                for k in range(TOP_K):
                    gets[k].wait()
                    puts.append(put(k))
                    if k + 2 < TOP_K:
                        puts[k].wait()
                        gets.append(get(k + 2))
                for k in range(TOP_K - 2, TOP_K):
                    puts[k].wait()

    return kernel(y, dest_g)


def _combine_kernel(xn_ref, g_ref, y_ref, oh_ref, ot_ref, *, n_head_tiles):
    def emit(o_ref):
        for gi in range(TOKEN_TILE // SC_GROUP):
            rows = slice(gi * SC_GROUP, (gi + 1) * SC_GROUP)
            acc = xn_ref[rows, :]
            for k in range(TOP_K):
                r0 = gi * LANES + k * SC_GROUP
                acc = acc + g_ref[rows, k:k + 1] * y_ref[r0:r0 + SC_GROUP, :]
            o_ref[rows, :] = acc

    pl.when(pl.program_id(0) < n_head_tiles)(lambda: emit(oh_ref))
    pl.when(pl.program_id(0) >= n_head_tiles)(lambda: emit(ot_ref))


def _combine(xn, gates, y4, n_head):
    n = xn.shape[0]
    tm = TOKEN_TILE
    row = lambda r, w: pl.BlockSpec((r, w), lambda i: (i, 0))
    return pl.pallas_call(
        functools.partial(_combine_kernel, n_head_tiles=n_head // tm),
        grid=(n // tm,),
        in_specs=[row(tm, D_MODEL), row(tm, ROUTER_PAD), row(tm * TOP_K, D_MODEL)],
        out_specs=_pair_specs(tm, D_MODEL, n_head),
        out_shape=[jax.ShapeDtypeStruct((n_head, D_MODEL), F32),
                   jax.ShapeDtypeStruct((n - n_head, D_MODEL), F32)],
        compiler_params=_cparams(("arbitrary",)),
        name="moe_combine",
    )(xn, gates, y4)


def _moe(xn, h2, idx, gates, rank, counts, lw, n_head):
    n_tok = xn.shape[0]
    dest, tile_expert, n_used, n_tiles = _route(idx[:, :TOP_K], rank[:, :TOP_K], counts)
    dest_g = dest.reshape(n_tok // SC_GROUP, SC_GROUP, TOP_K).transpose(0, 2, 1).reshape(-1, LANES)
    xs = _sc_dispatch(h2, dest_g, n_tiles * MOE_TILE)
    y = _moe_ffn(xs, tile_expert, n_used, lw)
    return _combine(xn, gates, _sc_collect(y, dest_g), n_head)


def _block_diag(w):
    g, a, b = w.shape
    out = jnp.zeros((g * a, g * b), w.dtype)
    for i in range(g):
        out = out.at[i * a:(i + 1) * a, i * b:(i + 1) * b].set(w[i])
    return out


def _layer_weights(l, p):
    wr = jnp.pad(p["w_router"][l], ((0, 0), (0, ROUTER_PAD - N_EXPERTS)))
    hd = jnp.arange(D_A) // HEAD_DIM
    return {
        "ln1_g": p["ln1_g"][l][None], "w_in": p["w_in"][l].astype(BF16),
        "g_q": jnp.tile(p["g_q"][l], N_HEADS_A)[None], "g_k": jnp.tile(p["g_k"][l], N_HEADS_A)[None],
        "g_vb": p["g_vb"][l][None],
        "ones_bd": (hd[:, None] == hd[None, :]).astype(BF16),
        "w_s": p["w_s"][l],
        "bias_s": jnp.repeat(p["b_s"][l].T, HEAD_DIM, axis=1),
        "conv_w": p["conv_w"][l], "conv_b": p["conv_b"][l][None],
        "w_a_bd": _block_diag(p["w_a"][l]).astype(BF16), "b_a": p["b_a"][l][None],
        "w_x_bd": _block_diag(p["w_x"][l]).astype(BF16), "b_x": p["b_x"][l][None],
        "lru_lambda": p["lru_lambda"][l][None],
        "g_out": p["g_out"][l][None], "w_out": p["w_out"][l].astype(BF16),
        "ln2_g": p["ln2_g"][l][None],
        "w_router": wr.astype(BF16),
        "b_router": jnp.pad(p["b_router"][l], (0, ROUTER_PAD - N_EXPERTS),
                            constant_values=NEG_BIG)[None],
        "layer": l, "w_gu": p["w_gu"], "w_down": p["w_down"],
        "b_glu": p["b_gu"][l][:, None, 0::2], "b_lin": p["b_gu"][l][:, None, 1::2],
        "b_down": p["b_down"][l][:, None, :],
    }


def kernel(x_prompt, x_sample, cache_win_k, cache_win_v, state_conv, state_lru, ln1_g, w_in, g_q, g_k,
           g_vb, w_s, b_s, conv_w, conv_b, w_a, b_a, w_x, b_x, lru_lambda, g_out, w_out, ln2_g, w_router,
           b_router, w_gu, b_gu, w_down, b_down):
    params = dict(ln1_g=ln1_g, w_in=w_in, g_q=g_q, g_k=g_k, g_vb=g_vb, w_s=w_s, b_s=b_s, conv_w=conv_w,
                  conv_b=conv_b, w_a=w_a, b_a=b_a, w_x=w_x, b_x=b_x, lru_lambda=lru_lambda, g_out=g_out,
                  w_out=w_out, ln2_g=ln2_g, w_router=w_router, b_router=b_router, w_gu=w_gu, b_gu=b_gu,
                  w_down=w_down, b_down=b_down)
    bp, sp, _ = x_prompt.shape
    bs, ss, _ = x_sample.shape
    depth = w_in.shape[0]
    n_p, n_s = bp * sp, bs * ss
    keep = min(DILATED_PATTERNS[-1][0], sp)
    w_buf = cache_win_k.shape[2]
    ckt = cache_win_k.transpose(0, 1, 3, 4, 2).reshape(depth, bs, D_A, w_buf)
    cvt = cache_win_v.transpose(0, 1, 3, 4, 2).reshape(depth, bs, D_A, w_buf)
    x = (x_prompt.reshape(n_p, D_MODEL), x_sample.reshape(n_s, D_MODEL))
    zero_conv = jnp.zeros((bp, SUBLANES, D_C), F32)
    zero_h = jnp.zeros((bp, 1, D_C), F32)
    outs = {name: [] for name in ("pconv", "plru", "sk", "sv", "sconv", "slru", "svb")}
    window = None
    for l in range(depth):
        lw = _layer_weights(l, params)
        q, k, v, bc = _in_proj(x, lw["ln1_g"], lw["w_in"], lw["g_q"], lw["g_k"], lw["g_vb"], lw["ones_bd"])
        oa_p = _attn_prompt(q, k, v, bp, sp)
        ob_p, oc_p, h_p = _mixer_bc(bc, zero_conv, zero_h, lw, bp, sp, MIX_TILE, MIX_TILE - 1)
        oa_s = _attn_sample(q, k, v, ckt, cvt, l, n_p, bs, ss)
        bc_s = bc[n_p:].reshape(bs, ss, -1)
        bc_s_pad = jnp.pad(bc_s, ((0, 0), (0, CHUNK - ss), (0, 0))).reshape(bs * CHUNK, -1)
        conv8 = jnp.pad(state_conv[l], ((0, 0), (SUBLANES - (CONV_W - 1), 0), (0, 0)))
        ob_s, oc_s, h_s = _mixer_bc(bc_s_pad, conv8, state_lru[l][:, None, :], lw, bs, CHUNK, CHUNK, ss - 1)
        ob_s = ob_s.reshape(bs, CHUNK, D_B)[:, :ss].reshape(n_s, D_B)
        oc_s = oc_s.reshape(bs, CHUNK, D_C)[:, :ss].reshape(n_s, D_C)
        xn, h2, idx, gates, rank, counts = _out_proj((oa_p, oa_s), (ob_p, ob_s), (oc_p, oc_s), x, lw)
        x = _moe(xn, h2, idx, gates, rank, counts[:, 0, :N_EXPERTS], lw, n_p)

        window = _export_window(k, v, window, l, depth, bp, sp, keep)
        xc_p = bc[:n_p, 2 * D_B:2 * D_B + D_C].reshape(bp, sp, D_C)
        outs["pconv"].append(xc_p[:, sp - (CONV_W - 1):])
        outs["plru"].append(h_p[:, 0])
        outs["sk"].append(k[n_p:].reshape(bs, ss, N_HEADS_A, HEAD_DIM))
        outs["sv"].append(v[n_p:].reshape(bs, ss, N_HEADS_A, HEAD_DIM))
        xpad_s = jnp.concatenate([state_conv[l], bc_s[:, :, 2 * D_B:2 * D_B + D_C]], axis=1)
        outs["sconv"].append(xpad_s[:, -(CONV_W - 1):])
        outs["slru"].append(h_s[:, 0])
        outs["svb"].append(bc_s[:, :, D_B:2 * D_B])
    y_p = x[0].reshape(bp, sp, D_MODEL)
    y_s = x[1].reshape(bs, ss, D_MODEL)
    st = lambda name: jnp.stack(outs[name])
    heads_last = lambda t: t.reshape(depth, bp, N_HEADS_A, HEAD_DIM, keep).transpose(0, 1, 4, 2, 3)
    return (y_p, y_s, heads_last(window[0]), heads_last(window[1]), st("pconv"), st("plru"), st("sk"), st("sv"), st("sconv"),
            st("slru"), st("svb"))
```

```python
import functools

import jax
import jax.numpy as jnp
from jax import lax
from jax.experimental import pallas as pl
from jax.experimental.pallas import tpu as pltpu
from jax.experimental.pallas import tpu_sc as plsc

F32 = jnp.float32
BF16 = jnp.bfloat16

D_MODEL = 1024
HEAD_DIM = 64
N_HEADS_A = 8
D_A = N_HEADS_A * HEAD_DIM
N_HEADS_B = 4
D_B = N_HEADS_B * HEAD_DIM
N_GROUPS_C = 4
D_C = N_GROUPS_C * HEAD_DIM
D_IN = 3 * D_A + 2 * D_B + 2 * D_C
DILATED_PATTERNS = ((128, 1), (512, 4), (2048, 16))
N_PATTERNS = len(DILATED_PATTERNS)
CHUNK = 128
CONV_W = 4
LRU_C = 8.0
N_EXPERTS = 32
TOP_K = 4
D_FF = 1024
SWIGLU_LIMIT = 7.0
SWIGLU_ALPHA = 1.702
EPS = 1e-6
ATTN_SCALE = HEAD_DIM ** -0.5
PAST_LEN = 16384

LANES = 128
SUBLANES = 8
VMEM_LIMIT_BYTES = 56 * 1024 * 1024

Q_BLOCK = 128
ATTN_SPAN = 2048
ATTN_GROUP = 4
ATTN_GROUP_RAW = 2
RAW_KEY_STRIDE = 16
TOKEN_TILE = 256
MIX_TILE = 512
MOE_TILE = 256
ROUTER_PAD = LANES
SC_SUBCORES = 16
SC_WORKERS = 2 * SC_SUBCORES
SC_GROUP = LANES // TOP_K
NEG_BIG = -1e30


def _cparams(semantics):
    return pltpu.CompilerParams(dimension_semantics=semantics,
                                vmem_limit_bytes=VMEM_LIMIT_BYTES)


def _full(shape):
    return pl.BlockSpec(shape, lambda *_: (0,) * len(shape))


def _rms(t, g):
    ms = jnp.mean(t * t, axis=-1, keepdims=True)
    return t * lax.rsqrt(ms + EPS) * g


def _pair_specs(tm, w, n_head):
    nh = n_head // tm
    return [pl.BlockSpec((tm, w), lambda i: (jnp.minimum(i, nh - 1), 0)),
            pl.BlockSpec((tm, w), lambda i: (jnp.maximum(i - nh, 0), 0))]


def _pick(head_ref, tail_ref, n_head_tiles):
    return jnp.where(pl.program_id(0) >= n_head_tiles, tail_ref[...], head_ref[...])


def _split_bf16(t):
    hi = t.astype(BF16)
    lo = (t - hi.astype(F32)).astype(BF16)
    return hi, lo


def _in_proj_kernel(xh_ref, xt_ref, g1_ref, w_ref, gq_ref, gk_ref, gvb_ref, ones_ref,
                    q_ref, k_ref, v_ref, bc_ref, *, n_head_tiles):
    h = _rms(_pick(xh_ref, xt_ref, n_head_tiles), g1_ref[...]).astype(BF16)
    z = jnp.dot(h, w_ref[...], preferred_element_type=F32)

    def head_norm(t, g):
        hi, lo = _split_bf16(t * t)
        ss = (jnp.dot(hi, ones_ref[...], preferred_element_type=F32)
              + jnp.dot(lo, ones_ref[...], preferred_element_type=F32))
        return t * lax.rsqrt(ss * (1.0 / HEAD_DIM) + EPS) * g

    q_ref[...] = head_norm(z[:, 0:D_A], gq_ref[...]) * ATTN_SCALE
    k_ref[...] = head_norm(z[:, D_A:2 * D_A], gk_ref[...])
    v_ref[...] = z[:, 2 * D_A:3 * D_A]
    o = 3 * D_A
    bc_ref[...] = z[:, o:]
    bc_ref[:, D_B:2 * D_B] = _rms(z[:, o + D_B:o + 2 * D_B], gvb_ref[...])


def _in_proj(x_pair, g1, w_bf16, gq, gk, gvb, ones_bd):
    n_head = x_pair[0].shape[0]
    n = n_head + x_pair[1].shape[0]
    tm = TOKEN_TILE
    row = lambda w: pl.BlockSpec((tm, w), lambda i: (i, 0))
    return pl.pallas_call(
        functools.partial(_in_proj_kernel, n_head_tiles=n_head // tm),
        grid=(n // tm,),
        in_specs=_pair_specs(tm, D_MODEL, n_head) + [_full((1, D_MODEL)), _full((D_MODEL, D_IN)), _full((1, D_A)),
                  _full((1, D_A)), _full((1, D_B)), _full((D_A, D_A))],
        out_specs=[row(D_A), row(D_A), row(D_A), row(2 * D_B + 2 * D_C)],
        out_shape=[jax.ShapeDtypeStruct((n, D_A), F32)] * 3
        + [jax.ShapeDtypeStruct((n, 2 * D_B + 2 * D_C), F32)],
        compiler_params=_cparams(("parallel",)),
        name="in_proj",
    )(*x_pair, g1, w_bf16, gq, gk, gvb, ones_bd)


def _attn_prompt_kernel(q_ref, kp_ref, kc_ref, vp_ref, vc_ref, o_ref, kq0a, kq0b, kq1a, kq1b, vqh, vql,
                        m_s, l_s, a_s):
    span = pl.program_id(2)
    qb2 = 2 * Q_BLOCK
    lane = lax.broadcasted_iota(jnp.int32, (Q_BLOCK, LANES), 1)
    head0 = lane < HEAD_DIM
    lane2 = lax.broadcasted_iota(jnp.int32, (qb2, LANES), 1)
    head0_2 = lane2 < HEAD_DIM
    swap = lambda t: pltpu.roll(t, HEAD_DIM, axis=1)

    def pack(k_ref, v_ref, base):
        def body(c, carry):
            src = pl.ds(pl.multiple_of(c * qb2, qb2), qb2)
            dst = pl.ds(pl.multiple_of(base + c * qb2, qb2), qb2)
            k = k_ref[src, :]
            kh = k.astype(BF16).astype(F32)
            kl = k - kh
            kq0a[dst, :] = jnp.where(head0_2, kh, swap(kh))
            kq0b[dst, :] = jnp.where(head0_2, kl, 0.0)
            kq1a[dst, :] = jnp.where(head0_2, swap(kh), kh)
            kq1b[dst, :] = jnp.where(head0_2, swap(kl), 0.0)
            v = v_ref[src, :]
            vh = v.astype(BF16).astype(F32)
            vqh[dst, :] = vh
            vql[dst, :] = v - vh
            return carry
        lax.fori_loop(0, ATTN_SPAN // qb2, body, 0)

    pack(kp_ref, vp_ref, 0)
    pack(kc_ref, vc_ref, ATTN_SPAN)

    row = lax.broadcasted_iota(jnp.int32, (qb2, qb2), 0) & (Q_BLOCK - 1)
    col = lax.broadcasted_iota(jnp.int32, (qb2, qb2), 1)
    band = (col >= row) & (col <= row + Q_BLOCK)
    cur = col >= Q_BLOCK
    nt = (((1,), (1,)), ((), ()))

    def ds(start, size, d):
        return pl.ds(start, size) if d == 1 else pl.ds(start, size, stride=d)

    for p, (_, d) in enumerate(DILATED_PATTERNS):
        nblk = ATTN_SPAN // (Q_BLOCK * d)
        assert d < RAW_KEY_STRIDE or nblk == 1

        def scores(blk, d=d, nblk=nblk):
            r = blk // nblk
            ib = blk % nblk
            qstart = r + d * Q_BLOCK * ib
            if d == 1:
                qstart = pl.multiple_of(Q_BLOCK * blk, Q_BLOCK)
            kstart = ATTN_SPAN + qstart - d * Q_BLOCK
            q = q_ref[ds(qstart, Q_BLOCK, d), :]
            qh = q.astype(BF16).astype(F32)
            ql = q - qh
            lhs0 = jnp.concatenate([jnp.where(head0, qh, swap(ql)), jnp.where(head0, qh, 0.0)], axis=1)
            lhs1 = jnp.concatenate([jnp.where(head0, swap(qh), ql), jnp.where(head0, swap(qh), 0.0)], axis=1)
            if d < RAW_KEY_STRIDE:
                keys = ds(kstart, qb2, d)
                k0 = jnp.concatenate([kq0a[keys, :], kq0b[keys, :]], axis=1).astype(BF16)
                k1 = jnp.concatenate([kq1a[keys, :], kq1b[keys, :]], axis=1).astype(BF16)
                vh, vl = vqh[keys, :], vql[keys, :]
            else:
                both = lambda p_ref, c_ref: jnp.concatenate(
                    [p_ref[ds(r, Q_BLOCK, d), :], c_ref[ds(r, Q_BLOCK, d), :]], axis=0)
                k = both(kp_ref, kc_ref)
                kh = k.astype(BF16).astype(F32)
                kl = k - kh
                k0 = jnp.concatenate([jnp.where(head0_2, kh, swap(kh)), jnp.where(head0_2, kl, 0.0)],
                                     axis=1).astype(BF16)
                k1 = jnp.concatenate([jnp.where(head0_2, swap(kh), kh), jnp.where(head0_2, swap(kl), 0.0)],
                                     axis=1).astype(BF16)
                v = both(vp_ref, vc_ref)
                vh = v.astype(BF16).astype(F32)
                vl = v - vh
            s0 = lax.dot_general(lhs0.astype(BF16), k0, nt, preferred_element_type=F32)
            s1 = lax.dot_general(lhs1.astype(BF16), k1, nt, preferred_element_type=F32)
            prev_ok = jnp.logical_or(ib > 0, span > 0)
            s = jnp.concatenate([s0, s1], axis=0)
            return jnp.where(band & (cur | prev_ok), s, -jnp.inf), qstart, (vh, vl)

        def softmax(s):
            m = jnp.max(s, axis=-1, keepdims=True)
            e = jnp.exp(s - m)
            return m, e, jnp.sum(e, axis=-1, keepdims=True)

        def weighted(e, v_parts):
            eh, el = _split_bf16(e)
            vh, vl = v_parts
            rhs = jnp.concatenate([jnp.concatenate([vh, vl], axis=1),
                                   jnp.concatenate([vh, jnp.zeros_like(vh)], axis=1)], axis=0).astype(BF16)
            out = jnp.dot(jnp.concatenate([eh, el], axis=1), rhs, preferred_element_type=F32)
            return out[:, :LANES] + out[:, LANES:]

        def store(qstart, m, l, acc, p=p, d=d):
            dst = ds(qstart, Q_BLOCK, d)
            shape = (Q_BLOCK, LANES)
            m_s[p, dst, :] = jnp.where(head0, jnp.broadcast_to(m[:Q_BLOCK], shape),
                                       jnp.broadcast_to(m[Q_BLOCK:], shape))
            l_s[p, dst, :] = jnp.where(head0, jnp.broadcast_to(l[:Q_BLOCK], shape),
                                       jnp.broadcast_to(l[Q_BLOCK:], shape))
            a_s[p, dst, :] = jnp.where(head0, acc[:Q_BLOCK], acc[Q_BLOCK:])

        group = ATTN_GROUP if d < RAW_KEY_STRIDE else ATTN_GROUP_RAW

        def body(it, carry, group=group):
            sc = [scores(it * group + g) for g in range(group)]
            sm = [softmax(s) for s, _, _ in sc]
            ac = [weighted(e, v_parts) for (_, e, _), (_, _, v_parts) in zip(sm, sc)]
            for (_, qstart, _), (m, _, l), acc in zip(sc, sm, ac):
                store(qstart, m, l, acc)
            return carry

        lax.fori_loop(0, ATTN_SPAN // (Q_BLOCK * group), body, 0)

    def merge(c, carry):
        rows = pl.ds(pl.multiple_of(c * Q_BLOCK, Q_BLOCK), Q_BLOCK)
        ms = [m_s[p, rows, :] for p in range(N_PATTERNS)]
        m_all = functools.reduce(jnp.maximum, ms)
        ws = [jnp.exp(m - m_all) for m in ms]
        num = sum(w * a_s[p, rows, :] for p, w in enumerate(ws))
        den = sum(w * l_s[p, rows, :] for p, w in enumerate(ws))
        o_ref[rows, :] = num / den
        return carry

    lax.fori_loop(0, ATTN_SPAN // Q_BLOCK, merge, 0)


def _attn_prompt(q, k, v, batch, seq):
    nspan = seq // ATTN_SPAN
    blk = (ATTN_SPAN, LANES)
    cur = pl.BlockSpec(blk, lambda b, hp, s: (b * nspan + s, hp))
    prev = pl.BlockSpec(blk, lambda b, hp, s: (b * nspan + jnp.maximum(s - 1, 0), hp))
    acc = pltpu.VMEM((N_PATTERNS, ATTN_SPAN, LANES), F32)
    packed = pltpu.VMEM((2 * ATTN_SPAN, LANES), F32)
    return pl.pallas_call(
        _attn_prompt_kernel,
        grid=(batch, D_A // LANES, nspan),
        in_specs=[cur, prev, cur, prev, cur],
        out_specs=cur,
        out_shape=jax.ShapeDtypeStruct((batch * seq, D_A), F32),
        scratch_shapes=[packed] * 6 + [acc, acc, acc],
        compiler_params=_cparams(("parallel", "parallel", "arbitrary")),
        name="attn_prompt",
    )(q, k, k, v, v)


def _attn_sample_kernel(q_ref, kn_ref, vn_ref, ckt_ref, cvt_ref, o_ref, *, w_buf, t_new):
    pad = LANES - t_new
    zeros = jnp.zeros((pad, D_A), F32)
    kn = jnp.concatenate([kn_ref[...], zeros], axis=0).astype(BF16)
    vn = jnp.concatenate([vn_ref[...], zeros], axis=0).astype(BF16)
    n_rows = N_HEADS_A * t_new
    hrow = lax.broadcasted_iota(jnp.int32, (n_rows, D_A), 0) // t_new
    hlane = lax.broadcasted_iota(jnp.int32, (n_rows, D_A), 1) // HEAD_DIM
    own = hrow == hlane
    q_rep = jnp.concatenate([q_ref[...]] * N_HEADS_A, axis=0)
    q64 = jnp.where(own, q_rep, 0.0).astype(BF16)
    nt = (((1,), (1,)), ((), ()))
    s_w = jnp.dot(q64, ckt_ref[...].astype(BF16), preferred_element_type=F32)
    s_n = lax.dot_general(q64, kn, nt, preferred_element_type=F32)
    vt = cvt_ref[...].astype(BF16)

    def dist(n_cols, first):
        t = lax.broadcasted_iota(jnp.int32, (n_rows, n_cols), 0) % t_new
        return t - lax.broadcasted_iota(jnp.int32, (n_rows, n_cols), 1) - first

    dist_w, dist_n = dist(w_buf, -w_buf), dist(LANES, 0)
    ms, ls, accs = [], [], []
    for w, d in DILATED_PATTERNS:
        ok = lambda ds: (ds >= 0) & (ds <= w) & ((ds & (d - 1)) == 0)
        sw = jnp.where(ok(dist_w), s_w, -jnp.inf)
        sn = jnp.where(ok(dist_n), s_n, -jnp.inf)
        m = jnp.maximum(jnp.max(sw, axis=-1, keepdims=True), jnp.max(sn, axis=-1, keepdims=True))
        ew = jnp.exp(sw - m)
        en = jnp.exp(sn - m)
        ms.append(m)
        ls.append(jnp.sum(ew, axis=-1, keepdims=True) + jnp.sum(en, axis=-1, keepdims=True))
        accs.append(lax.dot_general(ew.astype(BF16), vt, nt, preferred_element_type=F32)
                    + jnp.dot(en.astype(BF16), vn, preferred_element_type=F32))
    m_all = functools.reduce(jnp.maximum, ms)
    ws = [jnp.exp(m - m_all) for m in ms]
    num = sum(w * a for w, a in zip(ws, accs))
    den = sum(w * l for w, l in zip(ws, ls))
    o = jnp.where(own, num / den, 0.0)
    out = o[0:t_new]
    for h in range(1, N_HEADS_A):
        out = out + o[h * t_new:(h + 1) * t_new]
    o_ref[...] = out


def _attn_sample(q, k, v, cache_kt, cache_vt, layer, row0, batch, t_new):
    w_buf = cache_kt.shape[3]
    assert row0 % t_new == 0 and t_new == SUBLANES
    new = pl.BlockSpec((t_new, D_A), lambda b: (row0 // t_new + b, 0))
    cache = pl.BlockSpec((None, None, D_A, w_buf), lambda b: (layer, b, 0, 0))
    return pl.pallas_call(
        functools.partial(_attn_sample_kernel, w_buf=w_buf, t_new=t_new),
        grid=(batch,),
        in_specs=[new, new, new, cache, cache],
        out_specs=pl.BlockSpec((t_new, D_A), lambda b: (b, 0)),
        out_shape=jax.ShapeDtypeStruct((batch * t_new, D_A), F32),
        compiler_params=_cparams(("parallel",)),
        name="attn_sample",
    )(q, k, v, cache_kt, cache_vt)


def _export_kernel(*refs):
    k_ref, v_ref = refs[0], refs[1]
    pk_ref, pv_ref = refs[-2], refs[-1]
    pk_ref[...] = k_ref[...].T
    pv_ref[...] = v_ref[...].T


def _export_window(k, v, prev, layer, depth, batch, seq, keep):
    tm = TOKEN_TILE
    first = (seq - keep) // tm
    src = pl.BlockSpec((tm, D_A), lambda b, j: (b * (seq // tm) + first + j, 0))
    dst = pl.BlockSpec((None, None, D_A, tm), lambda b, j: (layer, b, 0, j))
    shape = jax.ShapeDtypeStruct((depth, batch, D_A, keep), F32)
    carried = [] if prev is None else list(prev)
    return pl.pallas_call(
        _export_kernel,
        grid=(batch, keep // tm),
        in_specs=[src, src] + [pl.BlockSpec(memory_space=pl.ANY)] * len(carried),
        out_specs=[dst, dst],
        out_shape=[shape, shape],
        input_output_aliases={2 + i: i for i in range(len(carried))},
        compiler_params=_cparams(("parallel", "parallel")),
        name="export_window",
    )(k, v, *carried)


def _gelu_tanh(x):
    return 0.5 * x * (1.0 + jnp.tanh(0.7978845608028654 * (x + 0.044715 * x * x * x)))


def _mixer_bc_kernel(bc_ref, cb_ref, h0_ref, ws_ref, bs_ref, cw_ref, cbias_ref, wa_ref, ba_ref,
                     wx_ref, bx_ref, lam_ref, ob_ref, oc_ref, hl_ref, xp_s, h_s, *, tt, last_row):
    j = pl.program_id(1)

    @pl.when(j == 0)
    def _():
        xp_s[0:SUBLANES, :] = cb_ref[...]
        h_s[...] = h0_ref[...]

    nch = tt // CHUNK
    vcat = jnp.concatenate([bc_ref[c * CHUNK:(c + 1) * CHUNK, D_B:2 * D_B] for c in range(nch)],
                           axis=1).astype(BF16)
    ri = lax.broadcasted_iota(jnp.int32, (CHUNK, CHUNK), 0)
    ci = lax.broadcasted_iota(jnp.int32, (CHUNK, CHUNK), 1)
    hl = (lax.broadcasted_iota(jnp.int32, (CHUNK, nch * D_B), 1) % D_B) // HEAD_DIM
    mixed = jnp.zeros((CHUNK, nch * D_B), F32)
    for h in range(N_HEADS_B):
        wh = jnp.where(ri >= ci, ws_ref[h], 0.0).astype(BF16)
        mh = jnp.dot(wh, vcat, preferred_element_type=F32)
        mixed = mixed + jnp.where(hl == h, mh, 0.0)
    for c in range(nch):
        rows = slice(c * CHUNK, (c + 1) * CHUNK)
        ob_ref[rows, :] = bc_ref[rows, 0:D_B] * (mixed[:, c * D_B:(c + 1) * D_B] + bs_ref[...])

    xc = bc_ref[:, 2 * D_B:2 * D_B + D_C]
    xp_s[SUBLANES:SUBLANES + tt, :] = xc
    xconv = cbias_ref[...] + cw_ref[CONV_W - 1:CONV_W, :] * xc
    for kk in range(CONV_W - 1):
        off = SUBLANES - (CONV_W - 1) + kk
        xconv = xconv + cw_ref[kk:kk + 1, :] * xp_s[off:off + tt, :]
    xp_s[0:SUBLANES, :] = xp_s[tt:tt + SUBLANES, :]
    xb = xconv.astype(BF16)
    r = jax.nn.sigmoid(jnp.dot(xb, wa_ref[...], preferred_element_type=F32) + ba_ref[...])
    i = jax.nn.sigmoid(jnp.dot(xb, wx_ref[...], preferred_element_type=F32) + bx_ref[...])
    nl = -lam_ref[...]
    softplus = jnp.maximum(nl, 0.0) + jnp.log1p(jnp.exp(-jnp.abs(nl)))
    a = jnp.exp(-LRU_C * r * softplus)
    b = jnp.sqrt(1.0 - a * a) * (i * xconv)
    rowi = lax.broadcasted_iota(jnp.int32, (tt, D_C), 0)
    step = 1
    while step < tt:
        a_sh = pltpu.roll(a, step, axis=0)
        b_sh = pltpu.roll(b, step, axis=0)
        live = rowi >= step
        b = jnp.where(live, a * b_sh + b, b)
        a = jnp.where(live, a * a_sh, a)
        step *= 2
    h = a * h_s[...] + b
    h_s[...] = h[tt - 1:tt, :]
    oc_ref[...] = h * _gelu_tanh(bc_ref[:, 2 * D_B + D_C:])

    @pl.when(j == pl.num_programs(1) - 1)
    def _():
        hl_ref[...] = h[last_row:last_row + 1, :]


def _mixer_bc(bc, conv_buf8, h0, lw, batch, t_len, tt, last_row):
    nt = t_len // tt
    rows = lambda w: pl.BlockSpec((tt, w), lambda b, j: (b * nt + j, 0))
    per_b = lambda s: pl.BlockSpec((None,) + s, lambda b, j: (b,) + (0,) * len(s))
    return pl.pallas_call(
        functools.partial(_mixer_bc_kernel, tt=tt, last_row=last_row),
        grid=(batch, nt),
        in_specs=[rows(2 * D_B + 2 * D_C), per_b((SUBLANES, D_C)), per_b((1, D_C)),
                  _full((N_HEADS_B, CHUNK, CHUNK)), _full((CHUNK, D_B)), _full((CONV_W, D_C)),
                  _full((1, D_C)), _full((D_C, D_C)), _full((1, D_C)), _full((D_C, D_C)),
                  _full((1, D_C)), _full((1, D_C))],
        out_specs=[rows(D_B), rows(D_C), per_b((1, D_C))],
        out_shape=[jax.ShapeDtypeStruct((batch * t_len, D_B), F32),
                   jax.ShapeDtypeStruct((batch * t_len, D_C), F32),
                   jax.ShapeDtypeStruct((batch, 1, D_C), F32)],
        scratch_shapes=[pltpu.VMEM((tt + SUBLANES, D_C), F32), pltpu.VMEM((1, D_C), F32)],
        compiler_params=_cparams(("parallel", "arbitrary")),
        name="mixer_bc",
    )(bc, conv_buf8, h0, lw["w_s"], lw["bias_s"], lw["conv_w"], lw["conv_b"], lw["w_a_bd"],
      lw["b_a"], lw["w_x_bd"], lw["b_x"], lw["lru_lambda"])


def _out_proj_kernel(oah_ref, oat_ref, obh_ref, obt_ref, och_ref, oct_ref, xh_ref, xt_ref, go_ref, wo_ref,
                     g2_ref, wr_ref, br_ref, xn_ref, h2_ref, idx_ref, gate_ref, rank_ref, cnt_ref, *,
                     n_head_tiles):
    pick = functools.partial(_pick, n_head_tiles=n_head_tiles)
    oa = _rms(pick(oah_ref, oat_ref), go_ref[:, 0:D_A]).astype(BF16)
    ob = _rms(pick(obh_ref, obt_ref), go_ref[:, D_A:D_A + D_B]).astype(BF16)
    oc = _rms(pick(och_ref, oct_ref), go_ref[:, D_A + D_B:]).astype(BF16)
    y = (jnp.dot(oa, wo_ref[0:D_A, :], preferred_element_type=F32)
         + jnp.dot(ob, wo_ref[D_A:D_A + D_B, :], preferred_element_type=F32)
         + jnp.dot(oc, wo_ref[D_A + D_B:, :], preferred_element_type=F32))
    xn = pick(xh_ref, xt_ref) + y
    xn_ref[...] = xn
    hb = _rms(xn, g2_ref[...]).astype(BF16)
    bits = pltpu.bitcast(hb.astype(F32), jnp.uint32)
    half = D_MODEL // 2
    h2_ref[...] = (bits[:, :half] >> 16) | (bits[:, half:] & jnp.uint32(0xFFFF0000))
    logits = jnp.dot(hb, wr_ref[...], preferred_element_type=F32) + br_ref[...]
    lane = lax.broadcasted_iota(jnp.int32, logits.shape, 1).astype(F32)
    cur = logits
    tops, idxs = [], []
    for _ in range(TOP_K):
        m = jnp.max(cur, axis=-1, keepdims=True)
        ix = jnp.min(jnp.where(cur == m, lane, float(ROUTER_PAD)), axis=-1, keepdims=True)
        tops.append(m)
        idxs.append(ix)
        cur = jnp.where(lane == ix, -jnp.inf, cur)
    es = [jnp.exp(t - tops[0]) for t in tops]
    den = sum(es)
    idx_out = jnp.zeros(logits.shape, F32)
    gate_out = jnp.zeros(logits.shape, F32)
    for kk in range(TOP_K):
        idx_out = jnp.where(lane == kk, idxs[kk], idx_out)
        gate_out = jnp.where(lane == kk, es[kk] / den, gate_out)
    idx_ref[...] = idx_out.astype(jnp.int32)
    gate_ref[...] = gate_out
    tm = logits.shape[0]
    chosen = jnp.zeros(logits.shape, F32)
    for kk in range(TOP_K):
        chosen = jnp.where(lane == idxs[kk], 1.0, chosen)
    below = (lax.broadcasted_iota(jnp.int32, (tm, tm), 0)
             > lax.broadcasted_iota(jnp.int32, (tm, tm), 1)).astype(BF16)
    earlier = jnp.dot(below, chosen.astype(BF16), preferred_element_type=F32)
    rank_out = jnp.zeros(logits.shape, F32)
    for kk in range(TOP_K):
        rk = jnp.sum(jnp.where(lane == idxs[kk], earlier, 0.0), axis=-1, keepdims=True)
        rank_out = jnp.where(lane == kk, rk, rank_out)
    rank_ref[...] = rank_out.astype(jnp.int32)
    cnt_ref[...] = jnp.broadcast_to(jnp.sum(chosen, axis=0, keepdims=True),
                                    cnt_ref.shape).astype(jnp.int32)


def _out_proj(oa_pair, ob_pair, oc_pair, x_pair, lw):
    n_head = x_pair[0].shape[0]
    n = n_head + x_pair[1].shape[0]
    tm = TOKEN_TILE
    row = lambda w: pl.BlockSpec((tm, w), lambda i: (i, 0))
    pair = lambda w: _pair_specs(tm, w, n_head)
    return pl.pallas_call(
        functools.partial(_out_proj_kernel, n_head_tiles=n_head // tm),
        grid=(n // tm,),
        in_specs=pair(D_A) + pair(D_B) + pair(D_C) + pair(D_MODEL) + [_full((1, D_MODEL)),
                  _full((D_MODEL, D_MODEL)), _full((1, D_MODEL)), _full((D_MODEL, ROUTER_PAD)),
                  _full((1, ROUTER_PAD))],
        out_specs=[row(D_MODEL), row(D_MODEL // 2), row(ROUTER_PAD), row(ROUTER_PAD), row(ROUTER_PAD),
                   pl.BlockSpec((None, SUBLANES, ROUTER_PAD), lambda i: (i, 0, 0))],
        out_shape=[jax.ShapeDtypeStruct((n, D_MODEL), F32), jax.ShapeDtypeStruct((n, D_MODEL // 2), jnp.uint32),
                   jax.ShapeDtypeStruct((n, ROUTER_PAD), jnp.int32),
                   jax.ShapeDtypeStruct((n, ROUTER_PAD), F32),
                   jax.ShapeDtypeStruct((n, ROUTER_PAD), jnp.int32),
                   jax.ShapeDtypeStruct((n // tm, SUBLANES, ROUTER_PAD), jnp.int32)],
        compiler_params=_cparams(("parallel",)),
        name="out_proj",
    )(*oa_pair, *ob_pair, *oc_pair, *x_pair, lw["g_out"], lw["w_out"], lw["ln2_g"], lw["w_router"],
      lw["b_router"])


def _moe_kernel(te_ref, nu_ref, x_ref, wgu_ref, bg_ref, bl_ref, wd_ref, bd_ref, y_ref, wg_s, wl_s, wd_s):
    i = pl.program_id(0)
    used = i < nu_ref[0]
    new_expert = jnp.logical_or(i == 0, te_ref[i] != te_ref[jnp.maximum(i - 1, 0)])

    @pl.when(jnp.logical_and(used, new_expert))
    def _():
        w2 = 2 * LANES
        src = lax.broadcasted_iota(jnp.int32, (w2, w2), 0)
        col = lax.broadcasted_iota(jnp.int32, (w2, w2), 1)
        pick = (src == jnp.where(col < LANES, 2 * col, 2 * (col - LANES) + 1)).astype(BF16)
        for c in range(2 * D_FF // w2):
            out = jnp.dot(wgu_ref[:, c * w2:(c + 1) * w2].astype(BF16), pick, preferred_element_type=F32)
            wg_s[:, c * LANES:(c + 1) * LANES] = out[:, :LANES].astype(BF16)
            wl_s[:, c * LANES:(c + 1) * LANES] = out[:, LANES:].astype(BF16)
        wd_s[...] = wd_ref[...].astype(BF16)

    @pl.when(used)
    def _():
        u = x_ref[...]
        x = jnp.concatenate([pltpu.bitcast(u << 16, F32).astype(BF16),
                             pltpu.bitcast(u & jnp.uint32(0xFFFF0000), F32).astype(BF16)], axis=1)
        zg = jnp.dot(x, wg_s[...], preferred_element_type=F32) + bg_ref[...]
        zl = jnp.dot(x, wl_s[...], preferred_element_type=F32) + bl_ref[...]
        glu = jnp.minimum(zg, SWIGLU_LIMIT)
        lin = jnp.clip(zl, -SWIGLU_LIMIT, SWIGLU_LIMIT)
        act = glu * jax.nn.sigmoid(SWIGLU_ALPHA * glu) * (lin + 1.0)
        y_ref[...] = jnp.dot(act.astype(BF16), wd_s[...], preferred_element_type=F32) + bd_ref[...]

    @pl.when(i >= nu_ref[0])
    def _():
        y_ref[...] = jnp.zeros(y_ref.shape, F32)


def _moe_ffn(xs, tile_expert, n_used, lw):
    n_slots = xs.shape[0]
    tm = MOE_TILE
    layer = lw["layer"]
    wspec = lambda r, c: pl.BlockSpec((None, r, c), lambda i, te, nu: (te[i], 0, 0))
    wfull = lambda r, c: pl.BlockSpec((None, None, r, c), lambda i, te, nu: (layer, te[i], 0, 0))
    return pl.pallas_call(
        _moe_kernel,
        grid_spec=pltpu.PrefetchScalarGridSpec(
            num_scalar_prefetch=2,
            grid=(n_slots // tm,),
            in_specs=[pl.BlockSpec((tm, D_MODEL // 2), lambda i, te, nu: (i, 0)),
                      wfull(D_MODEL, 2 * D_FF), wspec(1, D_FF), wspec(1, D_FF),
                      wfull(D_FF, D_MODEL), wspec(1, D_MODEL)],
            out_specs=pl.BlockSpec((tm, D_MODEL), lambda i, te, nu: (i, 0)),
            scratch_shapes=[pltpu.VMEM((D_MODEL, D_FF), BF16), pltpu.VMEM((D_MODEL, D_FF), BF16),
                            pltpu.VMEM((D_FF, D_MODEL), BF16)],
        ),
        out_shape=jax.ShapeDtypeStruct((n_slots, D_MODEL), F32),
        compiler_params=_cparams(("arbitrary",)),
        name="moe_ffn",
    )(tile_expert, n_used, xs, lw["w_gu"], lw["b_glu"], lw["b_lin"], lw["w_down"], lw["b_down"])


def _route(idx, rank, counts):
    n_tok = idx.shape[0]
    experts = jnp.arange(N_EXPERTS, dtype=jnp.int32)
    totals = jnp.sum(counts, axis=0)
    tiles = (totals + MOE_TILE - 1) // MOE_TILE
    tile_end = jnp.sum(jnp.where(experts[None, :] <= experts[:, None], tiles[None, :], 0), axis=1)
    pad_start = (tile_end - tiles) * MOE_TILE
    tt = jnp.arange(counts.shape[0], dtype=jnp.int32)
    before = jnp.sum(jnp.where((tt[None, :] < tt[:, None])[:, :, None], counts[None, :, :], 0), axis=1)
    base = pad_start[None, :] + before
    base_tok = jnp.repeat(base, TOKEN_TILE, axis=0)
    picked = idx[:, :, None] == experts[None, None, :]
    dest = jnp.sum(jnp.where(picked, base_tok[:, None, :], 0), axis=-1) + rank
    n_tiles = -(-(n_tok * TOP_K) // MOE_TILE) + N_EXPERTS
    tile_ids = jnp.arange(n_tiles, dtype=jnp.int32)
    tile_expert = jnp.minimum(jnp.sum((tile_end[None, :] <= tile_ids[:, None]).astype(jnp.int32), axis=1),
                              N_EXPERTS - 1)
    n_used = tile_end[-1:].astype(jnp.int32)
    return dest.astype(jnp.int32), tile_expert, n_used, n_tiles


def _sc_mesh():
    return plsc.VectorSubcoreMesh(core_axis_name="core", subcore_axis_name="subcore")


def _load_index_rows(d_hbm, i_vmem, sem, wid, ng, nr):
    def row(r):
        return pltpu.make_async_copy(d_hbm.at[pl.ds(wid + SC_WORKERS * r, 1)], i_vmem.at[pl.ds(r, 1)], sem)
    for r in range(nr):
        pl.when(wid + SC_WORKERS * r < ng)(lambda r=r: row(r).start())
    for r in range(nr):
        pl.when(wid + SC_WORKERS * r < ng)(lambda r=r: row(r).wait())


def _sc_dispatch(h2, dest_g, n_slots):
    ng = dest_g.shape[0]
    width = h2.shape[1]

    @pl.kernel(out_type=jax.ShapeDtypeStruct((n_slots, width), h2.dtype), mesh=_sc_mesh(),
               scratch_types=[pltpu.VMEM((pl.cdiv(ng, SC_WORKERS), LANES), jnp.int32), pltpu.VMEM((SC_GROUP, width), h2.dtype)]
               + [pltpu.SemaphoreType.DMA] * (1 + TOP_K))
    def kernel(h_hbm, d_hbm, o_hbm, i_vmem, buf, isem, *sems):
        wid = lax.axis_index("core") * SC_SUBCORES + lax.axis_index("subcore")
        nr = pl.cdiv(ng, SC_WORKERS)
        _load_index_rows(d_hbm, i_vmem, isem, wid, ng, nr)

        @pl.loop(0, nr)
        def _(r):
            g = wid + SC_WORKERS * r

            @pl.when(g < ng)
            def _():
                pltpu.sync_copy(h_hbm.at[pl.ds(g * SC_GROUP, SC_GROUP)], buf)
                puts = [pltpu.async_copy(buf, o_hbm.at[i_vmem.at[r, pl.ds(k * SC_GROUP, SC_GROUP)]], sems[k])
                        for k in range(TOP_K)]
                for put in puts:
                    put.wait()

    return kernel(h2, dest_g)


def _sc_collect(y, dest_g):
    ng = dest_g.shape[0]

    @pl.kernel(out_type=jax.ShapeDtypeStruct((ng * LANES, D_MODEL), F32), mesh=_sc_mesh(),
               scratch_types=[pltpu.VMEM((pl.cdiv(ng, SC_WORKERS), LANES), jnp.int32)]
               + [pltpu.VMEM((SC_GROUP, D_MODEL), F32)] * 2 + [pltpu.SemaphoreType.DMA] * 5)
    def kernel(y_hbm, d_hbm, o_hbm, i_vmem, buf0, buf1, isem, g0, g1, w0, w1):
        wid = lax.axis_index("core") * SC_SUBCORES + lax.axis_index("subcore")
        bufs, gsem, wsem = (buf0, buf1), (g0, g1), (w0, w1)
        nr = pl.cdiv(ng, SC_WORKERS)
        _load_index_rows(d_hbm, i_vmem, isem, wid, ng, nr)

        @pl.loop(0, nr)
        def _(r):
            g = wid + SC_WORKERS * r

            @pl.when(g < ng)
            def _():
                get = lambda k: pltpu.async_copy(
                    y_hbm.at[i_vmem.at[r, pl.ds(k * SC_GROUP, SC_GROUP)]], bufs[k % 2], gsem[k % 2])
                put = lambda k: pltpu.async_copy(
                    bufs[k % 2], o_hbm.at[pl.ds(g * LANES + k * SC_GROUP, SC_GROUP)], wsem[k % 2])
                gets = [get(0), get(1)]
                puts = []
                for k in range(TOP_K):
                    gets[k].wait()
                    puts.append(put(k))
                    if k + 2 < TOP_K:
                        puts[k].wait()
                        gets.append(get(k + 2))
                for k in range(TOP_K - 2, TOP_K):
                    puts[k].wait()

    return kernel(y, dest_g)


def _combine_kernel(xn_ref, g_ref, y_ref, oh_ref, ot_ref, *, n_head_tiles):
    def emit(o_ref):
        for gi in range(TOKEN_TILE // SC_GROUP):
            rows = slice(gi * SC_GROUP, (gi + 1) * SC_GROUP)
            acc = xn_ref[rows, :]
            for k in range(TOP_K):
                r0 = gi * LANES + k * SC_GROUP
                acc = acc + g_ref[rows, k:k + 1] * y_ref[r0:r0 + SC_GROUP, :]
            o_ref[rows, :] = acc

    pl.when(pl.program_id(0) < n_head_tiles)(lambda: emit(oh_ref))
    pl.when(pl.program_id(0) >= n_head_tiles)(lambda: emit(ot_ref))


def _combine(xn, gates, y4, n_head):
    n = xn.shape[0]
    tm = TOKEN_TILE
    row = lambda r, w: pl.BlockSpec((r, w), lambda i: (i, 0))
    return pl.pallas_call(
        functools.partial(_combine_kernel, n_head_tiles=n_head // tm),
        grid=(n // tm,),
        in_specs=[row(tm, D_MODEL), row(tm, ROUTER_PAD), row(tm * TOP_K, D_MODEL)],
        out_specs=_pair_specs(tm, D_MODEL, n_head),
        out_shape=[jax.ShapeDtypeStruct((n_head, D_MODEL), F32),
                   jax.ShapeDtypeStruct((n - n_head, D_MODEL), F32)],
        compiler_params=_cparams(("arbitrary",)),
        name="moe_combine",
    )(xn, gates, y4)


def _moe(xn, h2, idx, gates, rank, counts, lw, n_head):
    n_tok = xn.shape[0]
    dest, tile_expert, n_used, n_tiles = _route(idx[:, :TOP_K], rank[:, :TOP_K], counts)
    dest_g = dest.reshape(n_tok // SC_GROUP, SC_GROUP, TOP_K).transpose(0, 2, 1).reshape(-1, LANES)
    xs = _sc_dispatch(h2, dest_g, n_tiles * MOE_TILE)
    y = _moe_ffn(xs, tile_expert, n_used, lw)
    return _combine(xn, gates, _sc_collect(y, dest_g), n_head)


def _block_diag(w):
    g, a, b = w.shape
    out = jnp.zeros((g * a, g * b), w.dtype)
    for i in range(g):
        out = out.at[i * a:(i + 1) * a, i * b:(i + 1) * b].set(w[i])
    return out


def _layer_weights(l, p):
    wr = jnp.pad(p["w_router"][l], ((0, 0), (0, ROUTER_PAD - N_EXPERTS)))
    hd = jnp.arange(D_A) // HEAD_DIM
    return {
        "ln1_g": p["ln1_g"][l][None], "w_in": p["w_in"][l].astype(BF16),
        "g_q": jnp.tile(p["g_q"][l], N_HEADS_A)[None], "g_k": jnp.tile(p["g_k"][l], N_HEADS_A)[None],
        "g_vb": p["g_vb"][l][None],
        "ones_bd": (hd[:, None] == hd[None, :]).astype(BF16),
        "w_s": p["w_s"][l],
        "bias_s": jnp.repeat(p["b_s"][l].T, HEAD_DIM, axis=1),
        "conv_w": p["conv_w"][l], "conv_b": p["conv_b"][l][None],
        "w_a_bd": _block_diag(p["w_a"][l]).astype(BF16), "b_a": p["b_a"][l][None],
        "w_x_bd": _block_diag(p["w_x"][l]).astype(BF16), "b_x": p["b_x"][l][None],
        "lru_lambda": p["lru_lambda"][l][None],
        "g_out": p["g_out"][l][None], "w_out": p["w_out"][l].astype(BF16),
        "ln2_g": p["ln2_g"][l][None],
        "w_router": wr.astype(BF16),
        "b_router": jnp.pad(p["b_router"][l], (0, ROUTER_PAD - N_EXPERTS),
                            constant_values=NEG_BIG)[None],
        "layer": l, "w_gu": p["w_gu"], "w_down": p["w_down"],
        "b_glu": p["b_gu"][l][:, None, 0::2], "b_lin": p["b_gu"][l][:, None, 1::2],
        "b_down": p["b_down"][l][:, None, :],
    }


def kernel(x_prompt, x_sample, cache_win_k, cache_win_v, state_conv, state_lru, ln1_g, w_in, g_q, g_k,
           g_vb, w_s, b_s, conv_w, conv_b, w_a, b_a, w_x, b_x, lru_lambda, g_out, w_out, ln2_g, w_router,
           b_router, w_gu, b_gu, w_down, b_down):
    params = dict(ln1_g=ln1_g, w_in=w_in, g_q=g_q, g_k=g_k, g_vb=g_vb, w_s=w_s, b_s=b_s, conv_w=conv_w,
                  conv_b=conv_b, w_a=w_a, b_a=b_a, w_x=w_x, b_x=b_x, lru_lambda=lru_lambda, g_out=g_out,
                  w_out=w_out, ln2_g=ln2_g, w_router=w_router, b_router=b_router, w_gu=w_gu, b_gu=b_gu,
                  w_down=w_down, b_down=b_down)
    bp, sp, _ = x_prompt.shape
    bs, ss, _ = x_sample.shape
    depth = w_in.shape[0]
    n_p, n_s = bp * sp, bs * ss
    keep = min(DILATED_PATTERNS[-1][0], sp)
    w_buf = cache_win_k.shape[2]
    ckt = cache_win_k.transpose(0, 1, 3, 4, 2).reshape(depth, bs, D_A, w_buf)
    cvt = cache_win_v.transpose(0, 1, 3, 4, 2).reshape(depth, bs, D_A, w_buf)
    x = (x_prompt.reshape(n_p, D_MODEL), x_sample.reshape(n_s, D_MODEL))
    zero_conv = jnp.zeros((bp, SUBLANES, D_C), F32)
    zero_h = jnp.zeros((bp, 1, D_C), F32)
    outs = {name: [] for name in ("pconv", "plru", "sk", "sv", "sconv", "slru", "svb")}
    window = None
    for l in range(depth):
        lw = _layer_weights(l, params)
        q, k, v, bc = _in_proj(x, lw["ln1_g"], lw["w_in"], lw["g_q"], lw["g_k"], lw["g_vb"], lw["ones_bd"])
        oa_p = _attn_prompt(q, k, v, bp, sp)
        ob_p, oc_p, h_p = _mixer_bc(bc, zero_conv, zero_h, lw, bp, sp, MIX_TILE, MIX_TILE - 1)
        oa_s = _attn_sample(q, k, v, ckt, cvt, l, n_p, bs, ss)
        bc_s = bc[n_p:].reshape(bs, ss, -1)
        bc_s_pad = jnp.pad(bc_s, ((0, 0), (0, CHUNK - ss), (0, 0))).reshape(bs * CHUNK, -1)
        conv8 = jnp.pad(state_conv[l], ((0, 0), (SUBLANES - (CONV_W - 1), 0), (0, 0)))
        ob_s, oc_s, h_s = _mixer_bc(bc_s_pad, conv8, state_lru[l][:, None, :], lw, bs, CHUNK, CHUNK, ss - 1)
        ob_s = ob_s.reshape(bs, CHUNK, D_B)[:, :ss].reshape(n_s, D_B)
        oc_s = oc_s.reshape(bs, CHUNK, D_C)[:, :ss].reshape(n_s, D_C)
        xn, h2, idx, gates, rank, counts = _out_proj((oa_p, oa_s), (ob_p, ob_s), (oc_p, oc_s), x, lw)
        x = _moe(xn, h2, idx, gates, rank, counts[:, 0, :N_EXPERTS], lw, n_p)

        window = _export_window(k, v, window, l, depth, bp, sp, keep)
        xc_p = bc[:n_p, 2 * D_B:2 * D_B + D_C].reshape(bp, sp, D_C)
        outs["pconv"].append(xc_p[:, sp - (CONV_W - 1):])
        outs["plru"].append(h_p[:, 0])
        outs["sk"].append(k[n_p:].reshape(bs, ss, N_HEADS_A, HEAD_DIM))
        outs["sv"].append(v[n_p:].reshape(bs, ss, N_HEADS_A, HEAD_DIM))
        xpad_s = jnp.concatenate([state_conv[l], bc_s[:, :, 2 * D_B:2 * D_B + D_C]], axis=1)
        outs["sconv"].append(xpad_s[:, -(CONV_W - 1):])
        outs["slru"].append(h_s[:, 0])
        outs["svb"].append(bc_s[:, :, D_B:2 * D_B])
    y_p = x[0].reshape(bp, sp, D_MODEL)
    y_s = x[1].reshape(bs, ss, D_MODEL)
    st = lambda name: jnp.stack(outs[name])
    heads_last = lambda t: t.reshape(depth, bp, N_HEADS_A, HEAD_DIM, keep).transpose(0, 1, 4, 2, 3)
    return (y_p, y_s, heads_last(window[0]), heads_last(window[1]), st("pconv"), st("plru"), st("sk"), st("sv"), st("sconv"),
            st("slru"), st("svb"))
```

```python
import functools

import jax
import jax.numpy as jnp
from jax import lax
from jax.experimental import pallas as pl
from jax.experimental.pallas import tpu as pltpu
from jax.experimental.pallas import tpu_sc as plsc

F32 = jnp.float32
BF16 = jnp.bfloat16

D_MODEL = 1024
HEAD_DIM = 64
N_HEADS_A = 8
D_A = N_HEADS_A * HEAD_DIM
N_HEADS_B = 4
D_B = N_HEADS_B * HEAD_DIM
N_GROUPS_C = 4
D_C = N_GROUPS_C * HEAD_DIM
D_IN = 3 * D_A + 2 * D_B + 2 * D_C
DILATED_PATTERNS = ((128, 1), (512, 4), (2048, 16))
N_PATTERNS = len(DILATED_PATTERNS)
CHUNK = 128
CONV_W = 4
LRU_C = 8.0
N_EXPERTS = 32
TOP_K = 4
D_FF = 1024
SWIGLU_LIMIT = 7.0
SWIGLU_ALPHA = 1.702
EPS = 1e-6
ATTN_SCALE = HEAD_DIM ** -0.5
PAST_LEN = 16384

LANES = 128
SUBLANES = 8
VMEM_LIMIT_BYTES = 56 * 1024 * 1024

Q_BLOCK = 128
ATTN_SPAN = 2048
ATTN_GROUP = 4
RAW_KEY_STRIDE = 16
TOKEN_TILE = 256
MIX_TILE = 512
MOE_TILE = 256
ROUTER_PAD = LANES
SC_SUBCORES = 16
SC_WORKERS = 2 * SC_SUBCORES
SC_GROUP = LANES // TOP_K
NEG_BIG = -1e30


def _cparams(semantics):
    return pltpu.CompilerParams(dimension_semantics=semantics,
                                vmem_limit_bytes=VMEM_LIMIT_BYTES)


def _full(shape):
    return pl.BlockSpec(shape, lambda *_: (0,) * len(shape))


def _rms(t, g):
    ms = jnp.mean(t * t, axis=-1, keepdims=True)
    return t * lax.rsqrt(ms + EPS) * g


def _pair_specs(tm, w, n_head):
    nh = n_head // tm
    return [pl.BlockSpec((tm, w), lambda i: (jnp.minimum(i, nh - 1), 0)),
            pl.BlockSpec((tm, w), lambda i: (jnp.maximum(i - nh, 0), 0))]


def _pick(head_ref, tail_ref, n_head_tiles):
    return jnp.where(pl.program_id(0) >= n_head_tiles, tail_ref[...], head_ref[...])


def _split_bf16(t):
    hi = t.astype(BF16)
    lo = (t - hi.astype(F32)).astype(BF16)
    return hi, lo


def _in_proj_kernel(xh_ref, xt_ref, g1_ref, w_ref, gq_ref, gk_ref, gvb_ref, ones_ref,
                    q_ref, k_ref, v_ref, bc_ref, *, n_head_tiles):
    h = _rms(_pick(xh_ref, xt_ref, n_head_tiles), g1_ref[...]).astype(BF16)
    z = jnp.dot(h, w_ref[...], preferred_element_type=F32)

    def head_norm(t, g):
        hi, lo = _split_bf16(t * t)
        ss = (jnp.dot(hi, ones_ref[...], preferred_element_type=F32)
              + jnp.dot(lo, ones_ref[...], preferred_element_type=F32))
        return t * lax.rsqrt(ss * (1.0 / HEAD_DIM) + EPS) * g

    q_ref[...] = head_norm(z[:, 0:D_A], gq_ref[...]) * ATTN_SCALE
    k_ref[...] = head_norm(z[:, D_A:2 * D_A], gk_ref[...])
    v_ref[...] = z[:, 2 * D_A:3 * D_A]
    o = 3 * D_A
    bc_ref[...] = z[:, o:]
    bc_ref[:, D_B:2 * D_B] = _rms(z[:, o + D_B:o + 2 * D_B], gvb_ref[...])


def _in_proj(x_pair, g1, w_bf16, gq, gk, gvb, ones_bd):
    n_head = x_pair[0].shape[0]
    n = n_head + x_pair[1].shape[0]
    tm = TOKEN_TILE
    row = lambda w: pl.BlockSpec((tm, w), lambda i: (i, 0))
    return pl.pallas_call(
        functools.partial(_in_proj_kernel, n_head_tiles=n_head // tm),
        grid=(n // tm,),
        in_specs=_pair_specs(tm, D_MODEL, n_head) + [_full((1, D_MODEL)), _full((D_MODEL, D_IN)), _full((1, D_A)),
                  _full((1, D_A)), _full((1, D_B)), _full((D_A, D_A))],
        out_specs=[row(D_A), row(D_A), row(D_A), row(2 * D_B + 2 * D_C)],
        out_shape=[jax.ShapeDtypeStruct((n, D_A), F32)] * 3
        + [jax.ShapeDtypeStruct((n, 2 * D_B + 2 * D_C), F32)],
        compiler_params=_cparams(("parallel",)),
        name="in_proj",
    )(*x_pair, g1, w_bf16, gq, gk, gvb, ones_bd)


def _attn_prompt_kernel(q_ref, kp_ref, kc_ref, vp_ref, vc_ref, o_ref, kq0a, kq0b, kq1a, kq1b, vqh, vql,
                        m_s, l_s, a_s):
    span = pl.program_id(2)
    qb2 = 2 * Q_BLOCK
    lane = lax.broadcasted_iota(jnp.int32, (Q_BLOCK, LANES), 1)
    head0 = lane < HEAD_DIM
    lane2 = lax.broadcasted_iota(jnp.int32, (qb2, LANES), 1)
    head0_2 = lane2 < HEAD_DIM
    swap = lambda t: pltpu.roll(t, HEAD_DIM, axis=1)

    def pack(k_ref, v_ref, base):
        def body(c, carry):
            src = pl.ds(pl.multiple_of(c * qb2, qb2), qb2)
            dst = pl.ds(pl.multiple_of(base + c * qb2, qb2), qb2)
            k = k_ref[src, :]
            kh = k.astype(BF16).astype(F32)
            kl = k - kh
            kq0a[dst, :] = jnp.where(head0_2, kh, swap(kh))
            kq0b[dst, :] = jnp.where(head0_2, kl, 0.0)
            kq1a[dst, :] = jnp.where(head0_2, swap(kh), kh)
            kq1b[dst, :] = jnp.where(head0_2, swap(kl), 0.0)
            v = v_ref[src, :]
            vh = v.astype(BF16).astype(F32)
            vqh[dst, :] = vh
            vql[dst, :] = v - vh
            return carry
        lax.fori_loop(0, ATTN_SPAN // qb2, body, 0)

    pack(kp_ref, vp_ref, 0)
    pack(kc_ref, vc_ref, ATTN_SPAN)

    row = lax.broadcasted_iota(jnp.int32, (qb2, qb2), 0) & (Q_BLOCK - 1)
    col = lax.broadcasted_iota(jnp.int32, (qb2, qb2), 1)
    band = (col >= row) & (col <= row + Q_BLOCK)
    cur = col >= Q_BLOCK
    nt = (((1,), (1,)), ((), ()))

    def ds(start, size, d):
        return pl.ds(start, size) if d == 1 else pl.ds(start, size, stride=d)

    for p, (_, d) in enumerate(DILATED_PATTERNS):
        nblk = ATTN_SPAN // (Q_BLOCK * d)
        assert d < RAW_KEY_STRIDE or nblk == 1

        def scores(blk, d=d, nblk=nblk):
            r = blk // nblk
            ib = blk % nblk
            qstart = r + d * Q_BLOCK * ib
            if d == 1:
                qstart = pl.multiple_of(Q_BLOCK * blk, Q_BLOCK)
            kstart = ATTN_SPAN + qstart - d * Q_BLOCK
            q = q_ref[ds(qstart, Q_BLOCK, d), :]
            qh = q.astype(BF16).astype(F32)
            ql = q - qh
            lhs0 = jnp.concatenate([jnp.where(head0, qh, swap(ql)), jnp.where(head0, qh, 0.0)], axis=1)
            lhs1 = jnp.concatenate([jnp.where(head0, swap(qh), ql), jnp.where(head0, swap(qh), 0.0)], axis=1)
            if d < RAW_KEY_STRIDE:
                keys = ds(kstart, qb2, d)
                k0 = jnp.concatenate([kq0a[keys, :], kq0b[keys, :]], axis=1).astype(BF16)
                k1 = jnp.concatenate([kq1a[keys, :], kq1b[keys, :]], axis=1).astype(BF16)
                vh, vl = vqh[keys, :], vql[keys, :]
            else:
                both = lambda p_ref, c_ref: jnp.concatenate(
                    [p_ref[ds(r, Q_BLOCK, d), :], c_ref[ds(r, Q_BLOCK, d), :]], axis=0)
                k = both(kp_ref, kc_ref)
                kh = k.astype(BF16).astype(F32)
                kl = k - kh
                k0 = jnp.concatenate([jnp.where(head0_2, kh, swap(kh)), jnp.where(head0_2, kl, 0.0)],
                                     axis=1).astype(BF16)
                k1 = jnp.concatenate([jnp.where(head0_2, swap(kh), kh), jnp.where(head0_2, swap(kl), 0.0)],
                                     axis=1).astype(BF16)
                v = both(vp_ref, vc_ref)
                vh = v.astype(BF16).astype(F32)
                vl = v - vh
            s0 = lax.dot_general(lhs0.astype(BF16), k0, nt, preferred_element_type=F32)
            s1 = lax.dot_general(lhs1.astype(BF16), k1, nt, preferred_element_type=F32)
            prev_ok = jnp.logical_or(ib > 0, span > 0)
            s = jnp.concatenate([s0, s1], axis=0)
            return jnp.where(band & (cur | prev_ok), s, -jnp.inf), qstart, (vh, vl)

        def softmax(s):
            m = jnp.max(s, axis=-1, keepdims=True)
            e = jnp.exp(s - m)
            return m, e, jnp.sum(e, axis=-1, keepdims=True)

        def weighted(e, v_parts):
            eh, el = _split_bf16(e)
            vh, vl = v_parts
            rhs = jnp.concatenate([jnp.concatenate([vh, vl], axis=1),
                                   jnp.concatenate([vh, jnp.zeros_like(vh)], axis=1)], axis=0).astype(BF16)
            out = jnp.dot(jnp.concatenate([eh, el], axis=1), rhs, preferred_element_type=F32)
            return out[:, :LANES] + out[:, LANES:]

        def store(qstart, m, l, acc, p=p, d=d):
            dst = ds(qstart, Q_BLOCK, d)
            shape = (Q_BLOCK, LANES)
            m_s[p, dst, :] = jnp.where(head0, jnp.broadcast_to(m[:Q_BLOCK], shape),
                                       jnp.broadcast_to(m[Q_BLOCK:], shape))
            l_s[p, dst, :] = jnp.where(head0, jnp.broadcast_to(l[:Q_BLOCK], shape),
                                       jnp.broadcast_to(l[Q_BLOCK:], shape))
            a_s[p, dst, :] = jnp.where(head0, acc[:Q_BLOCK], acc[Q_BLOCK:])

        def body(it, carry):
            sc = [scores(it * ATTN_GROUP + g) for g in range(ATTN_GROUP)]
            sm = [softmax(s) for s, _, _ in sc]
            ac = [weighted(e, v_parts) for (_, e, _), (_, _, v_parts) in zip(sm, sc)]
            for (_, qstart, _), (m, _, l), acc in zip(sc, sm, ac):
                store(qstart, m, l, acc)
            return carry

        lax.fori_loop(0, ATTN_SPAN // (Q_BLOCK * ATTN_GROUP), body, 0)

    def merge(c, carry):
        rows = pl.ds(pl.multiple_of(c * Q_BLOCK, Q_BLOCK), Q_BLOCK)
        ms = [m_s[p, rows, :] for p in range(N_PATTERNS)]
        m_all = functools.reduce(jnp.maximum, ms)
        ws = [jnp.exp(m - m_all) for m in ms]
        num = sum(w * a_s[p, rows, :] for p, w in enumerate(ws))
        den = sum(w * l_s[p, rows, :] for p, w in enumerate(ws))
        o_ref[rows, :] = num / den
        return carry

    lax.fori_loop(0, ATTN_SPAN // Q_BLOCK, merge, 0)


def _attn_prompt(q, k, v, batch, seq):
    nspan = seq // ATTN_SPAN
    blk = (ATTN_SPAN, LANES)
    cur = pl.BlockSpec(blk, lambda b, hp, s: (b * nspan + s, hp))
    prev = pl.BlockSpec(blk, lambda b, hp, s: (b * nspan + jnp.maximum(s - 1, 0), hp))
    acc = pltpu.VMEM((N_PATTERNS, ATTN_SPAN, LANES), F32)
    packed = pltpu.VMEM((2 * ATTN_SPAN, LANES), F32)
    return pl.pallas_call(
        _attn_prompt_kernel,
        grid=(batch, D_A // LANES, nspan),
        in_specs=[cur, prev, cur, prev, cur],
        out_specs=cur,
        out_shape=jax.ShapeDtypeStruct((batch * seq, D_A), F32),
        scratch_shapes=[packed] * 6 + [acc, acc, acc],
        compiler_params=_cparams(("parallel", "parallel", "arbitrary")),
        name="attn_prompt",
    )(q, k, k, v, v)


def _attn_sample_kernel(q_ref, kn_ref, vn_ref, ckt_ref, cvt_ref, o_ref, *, w_buf, t_new):
    pad = LANES - t_new
    zeros = jnp.zeros((pad, D_A), F32)
    kn = jnp.concatenate([kn_ref[...], zeros], axis=0).astype(BF16)
    vn = jnp.concatenate([vn_ref[...], zeros], axis=0).astype(BF16)
    n_rows = N_HEADS_A * t_new
    hrow = lax.broadcasted_iota(jnp.int32, (n_rows, D_A), 0) // t_new
    hlane = lax.broadcasted_iota(jnp.int32, (n_rows, D_A), 1) // HEAD_DIM
    own = hrow == hlane
    q_rep = jnp.concatenate([q_ref[...]] * N_HEADS_A, axis=0)
    q64 = jnp.where(own, q_rep, 0.0).astype(BF16)
    nt = (((1,), (1,)), ((), ()))
    s_w = jnp.dot(q64, ckt_ref[...].astype(BF16), preferred_element_type=F32)
    s_n = lax.dot_general(q64, kn, nt, preferred_element_type=F32)
    vt = cvt_ref[...].astype(BF16)

    def dist(n_cols, first):
        t = lax.broadcasted_iota(jnp.int32, (n_rows, n_cols), 0) % t_new
        return t - lax.broadcasted_iota(jnp.int32, (n_rows, n_cols), 1) - first

    dist_w, dist_n = dist(w_buf, -w_buf), dist(LANES, 0)
    ms, ls, accs = [], [], []
    for w, d in DILATED_PATTERNS:
        ok = lambda ds: (ds >= 0) & (ds <= w) & ((ds & (d - 1)) == 0)
        sw = jnp.where(ok(dist_w), s_w, -jnp.inf)
        sn = jnp.where(ok(dist_n), s_n, -jnp.inf)
        m = jnp.maximum(jnp.max(sw, axis=-1, keepdims=True), jnp.max(sn, axis=-1, keepdims=True))
        ew = jnp.exp(sw - m)
        en = jnp.exp(sn - m)
        ms.append(m)
        ls.append(jnp.sum(ew, axis=-1, keepdims=True) + jnp.sum(en, axis=-1, keepdims=True))
        accs.append(lax.dot_general(ew.astype(BF16), vt, nt, preferred_element_type=F32)
                    + jnp.dot(en.astype(BF16), vn, preferred_element_type=F32))
    m_all = functools.reduce(jnp.maximum, ms)
    ws = [jnp.exp(m - m_all) for m in ms]
    num = sum(w * a for w, a in zip(ws, accs))
    den = sum(w * l for w, l in zip(ws, ls))
    o = jnp.where(own, num / den, 0.0)
    out = o[0:t_new]
    for h in range(1, N_HEADS_A):
        out = out + o[h * t_new:(h + 1) * t_new]
    o_ref[...] = out


def _attn_sample(q, k, v, cache_kt, cache_vt, layer, row0, batch, t_new):
    w_buf = cache_kt.shape[3]
    assert row0 % t_new == 0 and t_new == SUBLANES
    new = pl.BlockSpec((t_new, D_A), lambda b: (row0 // t_new + b, 0))
    cache = pl.BlockSpec((None, None, D_A, w_buf), lambda b: (layer, b, 0, 0))
    return pl.pallas_call(
        functools.partial(_attn_sample_kernel, w_buf=w_buf, t_new=t_new),
        grid=(batch,),
        in_specs=[new, new, new, cache, cache],
        out_specs=pl.BlockSpec((t_new, D_A), lambda b: (b, 0)),
        out_shape=jax.ShapeDtypeStruct((batch * t_new, D_A), F32),
        compiler_params=_cparams(("parallel",)),
        name="attn_sample",
    )(q, k, v, cache_kt, cache_vt)


def _export_kernel(*refs):
    k_ref, v_ref = refs[0], refs[1]
    pk_ref, pv_ref = refs[-2], refs[-1]
    pk_ref[...] = k_ref[...].T
    pv_ref[...] = v_ref[...].T


def _export_window(k, v, prev, layer, depth, batch, seq, keep):
    tm = TOKEN_TILE
    first = (seq - keep) // tm
    src = pl.BlockSpec((tm, D_A), lambda b, j: (b * (seq // tm) + first + j, 0))
    dst = pl.BlockSpec((None, None, D_A, tm), lambda b, j: (layer, b, 0, j))
    shape = jax.ShapeDtypeStruct((depth, batch, D_A, keep), F32)
    carried = [] if prev is None else list(prev)
    return pl.pallas_call(
        _export_kernel,
        grid=(batch, keep // tm),
        in_specs=[src, src] + [pl.BlockSpec(memory_space=pl.ANY)] * len(carried),
        out_specs=[dst, dst],
        out_shape=[shape, shape],
        input_output_aliases={2 + i: i for i in range(len(carried))},
        compiler_params=_cparams(("parallel", "parallel")),
        name="export_window",
    )(k, v, *carried)


def _gelu_tanh(x):
    return 0.5 * x * (1.0 + jnp.tanh(0.7978845608028654 * (x + 0.044715 * x * x * x)))


def _mixer_bc_kernel(bc_ref, cb_ref, h0_ref, ws_ref, bs_ref, cw_ref, cbias_ref, wa_ref, ba_ref,
                     wx_ref, bx_ref, lam_ref, ob_ref, oc_ref, hl_ref, xp_s, h_s, *, tt, last_row):
    j = pl.program_id(1)

    @pl.when(j == 0)
    def _():
        xp_s[0:SUBLANES, :] = cb_ref[...]
        h_s[...] = h0_ref[...]

    nch = tt // CHUNK
    vcat = jnp.concatenate([bc_ref[c * CHUNK:(c + 1) * CHUNK, D_B:2 * D_B] for c in range(nch)],
                           axis=1).astype(BF16)
    ri = lax.broadcasted_iota(jnp.int32, (CHUNK, CHUNK), 0)
    ci = lax.broadcasted_iota(jnp.int32, (CHUNK, CHUNK), 1)
    hl = (lax.broadcasted_iota(jnp.int32, (CHUNK, nch * D_B), 1) % D_B) // HEAD_DIM
    mixed = jnp.zeros((CHUNK, nch * D_B), F32)
    for h in range(N_HEADS_B):
        wh = jnp.where(ri >= ci, ws_ref[h], 0.0).astype(BF16)
        mh = jnp.dot(wh, vcat, preferred_element_type=F32)
        mixed = mixed + jnp.where(hl == h, mh, 0.0)
    for c in range(nch):
        rows = slice(c * CHUNK, (c + 1) * CHUNK)
        ob_ref[rows, :] = bc_ref[rows, 0:D_B] * (mixed[:, c * D_B:(c + 1) * D_B] + bs_ref[...])

    xc = bc_ref[:, 2 * D_B:2 * D_B + D_C]
    xp_s[SUBLANES:SUBLANES + tt, :] = xc
    xconv = cbias_ref[...] + cw_ref[CONV_W - 1:CONV_W, :] * xc
    for kk in range(CONV_W - 1):
        off = SUBLANES - (CONV_W - 1) + kk
        xconv = xconv + cw_ref[kk:kk + 1, :] * xp_s[off:off + tt, :]
    xp_s[0:SUBLANES, :] = xp_s[tt:tt + SUBLANES, :]
    xb = xconv.astype(BF16)
    r = jax.nn.sigmoid(jnp.dot(xb, wa_ref[...], preferred_element_type=F32) + ba_ref[...])
    i = jax.nn.sigmoid(jnp.dot(xb, wx_ref[...], preferred_element_type=F32) + bx_ref[...])
    nl = -lam_ref[...]
    softplus = jnp.maximum(nl, 0.0) + jnp.log1p(jnp.exp(-jnp.abs(nl)))
    a = jnp.exp(-LRU_C * r * softplus)
    b = jnp.sqrt(1.0 - a * a) * (i * xconv)
    rowi = lax.broadcasted_iota(jnp.int32, (tt, D_C), 0)
    step = 1
    while step < tt:
        a_sh = pltpu.roll(a, step, axis=0)
        b_sh = pltpu.roll(b, step, axis=0)
        live = rowi >= step
        b = jnp.where(live, a * b_sh + b, b)
        a = jnp.where(live, a * a_sh, a)
        step *= 2
    h = a * h_s[...] + b
    h_s[...] = h[tt - 1:tt, :]
    oc_ref[...] = h * _gelu_tanh(bc_ref[:, 2 * D_B + D_C:])

    @pl.when(j == pl.num_programs(1) - 1)
    def _():
        hl_ref[...] = h[last_row:last_row + 1, :]


def _mixer_bc(bc, conv_buf8, h0, lw, batch, t_len, tt, last_row):
    nt = t_len // tt
    rows = lambda w: pl.BlockSpec((tt, w), lambda b, j: (b * nt + j, 0))
    per_b = lambda s: pl.BlockSpec((None,) + s, lambda b, j: (b,) + (0,) * len(s))
    return pl.pallas_call(
        functools.partial(_mixer_bc_kernel, tt=tt, last_row=last_row),
        grid=(batch, nt),
        in_specs=[rows(2 * D_B + 2 * D_C), per_b((SUBLANES, D_C)), per_b((1, D_C)),
                  _full((N_HEADS_B, CHUNK, CHUNK)), _full((CHUNK, D_B)), _full((CONV_W, D_C)),
                  _full((1, D_C)), _full((D_C, D_C)), _full((1, D_C)), _full((D_C, D_C)),
                  _full((1, D_C)), _full((1, D_C))],
        out_specs=[rows(D_B), rows(D_C), per_b((1, D_C))],
        out_shape=[jax.ShapeDtypeStruct((batch * t_len, D_B), F32),
                   jax.ShapeDtypeStruct((batch * t_len, D_C), F32),
                   jax.ShapeDtypeStruct((batch, 1, D_C), F32)],
        scratch_shapes=[pltpu.VMEM((tt + SUBLANES, D_C), F32), pltpu.VMEM((1, D_C), F32)],
        compiler_params=_cparams(("parallel", "arbitrary")),
        name="mixer_bc",
    )(bc, conv_buf8, h0, lw["w_s"], lw["bias_s"], lw["conv_w"], lw["conv_b"], lw["w_a_bd"],
      lw["b_a"], lw["w_x_bd"], lw["b_x"], lw["lru_lambda"])


def _out_proj_kernel(oah_ref, oat_ref, obh_ref, obt_ref, och_ref, oct_ref, xh_ref, xt_ref, go_ref, wo_ref,
                     g2_ref, wr_ref, br_ref, xn_ref, h2_ref, idx_ref, gate_ref, rank_ref, cnt_ref, *,
                     n_head_tiles):
    pick = functools.partial(_pick, n_head_tiles=n_head_tiles)
    oa = _rms(pick(oah_ref, oat_ref), go_ref[:, 0:D_A]).astype(BF16)
    ob = _rms(pick(obh_ref, obt_ref), go_ref[:, D_A:D_A + D_B]).astype(BF16)
    oc = _rms(pick(och_ref, oct_ref), go_ref[:, D_A + D_B:]).astype(BF16)
    y = (jnp.dot(oa, wo_ref[0:D_A, :], preferred_element_type=F32)
         + jnp.dot(ob, wo_ref[D_A:D_A + D_B, :], preferred_element_type=F32)
         + jnp.dot(oc, wo_ref[D_A + D_B:, :], preferred_element_type=F32))
    xn = pick(xh_ref, xt_ref) + y
    xn_ref[...] = xn
    hb = _rms(xn, g2_ref[...]).astype(BF16)
    bits = pltpu.bitcast(hb.astype(F32), jnp.uint32)
    half = D_MODEL // 2
    h2_ref[...] = (bits[:, :half] >> 16) | (bits[:, half:] & jnp.uint32(0xFFFF0000))
    logits = jnp.dot(hb, wr_ref[...], preferred_element_type=F32) + br_ref[...]
    lane = lax.broadcasted_iota(jnp.int32, logits.shape, 1).astype(F32)
    cur = logits
    tops, idxs = [], []
    for _ in range(TOP_K):
        m = jnp.max(cur, axis=-1, keepdims=True)
        ix = jnp.min(jnp.where(cur == m, lane, float(ROUTER_PAD)), axis=-1, keepdims=True)
        tops.append(m)
        idxs.append(ix)
        cur = jnp.where(lane == ix, -jnp.inf, cur)
    es = [jnp.exp(t - tops[0]) for t in tops]
    den = sum(es)
    idx_out = jnp.zeros(logits.shape, F32)
    gate_out = jnp.zeros(logits.shape, F32)
    for kk in range(TOP_K):
        idx_out = jnp.where(lane == kk, idxs[kk], idx_out)
        gate_out = jnp.where(lane == kk, es[kk] / den, gate_out)
    idx_ref[...] = idx_out.astype(jnp.int32)
    gate_ref[...] = gate_out
    tm = logits.shape[0]
    chosen = jnp.zeros(logits.shape, F32)
    for kk in range(TOP_K):
        chosen = jnp.where(lane == idxs[kk], 1.0, chosen)
    below = (lax.broadcasted_iota(jnp.int32, (tm, tm), 0)
             > lax.broadcasted_iota(jnp.int32, (tm, tm), 1)).astype(BF16)
    earlier = jnp.dot(below, chosen.astype(BF16), preferred_element_type=F32)
    rank_out = jnp.zeros(logits.shape, F32)
    for kk in range(TOP_K):
        rk = jnp.sum(jnp.where(lane == idxs[kk], earlier, 0.0), axis=-1, keepdims=True)
        rank_out = jnp.where(lane == kk, rk, rank_out)
    rank_ref[...] = rank_out.astype(jnp.int32)
    cnt_ref[...] = jnp.broadcast_to(jnp.sum(chosen, axis=0, keepdims=True),
                                    cnt_ref.shape).astype(jnp.int32)


def _out_proj(oa_pair, ob_pair, oc_pair, x_pair, lw):
    n_head = x_pair[0].shape[0]
    n = n_head + x_pair[1].shape[0]
    tm = TOKEN_TILE
    row = lambda w: pl.BlockSpec((tm, w), lambda i: (i, 0))
    pair = lambda w: _pair_specs(tm, w, n_head)
    return pl.pallas_call(
        functools.partial(_out_proj_kernel, n_head_tiles=n_head // tm),
        grid=(n // tm,),
        in_specs=pair(D_A) + pair(D_B) + pair(D_C) + pair(D_MODEL) + [_full((1, D_MODEL)),
                  _full((D_MODEL, D_MODEL)), _full((1, D_MODEL)), _full((D_MODEL, ROUTER_PAD)),
                  _full((1, ROUTER_PAD))],
        out_specs=[row(D_MODEL), row(D_MODEL // 2), row(ROUTER_PAD), row(ROUTER_PAD), row(ROUTER_PAD),
                   pl.BlockSpec((None, SUBLANES, ROUTER_PAD), lambda i: (i, 0, 0))],
        out_shape=[jax.ShapeDtypeStruct((n, D_MODEL), F32), jax.ShapeDtypeStruct((n, D_MODEL // 2), jnp.uint32),
                   jax.ShapeDtypeStruct((n, ROUTER_PAD), jnp.int32),
                   jax.ShapeDtypeStruct((n, ROUTER_PAD), F32),
                   jax.ShapeDtypeStruct((n, ROUTER_PAD), jnp.int32),
                   jax.ShapeDtypeStruct((n // tm, SUBLANES, ROUTER_PAD), jnp.int32)],
        compiler_params=_cparams(("parallel",)),
        name="out_proj",
    )(*oa_pair, *ob_pair, *oc_pair, *x_pair, lw["g_out"], lw["w_out"], lw["ln2_g"], lw["w_router"],
      lw["b_router"])


def _moe_kernel(te_ref, nu_ref, nx_ref, x_ref, wgu_hbm, bg_ref, bl_ref, wd_hbm, bd_ref, y_ref, wg_s, wl_s, wd_s,
                gu_buf, d_buf, sem, slot_ref, *, layer):
    i = pl.program_id(0)
    used = i < nu_ref[0]
    new_expert = jnp.logical_or(i == 0, te_ref[i] != te_ref[jnp.maximum(i - 1, 0)])

    def fetch(e, s):
        return (pltpu.make_async_copy(wgu_hbm.at[layer, e], gu_buf.at[s], sem.at[0, s]),
                pltpu.make_async_copy(wd_hbm.at[layer, e], d_buf.at[s], sem.at[1, s]))

    @pl.when(i == 0)
    def _():
        slot_ref[0] = 0
        for c in fetch(te_ref[0], 0):
            c.start()

    @pl.when(jnp.logical_and(used, new_expert))
    def _():
        s = slot_ref[0]
        for c in fetch(te_ref[i], s):
            c.wait()
        nx = nx_ref[i]

        @pl.when(nx >= 0)
        def _():
            for c in fetch(nx, 1 - s):
                c.start()

        slot_ref[0] = 1 - s
        wgu_ref = gu_buf.at[s]
        wd_ref = d_buf.at[s]
        w2 = 2 * LANES
        src = lax.broadcasted_iota(jnp.int32, (w2, w2), 0)
        col = lax.broadcasted_iota(jnp.int32, (w2, w2), 1)
        pick = (src == jnp.where(col < LANES, 2 * col, 2 * (col - LANES) + 1)).astype(BF16)
        for c in range(2 * D_FF // w2):
            out = jnp.dot(wgu_ref[:, c * w2:(c + 1) * w2].astype(BF16), pick, preferred_element_type=F32)
            wg_s[:, c * LANES:(c + 1) * LANES] = out[:, :LANES].astype(BF16)
            wl_s[:, c * LANES:(c + 1) * LANES] = out[:, LANES:].astype(BF16)
        wd_s[...] = wd_ref[...].astype(BF16)

    @pl.when(used)
    def _():
        u = x_ref[...]
        x = jnp.concatenate([pltpu.bitcast(u << 16, F32).astype(BF16),
                             pltpu.bitcast(u & jnp.uint32(0xFFFF0000), F32).astype(BF16)], axis=1)
        zg = jnp.dot(x, wg_s[...], preferred_element_type=F32) + bg_ref[...]
        zl = jnp.dot(x, wl_s[...], preferred_element_type=F32) + bl_ref[...]
        glu = jnp.minimum(zg, SWIGLU_LIMIT)
        lin = jnp.clip(zl, -SWIGLU_LIMIT, SWIGLU_LIMIT)
        act = glu * jax.nn.sigmoid(SWIGLU_ALPHA * glu) * (lin + 1.0)
        y_ref[...] = jnp.dot(act.astype(BF16), wd_s[...], preferred_element_type=F32) + bd_ref[...]

    @pl.when(i >= nu_ref[0])
    def _():
        y_ref[...] = jnp.zeros(y_ref.shape, F32)


def _moe_ffn(xs, tile_expert, n_used, next_expert, lw):
    n_slots = xs.shape[0]
    tm = MOE_TILE
    wspec = lambda r, c: pl.BlockSpec((None, r, c), lambda i, te, nu, nx: (te[i], 0, 0))
    hbm = pl.BlockSpec(memory_space=pl.ANY)
    return pl.pallas_call(
        functools.partial(_moe_kernel, layer=lw["layer"]),
        grid_spec=pltpu.PrefetchScalarGridSpec(
            num_scalar_prefetch=3,
            grid=(n_slots // tm,),
            in_specs=[pl.BlockSpec((tm, D_MODEL // 2), lambda i, te, nu, nx: (i, 0)),
                      hbm, wspec(1, D_FF), wspec(1, D_FF), hbm, wspec(1, D_MODEL)],
            out_specs=pl.BlockSpec((tm, D_MODEL), lambda i, te, nu, nx: (i, 0)),
            scratch_shapes=[pltpu.VMEM((D_MODEL, D_FF), BF16), pltpu.VMEM((D_MODEL, D_FF), BF16),
                            pltpu.VMEM((D_FF, D_MODEL), BF16),
                            pltpu.VMEM((2, D_MODEL, 2 * D_FF), F32), pltpu.VMEM((2, D_FF, D_MODEL), F32),
                            pltpu.SemaphoreType.DMA((2, 2)), pltpu.SMEM((1,), jnp.int32)],
        ),
        out_shape=jax.ShapeDtypeStruct((n_slots, D_MODEL), F32),
        compiler_params=_cparams(("arbitrary",)),
        name="moe_ffn",
    )(tile_expert, n_used, next_expert, xs, lw["w_gu"], lw["b_glu"], lw["b_lin"], lw["w_down"], lw["b_down"])


def _route(idx, rank, counts):
    n_tok = idx.shape[0]
    experts = jnp.arange(N_EXPERTS, dtype=jnp.int32)
    totals = jnp.sum(counts, axis=0)
    tiles = (totals + MOE_TILE - 1) // MOE_TILE
    tile_end = jnp.sum(jnp.where(experts[None, :] <= experts[:, None], tiles[None, :], 0), axis=1)
    pad_start = (tile_end - tiles) * MOE_TILE
    tt = jnp.arange(counts.shape[0], dtype=jnp.int32)
    before = jnp.sum(jnp.where((tt[None, :] < tt[:, None])[:, :, None], counts[None, :, :], 0), axis=1)
    base = pad_start[None, :] + before
    base_tok = jnp.repeat(base, TOKEN_TILE, axis=0)
    picked = idx[:, :, None] == experts[None, None, :]
    dest = jnp.sum(jnp.where(picked, base_tok[:, None, :], 0), axis=-1) + rank
    n_tiles = -(-(n_tok * TOP_K) // MOE_TILE) + N_EXPERTS
    tile_ids = jnp.arange(n_tiles, dtype=jnp.int32)
    tile_expert = jnp.minimum(jnp.sum((tile_end[None, :] <= tile_ids[:, None]).astype(jnp.int32), axis=1),
                              N_EXPERTS - 1)
    n_used = tile_end[-1:].astype(jnp.int32)
    later = (experts[None, :] > experts[:, None]) & (tiles[None, :] > 0)
    next_of = jnp.min(jnp.where(later, experts[None, :], N_EXPERTS), axis=1)
    next_of = jnp.where(next_of == N_EXPERTS, -1, next_of)
    next_expert = jnp.sum(jnp.where(experts[None, :] == tile_expert[:, None], next_of[None, :], 0), axis=1)
    return dest.astype(jnp.int32), tile_expert, n_used, next_expert.astype(jnp.int32), n_tiles


def _sc_mesh():
    return plsc.VectorSubcoreMesh(core_axis_name="core", subcore_axis_name="subcore")


def _load_index_rows(d_hbm, i_vmem, sem, wid, ng, nr):
    def row(r):
        return pltpu.make_async_copy(d_hbm.at[pl.ds(wid + SC_WORKERS * r, 1)], i_vmem.at[pl.ds(r, 1)], sem)
    for r in range(nr):
        pl.when(wid + SC_WORKERS * r < ng)(lambda r=r: row(r).start())
    for r in range(nr):
        pl.when(wid + SC_WORKERS * r < ng)(lambda r=r: row(r).wait())


def _sc_dispatch(h2, dest_g, n_slots):
    ng = dest_g.shape[0]
    width = h2.shape[1]

    @pl.kernel(out_type=jax.ShapeDtypeStruct((n_slots, width), h2.dtype), mesh=_sc_mesh(),
               scratch_types=[pltpu.VMEM((pl.cdiv(ng, SC_WORKERS), LANES), jnp.int32), pltpu.VMEM((SC_GROUP, width), h2.dtype)]
               + [pltpu.SemaphoreType.DMA] * (1 + TOP_K))
    def kernel(h_hbm, d_hbm, o_hbm, i_vmem, buf, isem, *sems):
        wid = lax.axis_index("core") * SC_SUBCORES + lax.axis_index("subcore")
        nr = pl.cdiv(ng, SC_WORKERS)
        _load_index_rows(d_hbm, i_vmem, isem, wid, ng, nr)

        @pl.loop(0, nr)
        def _(r):
            g = wid + SC_WORKERS * r

            @pl.when(g < ng)
            def _():
                pltpu.sync_copy(h_hbm.at[pl.ds(g * SC_GROUP, SC_GROUP)], buf)
                puts = [pltpu.async_copy(buf, o_hbm.at[i_vmem.at[r, pl.ds(k * SC_GROUP, SC_GROUP)]], sems[k])
                        for k in range(TOP_K)]
                for put in puts:
                    put.wait()

    return kernel(h2, dest_g)


def _sc_collect(y, dest_g):
    ng = dest_g.shape[0]

    @pl.kernel(out_type=jax.ShapeDtypeStruct((ng * LANES, D_MODEL), F32), mesh=_sc_mesh(),
               scratch_types=[pltpu.VMEM((pl.cdiv(ng, SC_WORKERS), LANES), jnp.int32)]
               + [pltpu.VMEM((SC_GROUP, D_MODEL), F32)] * 2 + [pltpu.SemaphoreType.DMA] * 5)
    def kernel(y_hbm, d_hbm, o_hbm, i_vmem, buf0, buf1, isem, g0, g1, w0, w1):
        wid = lax.axis_index("core") * SC_SUBCORES + lax.axis_index("subcore")
        bufs, gsem, wsem = (buf0, buf1), (g0, g1), (w0, w1)
        nr = pl.cdiv(ng, SC_WORKERS)
        _load_index_rows(d_hbm, i_vmem, isem, wid, ng, nr)

        @pl.loop(0, nr)
        def _(r):
            g = wid + SC_WORKERS * r

            @pl.when(g < ng)
            def _():
                get = lambda k: pltpu.async_copy(
                    y_hbm.at[i_vmem.at[r, pl.ds(k * SC_GROUP, SC_GROUP)]], bufs[k % 2], gsem[k % 2])
                put = lambda k: pltpu.async_copy(
                    bufs[k % 2], o_hbm.at[pl.ds(g * LANES + k * SC_GROUP, SC_GROUP)], wsem[k % 2])
                gets = [get(0), get(1)]
                puts = []
                for k in range(TOP_K):
                    gets[k].wait()
                    puts.append(put(k))
                    if k + 2 < TOP_K:
                        puts[k].wait()
                        gets.append(get(k + 2))
                for k in range(TOP_K - 2, TOP_K):
                    puts[k].wait()

    return kernel(y, dest_g)


def _combine_kernel(xn_ref, g_ref, y_ref, oh_ref, ot_ref, *, n_head_tiles):
    def emit(o_ref):
        for gi in range(TOKEN_TILE // SC_GROUP):
            rows = slice(gi * SC_GROUP, (gi + 1) * SC_GROUP)
            acc = xn_ref[rows, :]
            for k in range(TOP_K):
                r0 = gi * LANES + k * SC_GROUP
                acc = acc + g_ref[rows, k:k + 1] * y_ref[r0:r0 + SC_GROUP, :]
            o_ref[rows, :] = acc

    pl.when(pl.program_id(0) < n_head_tiles)(lambda: emit(oh_ref))
    pl.when(pl.program_id(0) >= n_head_tiles)(lambda: emit(ot_ref))


def _combine(xn, gates, y4, n_head):
    n = xn.shape[0]
    tm = TOKEN_TILE
    row = lambda r, w: pl.BlockSpec((r, w), lambda i: (i, 0))
    return pl.pallas_call(
        functools.partial(_combine_kernel, n_head_tiles=n_head // tm),
        grid=(n // tm,),
        in_specs=[row(tm, D_MODEL), row(tm, ROUTER_PAD), row(tm * TOP_K, D_MODEL)],
        out_specs=_pair_specs(tm, D_MODEL, n_head),
        out_shape=[jax.ShapeDtypeStruct((n_head, D_MODEL), F32),
                   jax.ShapeDtypeStruct((n - n_head, D_MODEL), F32)],
        compiler_params=_cparams(("arbitrary",)),
        name="moe_combine",
    )(xn, gates, y4)


def _moe(xn, h2, idx, gates, rank, counts, lw, n_head):
    n_tok = xn.shape[0]
    dest, tile_expert, n_used, next_expert, n_tiles = _route(idx[:, :TOP_K], rank[:, :TOP_K], counts)
    dest_g = dest.reshape(n_tok // SC_GROUP, SC_GROUP, TOP_K).transpose(0, 2, 1).reshape(-1, LANES)
    xs = _sc_dispatch(h2, dest_g, n_tiles * MOE_TILE)
    y = _moe_ffn(xs, tile_expert, n_used, next_expert, lw)
    return _combine(xn, gates, _sc_collect(y, dest_g), n_head)


def _block_diag(w):
    g, a, b = w.shape
    out = jnp.zeros((g * a, g * b), w.dtype)
    for i in range(g):
        out = out.at[i * a:(i + 1) * a, i * b:(i + 1) * b].set(w[i])
    return out


def _layer_weights(l, p):
    wr = jnp.pad(p["w_router"][l], ((0, 0), (0, ROUTER_PAD - N_EXPERTS)))
    hd = jnp.arange(D_A) // HEAD_DIM
    return {
        "ln1_g": p["ln1_g"][l][None], "w_in": p["w_in"][l].astype(BF16),
        "g_q": jnp.tile(p["g_q"][l], N_HEADS_A)[None], "g_k": jnp.tile(p["g_k"][l], N_HEADS_A)[None],
        "g_vb": p["g_vb"][l][None],
        "ones_bd": (hd[:, None] == hd[None, :]).astype(BF16),
        "w_s": p["w_s"][l],
        "bias_s": jnp.repeat(p["b_s"][l].T, HEAD_DIM, axis=1),
        "conv_w": p["conv_w"][l], "conv_b": p["conv_b"][l][None],
        "w_a_bd": _block_diag(p["w_a"][l]).astype(BF16), "b_a": p["b_a"][l][None],
        "w_x_bd": _block_diag(p["w_x"][l]).astype(BF16), "b_x": p["b_x"][l][None],
        "lru_lambda": p["lru_lambda"][l][None],
        "g_out": p["g_out"][l][None], "w_out": p["w_out"][l].astype(BF16),
        "ln2_g": p["ln2_g"][l][None],
        "w_router": wr.astype(BF16),
        "b_router": jnp.pad(p["b_router"][l], (0, ROUTER_PAD - N_EXPERTS),
                            constant_values=NEG_BIG)[None],
        "layer": l, "w_gu": p["w_gu"], "w_down": p["w_down"],
        "b_glu": p["b_gu"][l][:, None, 0::2], "b_lin": p["b_gu"][l][:, None, 1::2],
        "b_down": p["b_down"][l][:, None, :],
    }


def kernel(x_prompt, x_sample, cache_win_k, cache_win_v, state_conv, state_lru, ln1_g, w_in, g_q, g_k,
           g_vb, w_s, b_s, conv_w, conv_b, w_a, b_a, w_x, b_x, lru_lambda, g_out, w_out, ln2_g, w_router,
           b_router, w_gu, b_gu, w_down, b_down):
    params = dict(ln1_g=ln1_g, w_in=w_in, g_q=g_q, g_k=g_k, g_vb=g_vb, w_s=w_s, b_s=b_s, conv_w=conv_w,
                  conv_b=conv_b, w_a=w_a, b_a=b_a, w_x=w_x, b_x=b_x, lru_lambda=lru_lambda, g_out=g_out,
                  w_out=w_out, ln2_g=ln2_g, w_router=w_router, b_router=b_router, w_gu=w_gu, b_gu=b_gu,
                  w_down=w_down, b_down=b_down)
    bp, sp, _ = x_prompt.shape
    bs, ss, _ = x_sample.shape
    depth = w_in.shape[0]
    n_p, n_s = bp * sp, bs * ss
    keep = min(DILATED_PATTERNS[-1][0], sp)
    w_buf = cache_win_k.shape[2]
    ckt = cache_win_k.transpose(0, 1, 3, 4, 2).reshape(depth, bs, D_A, w_buf)
    cvt = cache_win_v.transpose(0, 1, 3, 4, 2).reshape(depth, bs, D_A, w_buf)
    x = (x_prompt.reshape(n_p, D_MODEL), x_sample.reshape(n_s, D_MODEL))
    zero_conv = jnp.zeros((bp, SUBLANES, D_C), F32)
    zero_h = jnp.zeros((bp, 1, D_C), F32)
    outs = {name: [] for name in ("pconv", "plru", "sk", "sv", "sconv", "slru", "svb")}
    window = None
    for l in range(depth):
        lw = _layer_weights(l, params)
        q, k, v, bc = _in_proj(x, lw["ln1_g"], lw["w_in"], lw["g_q"], lw["g_k"], lw["g_vb"], lw["ones_bd"])
        oa_p = _attn_prompt(q, k, v, bp, sp)
        ob_p, oc_p, h_p = _mixer_bc(bc, zero_conv, zero_h, lw, bp, sp, MIX_TILE, MIX_TILE - 1)
        oa_s = _attn_sample(q, k, v, ckt, cvt, l, n_p, bs, ss)
        bc_s = bc[n_p:].reshape(bs, ss, -1)
        bc_s_pad = jnp.pad(bc_s, ((0, 0), (0, CHUNK - ss), (0, 0))).reshape(bs * CHUNK, -1)
        conv8 = jnp.pad(state_conv[l], ((0, 0), (SUBLANES - (CONV_W - 1), 0), (0, 0)))
        ob_s, oc_s, h_s = _mixer_bc(bc_s_pad, conv8, state_lru[l][:, None, :], lw, bs, CHUNK, CHUNK, ss - 1)
        ob_s = ob_s.reshape(bs, CHUNK, D_B)[:, :ss].reshape(n_s, D_B)
        oc_s = oc_s.reshape(bs, CHUNK, D_C)[:, :ss].reshape(n_s, D_C)
        xn, h2, idx, gates, rank, counts = _out_proj((oa_p, oa_s), (ob_p, ob_s), (oc_p, oc_s), x, lw)
        x = _moe(xn, h2, idx, gates, rank, counts[:, 0, :N_EXPERTS], lw, n_p)

        window = _export_window(k, v, window, l, depth, bp, sp, keep)
        xc_p = bc[:n_p, 2 * D_B:2 * D_B + D_C].reshape(bp, sp, D_C)
        outs["pconv"].append(xc_p[:, sp - (CONV_W - 1):])
        outs["plru"].append(h_p[:, 0])
        outs["sk"].append(k[n_p:].reshape(bs, ss, N_HEADS_A, HEAD_DIM))
        outs["sv"].append(v[n_p:].reshape(bs, ss, N_HEADS_A, HEAD_DIM))
        xpad_s = jnp.concatenate([state_conv[l], bc_s[:, :, 2 * D_B:2 * D_B + D_C]], axis=1)
        outs["sconv"].append(xpad_s[:, -(CONV_W - 1):])
        outs["slru"].append(h_s[:, 0])
        outs["svb"].append(bc_s[:, :, D_B:2 * D_B])
    y_p = x[0].reshape(bp, sp, D_MODEL)
    y_s = x[1].reshape(bs, ss, D_MODEL)
    st = lambda name: jnp.stack(outs[name])
    heads_last = lambda t: t.reshape(depth, bp, N_HEADS_A, HEAD_DIM, keep).transpose(0, 1, 4, 2, 3)
    return (y_p, y_s, heads_last(window[0]), heads_last(window[1]), st("pconv"), st("plru"), st("sk"), st("sv"), st("sconv"),
            st("slru"), st("svb"))
```
